```python
import jax
import jax.numpy as jnp
from jax import lax
import numpy as np

D_MODEL = 2048
BATCH = 2
SEQ = 8192
DEPTH = 2

GRID_W = 64
CTX_LEN = 256
HEAD_DIM = 128
ROPE_THETA = 10000.0
EPS = 1e-6
NEG = -1e30
N_BRANCH = 4
BRANCH_W = 512
QBLOCK = 128

ATT_HEADS = 4
ATT_KV_HEADS = 2
MLA_HEADS = 4
MLA_Q_LORA = 384
MLA_KV_LORA = 256
MLA_NOPE = 128
MLA_ROPE = 64
MLA_V = 128
WIN_HEADS = 4
WIN_KV_HEADS = 2
WINDOW = 128
NA_HEADS = 4
NA_WIN_H = 8
NA_WIN_W = 16
N_EXPERTS = 16
N_GROUPS = 4
TOPK_GROUPS = 1
TOP_K = 2
D_EXPERT = 512

SPLITS = (ATT_HEADS * HEAD_DIM, ATT_KV_HEADS * HEAD_DIM, ATT_KV_HEADS * HEAD_DIM,
          MLA_Q_LORA, MLA_KV_LORA, MLA_ROPE,
          WIN_HEADS * HEAD_DIM, WIN_KV_HEADS * HEAD_DIM, WIN_KV_HEADS * HEAD_DIM,
          NA_HEADS * HEAD_DIM, NA_HEADS * HEAD_DIM, NA_HEADS * HEAD_DIM)
D_IN = sum(SPLITS)

kernel_name = 'hybrid_gated_mixers_grouped_moe_dit_block'


def rms_norm(x, g):
    xf = x.astype(jnp.float32)
    xf = xf * lax.rsqrt(jnp.mean(xf * xf, axis=-1, keepdims=True) + EPS)
    return (xf * g.astype(jnp.float32)).astype(x.dtype)


def modulate(h, shift, scale):
    return h * (1 + scale) + shift


def heads(t, n_heads):
    return t.reshape(t.shape[:-1] + (n_heads, t.shape[-1] // n_heads))


def rope_1d(x, pos):
    d2 = x.shape[-1] // 2
    freqs = ROPE_THETA ** (-jnp.arange(d2, dtype=jnp.float32) / d2)
    ang = pos.astype(jnp.float32)[:, None] * freqs[None, :]
    cos = jnp.cos(ang)[None, :, None, :].astype(x.dtype)
    sin = jnp.sin(ang)[None, :, None, :].astype(x.dtype)
    x1, x2 = x[..., :d2], x[..., d2:]
    return jnp.concatenate([x1 * cos - x2 * sin, x1 * sin + x2 * cos], axis=-1)


def axial_rope(x, rows, cols):
    h = x.shape[-1] // 2
    return jnp.concatenate([rope_1d(x[..., :h], rows), rope_1d(x[..., h:], cols)], axis=-1)


def mixer_heads(proj, pos, mla_qa_norm, mla_w_uq, mla_kva_norm, mla_w_ukv,
                qn_att, kn_att, qn_mla, kn_mla, qn_win, kn_win, qn_na, kn_na):
    points, acc = [], 0
    for s in SPLITS[:-1]:
        acc += s
        points.append(acc)
    (a_q, a_k, a_v, m_cq, m_ckv, m_kr, w_q, w_k, w_v, n_q, n_k, n_v) = jnp.split(proj, points, axis=-1)
    if pos is None:
        rope = lambda t: t
    else:
        rope = lambda t: axial_rope(t, pos[0], pos[1])
    att = (rope(rms_norm(heads(a_q, ATT_HEADS), qn_att)),
           rope(rms_norm(heads(a_k, ATT_KV_HEADS), kn_att)),
           heads(a_v, ATT_KV_HEADS))
    q = heads(rms_norm(m_cq, mla_qa_norm) @ mla_w_uq, MLA_HEADS)
    kv = heads(rms_norm(m_ckv, mla_kva_norm) @ mla_w_ukv, MLA_HEADS)
    k_nope, v_m = kv[..., :MLA_NOPE], kv[..., MLA_NOPE:]
    k_rope = jnp.broadcast_to(m_kr[..., None, :], k_nope.shape[:-1] + (MLA_ROPE,))
    q = rms_norm(q, qn_mla)
    k = rms_norm(jnp.concatenate([k_nope, k_rope], axis=-1), kn_mla)
    q = jnp.concatenate([q[..., :MLA_NOPE], rope(q[..., MLA_NOPE:])], axis=-1)
    k = jnp.concatenate([k[..., :MLA_NOPE], rope(k[..., MLA_NOPE:])], axis=-1)
    mla = (q, k, v_m)
    win = (rope(rms_norm(heads(w_q, WIN_HEADS), qn_win)),
           rope(rms_norm(heads(w_k, WIN_KV_HEADS), kn_win)),
           heads(w_v, WIN_KV_HEADS))
    na = (rms_norm(heads(n_q, NA_HEADS), qn_na),
          rms_norm(heads(n_k, NA_HEADS), kn_na),
          heads(n_v, NA_HEADS))
    return att, mla, win, na


def dense_gqa(q, k, v, sink=None):
    b, lq, hq, d = q.shape
    hkv = k.shape[2]
    g = hq // hkv
    qg = q.reshape(b, lq, hkv, g, d)
    s = jnp.einsum('bqkgd,bskd->bkgqs', qg, k).astype(jnp.float32) * (d ** -0.5)
    if sink is None:
        p = jax.nn.softmax(s, axis=-1)
    else:
        s_sink = jnp.broadcast_to(sink.astype(jnp.float32).reshape(1, hkv, g, 1, 1), s.shape[:-1] + (1,))
        p = jax.nn.softmax(jnp.concatenate([s, s_sink], axis=-1), axis=-1)[..., :-1]
    o = jnp.einsum('bkgqs,bskd->bqkgd', p.astype(v.dtype), v)
    return o.reshape(b, lq, hq * v.shape[-1])


def blocked_dense_gqa(q, k_all, v_all):
    b, n, hq, d = q.shape
    qb = q.reshape(b, n // QBLOCK, QBLOCK, hq, d).swapaxes(0, 1)
    o = lax.map(lambda qi: dense_gqa(qi, k_all, v_all), qb)
    return o.swapaxes(0, 1).reshape(b, n, -1)


def windowed_gqa(q, k, v, k_ctx, v_ctx, sink):
    b, n, hq, d = q.shape
    hkv = k.shape[2]
    g = hq // hkv
    wb = WINDOW
    nb = n // wb
    n_ctx = k_ctx.shape[1]
    scale = d ** -0.5

    def band(t):
        tp = jnp.pad(t, ((0, 0), (wb, wb), (0, 0), (0, 0))).reshape(b, nb + 2, wb, hkv, t.shape[-1])
        return jnp.concatenate([tp[:, :-2], tp[:, 1:-1], tp[:, 2:]], axis=2)

    k_band, v_band = band(k), band(v)
    qb = q.reshape(b, nb, wb, hkv, g, d)
    s_loc = jnp.einsum('bnqkgd,bnskd->bnkgqs', qb, k_band).astype(jnp.float32) * scale
    blk = jnp.arange(nb)[:, None, None] * wb
    q_pos = blk + jnp.arange(wb)[None, :, None]
    k_pos = blk - wb + jnp.arange(3 * wb)[None, None, :]
    valid = (jnp.abs(q_pos - k_pos) <= WINDOW) & (k_pos >= 0) & (k_pos < n)
    s_loc = jnp.where(valid[None, :, None, None], s_loc, NEG)
    s_ctx = jnp.einsum('bnqkgd,bskd->bnkgqs', qb, k_ctx).astype(jnp.float32) * scale
    s_sink = jnp.broadcast_to(sink.astype(jnp.float32).reshape(1, 1, hkv, g, 1, 1), s_loc.shape[:-1] + (1,))
    p = jax.nn.softmax(jnp.concatenate([s_loc, s_ctx, s_sink], axis=-1), axis=-1).astype(v.dtype)
    p_loc, p_ctx = p[..., :3 * wb], p[..., 3 * wb:3 * wb + n_ctx]
    o = (jnp.einsum('bnkgqs,bnskd->bnqkgd', p_loc, v_band)
         + jnp.einsum('bnkgqs,bskd->bnqkgd', p_ctx, v_ctx))
    return o.reshape(b, n, hq * v.shape[-1])


def neighbourhood_attn(q, k, v, k_ctx, v_ctx, rpb):
    b, n, h, d = q.shape
    n_rows = n // GRID_W
    kh = min(NA_WIN_H, n_rows)
    kw = NA_WIN_W
    qg = q.reshape(b, n_rows, GRID_W, h, d)
    kg = k.reshape(b, n_rows, GRID_W, h, d)
    vg = v.reshape(b, n_rows, GRID_W, h, v.shape[-1])
    cols = jnp.arange(GRID_W)
    col_idx = jnp.clip(cols - kw // 2, 0, GRID_W - kw)[:, None] + jnp.arange(kw)[None, :]
    dx = col_idx - cols[:, None] + (NA_WIN_W - 1)
    scale = d ** -0.5

    def row(r):
        rs = jnp.clip(r - kh // 2, 0, n_rows - kh)
        k_nb = lax.dynamic_slice_in_dim(kg, rs, kh, axis=1)[:, :, col_idx]
        v_nb = lax.dynamic_slice_in_dim(vg, rs, kh, axis=1)[:, :, col_idx]
        q_r = lax.dynamic_index_in_dim(qg, r, axis=1, keepdims=False)
        dy = rs + jnp.arange(kh) - r + (NA_WIN_H - 1)
        bias = rpb[:, dy][:, :, dx].transpose(0, 2, 1, 3)
        s_nb = jnp.einsum('bwhd,bywkhd->bhwyk', q_r, k_nb).astype(jnp.float32) * scale
        s_nb = s_nb + bias[None].astype(jnp.float32)
        s_ctx = jnp.einsum('bwhd,bshd->bhws', q_r, k_ctx).astype(jnp.float32) * scale
        s = jnp.concatenate([s_nb.reshape(b, h, GRID_W, kh * kw), s_ctx], axis=-1)
        p = jax.nn.softmax(s, axis=-1).astype(v.dtype)
        p_nb = p[..., :kh * kw].reshape(b, h, GRID_W, kh, kw)
        p_ctx = p[..., kh * kw:]
        return (jnp.einsum('bhwyk,bywkhd->bwhd', p_nb, v_nb)
                + jnp.einsum('bhws,bshd->bwhd', p_ctx, v_ctx))

    o = lax.map(row, jnp.arange(n_rows))
    return o.swapaxes(0, 1).reshape(b, n, h * v.shape[-1])


def merge_branches(h, outs, w_branch, w_gate, b_gate, w_out):
    acc = None
    for i, o in enumerate(outs):
        y = jax.nn.sigmoid(h @ w_gate[i] + b_gate[i]) * (o @ w_branch[i])
        acc = y if acc is None else acc + y
    return acc @ w_out


def token_mixer(h, hc, need_ctx, w_in, head_params, win_sink, na_rpb, merge_params):
    n = h.shape[1]
    t = jnp.arange(n)
    pos = (t // GRID_W, t % GRID_W)
    att, mla, win, na = mixer_heads(h @ w_in, pos, *head_params)
    att_c, mla_c, win_c, na_c = mixer_heads(hc @ w_in, None, *head_params)
    cat = lambda a, bb: jnp.concatenate([a, bb], axis=1)
    outs = (blocked_dense_gqa(att[0], cat(att_c[1], att[1]), cat(att_c[2], att[2])),
            blocked_dense_gqa(mla[0], cat(mla_c[1], mla[1]), cat(mla_c[2], mla[2])),
            windowed_gqa(win[0], win[1], win[2], win_c[1], win_c[2], win_sink),
            neighbourhood_attn(na[0], na[1], na[2], na_c[1], na_c[2], na_rpb))
    y = merge_branches(h, outs, *merge_params)
    if not need_ctx:
        return y, None
    outs_c = (dense_gqa(*att_c), dense_gqa(*mla_c), dense_gqa(*win_c, sink=win_sink), dense_gqa(*na_c))
    yc = merge_branches(hc, outs_c, *merge_params)
    return y, yc


def moe(h, w_router, b_router, w1, w3, w2):
    shape = h.shape
    t = h.reshape(-1, shape[-1])
    scores = jax.nn.sigmoid((t @ w_router).astype(jnp.float32))
    biased = scores + b_router.astype(jnp.float32)
    grp = biased.reshape(-1, N_GROUPS, N_EXPERTS // N_GROUPS)
    grp_score = lax.top_k(grp, 2)[0].sum(axis=-1)
    _, g_idx = lax.top_k(grp_score, TOPK_GROUPS)
    g_mask = jax.nn.one_hot(g_idx, N_GROUPS, dtype=jnp.float32).sum(axis=-2)
    e_mask = jnp.repeat(g_mask, N_EXPERTS // N_GROUPS, axis=-1)
    _, e_idx = lax.top_k(jnp.where(e_mask > 0, biased, NEG), TOP_K)
    w = jnp.take_along_axis(scores, e_idx, axis=-1)
    w = w / jnp.sum(w, axis=-1, keepdims=True)
    combine = jnp.sum(jax.nn.one_hot(e_idx, N_EXPERTS, dtype=jnp.float32) * w[..., None], axis=-2).astype(t.dtype)
    out = jnp.zeros_like(t)
    for e in range(N_EXPERTS):
        hid = jax.nn.silu(t @ w1[e]) * (t @ w3[e])
        out = out + combine[:, e:e + 1] * (hid @ w2[e])
    return out.reshape(shape)


def setup_inputs(seed: int = 0) -> dict:
    key = jax.random.key(seed)
    keys = jax.random.split(key, 32)

    def nrm(i, shape, scale):
        return jax.random.normal(keys[i], shape, jnp.float32) * scale

    def gain(i, shape):
        return 1.0 + nrm(i, shape, 0.02)

    L, D = DEPTH, D_MODEL
    return {
        'x': nrm(0, (BATCH, SEQ, D), 1.0),
        'c': nrm(1, (BATCH, D), 1.0),
        'ctx': nrm(2, (BATCH, CTX_LEN, D), 1.0),
        'c_ctx': nrm(3, (D,), 1.0),
        'w_mod': nrm(4, (L, D, 6 * D), 0.5 * D ** -0.5),
        'b_mod': nrm(5, (L, 6 * D), 0.02),
        'norm_mix': gain(6, (L, D)),
        'norm_ffn': gain(7, (L, D)),
        'w_in': nrm(8, (L, D, D_IN), D ** -0.5),
        'mla_qa_norm': gain(9, (L, MLA_Q_LORA)),
        'mla_w_uq': nrm(10, (L, MLA_Q_LORA, MLA_HEADS * (MLA_NOPE + MLA_ROPE)), MLA_Q_LORA ** -0.5),
        'mla_kva_norm': gain(11, (L, MLA_KV_LORA)),
        'mla_w_ukv': nrm(12, (L, MLA_KV_LORA, MLA_HEADS * (MLA_NOPE + MLA_V)), MLA_KV_LORA ** -0.5),
        'qn_att': gain(13, (L, HEAD_DIM)),
        'kn_att': gain(14, (L, HEAD_DIM)),
        'qn_mla': gain(15, (L, MLA_NOPE + MLA_ROPE)),
        'kn_mla': gain(16, (L, MLA_NOPE + MLA_ROPE)),
        'qn_win': gain(17, (L, HEAD_DIM)),
        'kn_win': gain(18, (L, HEAD_DIM)),
        'qn_na': gain(19, (L, HEAD_DIM)),
        'kn_na': gain(20, (L, HEAD_DIM)),
        'win_sink': nrm(21, (L, WIN_HEADS), 1.0),
        'na_rpb': nrm(22, (L, NA_HEADS, 2 * NA_WIN_H - 1, 2 * NA_WIN_W - 1), 0.5),
        'w_branch': nrm(23, (L, N_BRANCH, BRANCH_W, D), BRANCH_W ** -0.5),
        'w_gate': nrm(24, (L, N_BRANCH, D, D), D ** -0.5),
        'b_gate': nrm(25, (L, N_BRANCH, D), 0.02),
        'w_out': nrm(26, (L, D, D), D ** -0.5),
        'w_router': nrm(27, (D, N_EXPERTS), D ** -0.5),
        'b_router': nrm(28, (N_EXPERTS,), 0.01),
        'moe_w1': nrm(29, (L, N_EXPERTS, D, D_EXPERT), D ** -0.5),
        'moe_w3': nrm(30, (L, N_EXPERTS, D, D_EXPERT), D ** -0.5),
        'moe_w2': nrm(31, (L, N_EXPERTS, D_EXPERT, D), D_EXPERT ** -0.5),
    }


def reference(x, c, ctx, c_ctx, w_mod, b_mod, norm_mix, norm_ffn, w_in,
              mla_qa_norm, mla_w_uq, mla_kva_norm, mla_w_ukv,
              qn_att, kn_att, qn_mla, kn_mla, qn_win, kn_win, qn_na, kn_na,
              win_sink, na_rpb, w_branch, w_gate, b_gate, w_out,
              w_router, b_router, moe_w1, moe_w3, moe_w2):
    c_s = jax.nn.silu(c)
    cc_s = jax.nn.silu(c_ctx)
    xc = ctx
    for l in range(DEPTH):
        need_ctx = l < DEPTH - 1
        mod = c_s @ w_mod[l] + b_mod[l]
        mod_c = cc_s @ w_mod[l] + b_mod[l]
        sh1, sc1, g1, sh2, sc2, g2 = jnp.split(mod[:, None, :], 6, axis=-1)
        sh1c, sc1c, g1c, sh2c, sc2c, g2c = jnp.split(mod_c, 6, axis=-1)
        head_params = (mla_qa_norm[l], mla_w_uq[l], mla_kva_norm[l], mla_w_ukv[l],
                       qn_att[l], kn_att[l], qn_mla[l], kn_mla[l],
                       qn_win[l], kn_win[l], qn_na[l], kn_na[l])
        merge_params = (w_branch[l], w_gate[l], b_gate[l], w_out[l])
        h = modulate(rms_norm(x, norm_mix[l]), sh1, sc1)
        hc = modulate(rms_norm(xc, norm_mix[l]), sh1c, sc1c)
        y, yc = token_mixer(h, hc, need_ctx, w_in[l], head_params, win_sink[l], na_rpb[l], merge_params)
        x = x + g1 * y
        h2 = modulate(rms_norm(x, norm_ffn[l]), sh2, sc2)
        x = x + g2 * moe(h2, w_router, b_router, moe_w1[l], moe_w3[l], moe_w2[l])
        if need_ctx:
            xc = xc + g1c * yc
            hc2 = modulate(rms_norm(xc, norm_ffn[l]), sh2c, sc2c)
            xc = xc + g2c * moe(hc2, w_router, b_router, moe_w1[l], moe_w3[l], moe_w2[l])
    return x
```

```python
import functools

import numpy as np
import jax
import jax.numpy as jnp
from jax import lax
from jax.experimental import pallas as pl
from jax.experimental.pallas import tpu as pltpu

BF = jnp.bfloat16
F32 = jnp.float32

GRID_W = 64
HEAD_DIM = 128
ROPE_THETA = 10000.0
EPS = 1e-6
NEG = -1e30
WINDOW = 128
NA_WIN_H = 8
NA_WIN_W = 16
N_EXPERTS = 16
N_GROUPS = 4
MLA_QK = 192
MLA_PAD = 256
LANES = 128
VMEM_LIMIT = 56 * 1024 * 1024

P_AQ, P_AK, P_AV = 0, 512, 768
P_MCQ, P_MCKV, P_MKR = 1024, 1408, 1664
P_WQ, P_WK, P_WV = 1792, 2304, 2560
P_NQ, P_NK, P_NV = 2816, 3328, 3840
P_TOTAL = 4352
KR_END = 1728


def _cparams(sem):
    return pltpu.CompilerParams(dimension_semantics=sem, vmem_limit_bytes=VMEM_LIMIT)


def _sigmoid(z):
    return 1.0 / (1.0 + jnp.exp(-z))


def _dot(a, b):
    return jnp.dot(a, b, preferred_element_type=F32)


def _dot_nt(a, b):
    return lax.dot_general(a, b, (((1,), (1,)), ((), ())), preferred_element_type=F32)


def _mod_kernel(c_ref, w_ref, b_ref, o_ref):
    c = c_ref[...]
    s = c * _sigmoid(c)
    o_ref[...] = _dot(s.astype(BF), w_ref[...].astype(BF)) + b_ref[...]


def _mod_call(cvec, w_mod, b_mod, tn):
    L, D, D6 = w_mod.shape
    return pl.pallas_call(
        _mod_kernel,
        grid=(L, D6 // tn),
        in_specs=[
            pl.BlockSpec((8, D), lambda l, j: (0, 0)),
            pl.BlockSpec((None, D, tn), lambda l, j: (l, 0, j)),
            pl.BlockSpec((None, 1, tn), lambda l, j: (l, 0, j)),
        ],
        out_specs=pl.BlockSpec((None, 8, tn), lambda l, j: (l, 0, j)),
        out_shape=jax.ShapeDtypeStruct((L, 8, D6), F32),
        compiler_params=_cparams(("parallel", "parallel")),
        name="adaln_mod",
    )(cvec, w_mod, b_mod.reshape(L, 1, D6))


def _mod_spec(comp, tiles_per_batch, n_batch, D):
    return pl.BlockSpec(
        (None, None, 1, D),
        lambda i, *_: (jnp.minimum(i // tiles_per_batch, n_batch), comp, 0, 0))


def _proj_kernel(x_ref, g_ref, sh_ref, sc_ref, w_ref, h_ref, o_ref, h_scr):
    @pl.when(pl.program_id(1) == 0)
    def _():
        x = x_ref[...]
        ms = jnp.mean(x * x, axis=-1, keepdims=True)
        xn = x * lax.rsqrt(ms + EPS) * g_ref[...]
        h = (xn * (1.0 + sc_ref[...]) + sh_ref[...]).astype(BF)
        h_scr[...] = h
        h_ref[...] = h

    o_ref[...] = _dot(h_scr[...], w_ref[...])


def _proj_call(X, norm_g, modl, w_in_p, n_batch, n_lat, tm, tn):
    R, D = X.shape
    P = w_in_p.shape[1]
    tpb = n_lat // tm
    return pl.pallas_call(
        _proj_kernel,
        grid=(R // tm, P // tn),
        in_specs=[
            pl.BlockSpec((tm, D), lambda i, j: (i, 0)),
            pl.BlockSpec((1, D), lambda i, j: (0, 0)),
            _mod_spec(0, tpb, n_batch, D),
            _mod_spec(1, tpb, n_batch, D),
            pl.BlockSpec((D, tn), lambda i, j: (0, j)),
        ],
        out_specs=[
            pl.BlockSpec((tm, D), lambda i, j: (i, 0)),
            pl.BlockSpec((tm, tn), lambda i, j: (i, j)),
        ],
        out_shape=[jax.ShapeDtypeStruct((R, D), BF), jax.ShapeDtypeStruct((R, P), F32)],
        scratch_shapes=[pltpu.VMEM((tm, D), BF)],
        compiler_params=_cparams(("parallel", "arbitrary")),
        name="norm_mod_proj",
    )(X, norm_g.reshape(1, D), modl, modl, w_in_p)


def _rms(x, g, n):
    ss = jnp.sum(x * x, axis=-1, keepdims=True)
    return x * lax.rsqrt(ss / n + EPS) * g


def _rope(x, cos, sin_signed, half):
    lane = lax.broadcasted_iota(jnp.int32, x.shape, 1)
    first = (lane & (2 * half - 1)) < half
    xr = jnp.where(first, pltpu.roll(x, LANES - half, 1), pltpu.roll(x, half, 1))
    return x * cos + xr * sin_signed


def _heads_kernel(p_ref, ca_ref, sa_ref, cm_ref, sm_ref,
                  qa_g, ka_g, qw_g, kw_g, qn_g, kn_g, qm_g, km_g, cqn_g, ckvn_g, wuq_ref, wukv_ref,
                  qa_ref, ka_ref, va_ref, qm_ref, km_ref, vm_ref,
                  qw_ref, kw_ref, vw_ref, qn_ref, kn_ref, vn_ref):
    hd = HEAD_DIM
    sc128 = HEAD_DIM ** -0.5
    sc192 = MLA_QK ** -0.5
    ca, sa, cm, sm = ca_ref[...], sa_ref[...], cm_ref[...], sm_ref[...]

    def sl(off, h, w=hd):
        return p_ref[:, off + h * w: off + (h + 1) * w]

    for (q_off, k_off, v_off, qg, kg, q_out, k_out, v_out) in (
            (P_AQ, P_AK, P_AV, qa_g, ka_g, qa_ref, ka_ref, va_ref),
            (P_WQ, P_WK, P_WV, qw_g, kw_g, qw_ref, kw_ref, vw_ref)):
        for h in range(4):
            q = _rope(_rms(sl(q_off, h), qg[...], hd), ca, sa, 32)
            q_out[:, h * hd:(h + 1) * hd] = (q * sc128).astype(BF)
        for h in range(2):
            k = _rope(_rms(sl(k_off, h), kg[...], hd), ca, sa, 32)
            k_out[:, h * hd:(h + 1) * hd] = k.astype(BF)
        v_out[...] = p_ref[:, v_off:v_off + 2 * hd].astype(BF)

    for h in range(4):
        qn_ref[:, h * hd:(h + 1) * hd] = (_rms(sl(P_NQ, h), qn_g[...], hd) * sc128).astype(BF)
        kn_ref[:, h * hd:(h + 1) * hd] = _rms(sl(P_NK, h), kn_g[...], hd).astype(BF)
    vn_ref[...] = p_ref[:, P_NV:P_NV + 4 * hd].astype(BF)

    cq = _rms(p_ref[:, P_MCQ:P_MCQ + 384], cqn_g[...], 384)
    qf = _dot(cq.astype(BF), wuq_ref[...])
    ckv = _rms(p_ref[:, P_MCKV:P_MCKV + 256], ckvn_g[...], 256)
    kvf = _dot(ckv.astype(BF), wukv_ref[...])
    kr = p_ref[:, P_MKR:P_MKR + hd]
    kr_ss = jnp.sum(kr * kr, axis=-1, keepdims=True)
    qg0, qg1 = qm_g[:, :hd], qm_g[:, hd:]
    kg0, kg1 = km_g[:, :hd], km_g[:, hd:]
    for h in range(4):
        q0 = qf[:, h * MLA_PAD: h * MLA_PAD + hd]
        q1 = qf[:, h * MLA_PAD + hd: (h + 1) * MLA_PAD]
        ss = jnp.sum(q0 * q0, axis=-1, keepdims=True) + jnp.sum(q1 * q1, axis=-1, keepdims=True)
        r = lax.rsqrt(ss / MLA_QK + EPS)
        qm_ref[:, h * MLA_PAD: h * MLA_PAD + hd] = (q0 * r * qg0 * sc192).astype(BF)
        qm_ref[:, h * MLA_PAD + hd: (h + 1) * MLA_PAD] = (
            _rope(q1 * r * qg1, cm, sm, 16) * sc192).astype(BF)
        k0 = kvf[:, h * 2 * hd: h * 2 * hd + hd]
        ss = jnp.sum(k0 * k0, axis=-1, keepdims=True) + kr_ss
        r = lax.rsqrt(ss / MLA_QK + EPS)
        km_ref[:, h * MLA_PAD: h * MLA_PAD + hd] = (k0 * r * kg0).astype(BF)
        km_ref[:, h * MLA_PAD + hd: (h + 1) * MLA_PAD] = _rope(kr * r * kg1, cm, sm, 16).astype(BF)
        vm_ref[:, h * hd:(h + 1) * hd] = kvf[:, h * 2 * hd + hd: (h + 1) * 2 * hd].astype(BF)


def _heads_call(proj, tabs, gains, wuq_p, wukv, n_batch, n_lat, tm):
    R, P = proj.shape
    tpb = n_lat // tm
    tab_spec = pl.BlockSpec((tm, LANES), lambda i: (jnp.where(i < n_batch * tpb, i % tpb, tpb), 0))

    def full(a):
        return pl.BlockSpec(a.shape, lambda i: (0,) * a.ndim)

    widths = (512, 256, 256, 4 * MLA_PAD, 4 * MLA_PAD, 512, 512, 256, 256, 512, 512, 512)
    return pl.pallas_call(
        _heads_kernel,
        grid=(R // tm,),
        in_specs=[pl.BlockSpec((tm, P), lambda i: (i, 0))] + [tab_spec] * 4
        + [full(g) for g in gains] + [full(wuq_p), full(wukv)],
        out_specs=[pl.BlockSpec((tm, w), lambda i: (i, 0)) for w in widths],
        out_shape=[jax.ShapeDtypeStruct((R, w), BF) for w in widths],
        compiler_params=_cparams(("parallel",)),
        name="head_prep",
    )(proj, *tabs, *gains, wuq_p, wukv)


def _flash_kernel(q_ref, k_ref, v_ref, kc_ref, vc_ref, o_ref, m_scr, l_scr, acc_scr, *, G, d, dv):
    kv = pl.program_id(3)

    @pl.when(kv == 0)
    def _():
        for g in range(G):
            s = _dot_nt(q_ref[:, g * d:(g + 1) * d], kc_ref[...])
            m = jnp.max(s, axis=-1, keepdims=True)
            p = jnp.exp(s - m)
            m_scr[g] = m
            l_scr[g] = jnp.sum(p, axis=-1, keepdims=True)
            acc_scr[g] = _dot(p.astype(BF), vc_ref[...])

    for g in range(G):
        s = _dot_nt(q_ref[:, g * d:(g + 1) * d], k_ref[...])
        m_prev = m_scr[g]
        m_new = jnp.maximum(m_prev, jnp.max(s, axis=-1, keepdims=True))
        alpha = jnp.exp(m_prev - m_new)
        p = jnp.exp(s - m_new)
        l_scr[g] = alpha * l_scr[g] + jnp.sum(p, axis=-1, keepdims=True)
        acc_scr[g] = alpha * acc_scr[g] + _dot(p.astype(BF), v_ref[...])
        m_scr[g] = m_new

    @pl.when(kv == pl.num_programs(3) - 1)
    def _():
        for g in range(G):
            o_ref[:, g * dv:(g + 1) * dv] = (acc_scr[g] / l_scr[g]).astype(BF)


def _flash_call(q, k, v, n_batch, n_lat, n_ctx, G, d, dv, tq, tk, name):
    hkv = k.shape[1] // d
    nq, nk = n_lat // tq, n_lat // tk
    cb = (n_batch * n_lat) // n_ctx
    return pl.pallas_call(
        functools.partial(_flash_kernel, G=G, d=d, dv=dv),
        grid=(n_batch, hkv, nq, nk),
        in_specs=[
            pl.BlockSpec((tq, G * d), lambda b, h, i, j: (b * nq + i, h)),
            pl.BlockSpec((tk, d), lambda b, h, i, j: (b * nk + j, h)),
            pl.BlockSpec((tk, dv), lambda b, h, i, j: (b * nk + j, h)),
            pl.BlockSpec((n_ctx, d), lambda b, h, i, j: (cb + b, h)),
            pl.BlockSpec((n_ctx, dv), lambda b, h, i, j: (cb + b, h)),
        ],
        out_specs=pl.BlockSpec((tq, G * dv), lambda b, h, i, j: (b * nq + i, h)),
        out_shape=jax.ShapeDtypeStruct((n_batch * n_lat, hkv * G * dv), BF),
        scratch_shapes=[pltpu.VMEM((G, tq, 1), F32), pltpu.VMEM((G, tq, 1), F32),
                        pltpu.VMEM((G, tq, dv), F32)],
        compiler_params=_cparams(("parallel", "parallel", "parallel", "arbitrary")),
        name=name,
    )(q, k, v, k, v)


def _window_kernel(sink_ref, q_ref, kp_ref, km_ref, kn_ref, vp_ref, vm_ref, vn_ref, kc_ref, vc_ref,
                   o_ref, *, G):
    hd = HEAD_DIM
    h = pl.program_id(1)
    i = pl.program_id(2)
    ni = pl.num_programs(2)
    tq = q_ref.shape[0]
    kcat = jnp.concatenate([kp_ref[...], km_ref[...], kn_ref[...]], axis=0)
    vcat = jnp.concatenate([vp_ref[...], vm_ref[...], vn_ref[...]], axis=0)
    r = lax.broadcasted_iota(jnp.int32, (tq, tq + 2 * WINDOW), 0)
    c = lax.broadcasted_iota(jnp.int32, (tq, tq + 2 * WINDOW), 1)
    rel = c - WINDOW - r
    lo = jnp.where(i > 0, 0, WINDOW)
    hi = jnp.where(i < ni - 1, tq + 2 * WINDOW, tq + WINDOW)
    valid = (jnp.abs(rel) <= WINDOW) & (c >= lo) & (c < hi)
    for g in range(G):
        q = q_ref[:, g * hd:(g + 1) * hd]
        s = jnp.where(valid, _dot_nt(q, kcat), NEG)
        sc = _dot_nt(q, kc_ref[...])
        sk = sink_ref[h * G + g]
        m = jnp.maximum(jnp.max(s, axis=-1, keepdims=True), jnp.max(sc, axis=-1, keepdims=True))
        m = jnp.maximum(m, sk)
        p = jnp.exp(s - m)
        pc = jnp.exp(sc - m)
        l = jnp.sum(p, axis=-1, keepdims=True) + jnp.sum(pc, axis=-1, keepdims=True) + jnp.exp(sk - m)
        o = _dot(p.astype(BF), vcat) + _dot(pc.astype(BF), vc_ref[...])
        o_ref[:, g * hd:(g + 1) * hd] = (o / l).astype(BF)


def _window_call(sink, q, k, v, n_batch, n_lat, n_ctx, G, tq):
    hd = HEAD_DIM
    hkv = k.shape[1] // hd
    nq = n_lat // tq
    bpt = tq // WINDOW
    nblk = k.shape[0] // WINDOW
    cb = (n_batch * n_lat) // n_ctx

    def prev(b, h, i):
        return (jnp.maximum((b * nq + i) * bpt - 1, 0), h)

    def nxt(b, h, i):
        return (jnp.minimum((b * nq + i + 1) * bpt, nblk - 1), h)

    def main(b, h, i):
        return (b * nq + i, h)

    def ctxb(b, h, i):
        return (cb + b, h)

    return pl.pallas_call(
        functools.partial(_window_kernel, G=G),
        grid=(n_batch, hkv, nq),
        in_specs=[
            pl.BlockSpec(memory_space=pltpu.SMEM),
            pl.BlockSpec((tq, G * hd), main),
            pl.BlockSpec((WINDOW, hd), prev), pl.BlockSpec((tq, hd), main), pl.BlockSpec((WINDOW, hd), nxt),
            pl.BlockSpec((WINDOW, hd), prev), pl.BlockSpec((tq, hd), main), pl.BlockSpec((WINDOW, hd), nxt),
            pl.BlockSpec((n_ctx, hd), ctxb), pl.BlockSpec((n_ctx, hd), ctxb),
        ],
        out_specs=pl.BlockSpec((tq, G * hd), main),
        out_shape=jax.ShapeDtypeStruct((n_batch * n_lat, hkv * G * hd), BF),
        compiler_params=_cparams(("parallel", "parallel", "parallel")),
        name="window_attn",
    )(sink, q, k, k, k, v, v, v, k, v)


NA_TQ = 8 * GRID_W
NA_KB = 4 * GRID_W
NA_NKB = 4


def _na_kernel(q_ref, k0, k1, k2, k3, v0, v1, v2, v3, kc_ref, vc_ref, bias_ref, o_ref):
    q = q_ref[...]
    kcat = jnp.concatenate([k0[...], k1[...], k2[...], k3[...]], axis=0)
    vcat = jnp.concatenate([v0[...], v1[...], v2[...], v3[...]], axis=0)
    s = _dot_nt(q, kcat) + bias_ref[...]
    sc = _dot_nt(q, kc_ref[...])
    m = jnp.maximum(jnp.max(s, axis=-1, keepdims=True), jnp.max(sc, axis=-1, keepdims=True))
    p = jnp.exp(s - m)
    pc = jnp.exp(sc - m)
    l = jnp.sum(p, axis=-1, keepdims=True) + jnp.sum(pc, axis=-1, keepdims=True)
    o = _dot(p.astype(BF), vcat) + _dot(pc.astype(BF), vc_ref[...])
    o_ref[...] = (o / l).astype(BF)


def _na_bias_tables(rpb, n_rows):
    H = rpb.shape[0]
    J = n_rows // 8
    nb = n_rows // 4
    kk = np.arange(NA_NKB * NA_KB)
    blk, within = kk // NA_KB, kk % NA_KB
    krl, kc = within // GRID_W, within % GRID_W
    qr = np.arange(8)
    cq = np.arange(GRID_W)
    cs = np.clip(cq - NA_WIN_W // 2, 0, GRID_W - NA_WIN_W)
    dx = np.arange(GRID_W)[None, :] - cq[:, None] + NA_WIN_W - 1
    col_ok = (np.arange(GRID_W)[None, :] >= cs[:, None]) & (np.arange(GRID_W)[None, :] < cs[:, None] + NA_WIN_W)
    dx1h = np.zeros((2 * NA_WIN_W - 1, GRID_W * GRID_W), np.float32)
    dxc = np.clip(dx, 0, 2 * NA_WIN_W - 2).reshape(-1)
    dx1h[dxc, np.arange(GRID_W * GRID_W)] = 1.0
    out = []
    for jv in (0, min(1, J - 1), J - 1):
        r = 8 * jv + qr
        kb_un = 2 * jv - 1 + np.arange(NA_NKB)
        kb = np.clip(kb_un, 0, nb - 1)
        dup = kb != kb_un
        krow = (4 * kb[:, None] + np.arange(4)[None, :]).reshape(-1)
        krow_dup = np.repeat(dup, 4)
        rs = np.clip(r - NA_WIN_H // 2, 0, n_rows - NA_WIN_H)
        row_ok = (krow[None, :] >= rs[:, None]) & (krow[None, :] < rs[:, None] + NA_WIN_H) & ~krow_dup[None, :]
        dy = np.clip(krow[None, :] - r[:, None] + NA_WIN_H - 1, 0, 2 * NA_WIN_H - 2)
        dy1h = np.zeros((8 * 16, 2 * NA_WIN_H - 1), np.float32)
        dy1h[np.arange(8 * 16), dy.reshape(-1)] = 1.0
        t = jnp.einsum("ay,hyx,xb->hab", jnp.asarray(dy1h), rpb.astype(F32), jnp.asarray(dx1h),
                       precision=lax.Precision.HIGHEST)
        t = t.reshape(H, 8, 16, GRID_W, GRID_W)
        ok = row_ok[:, :, None, None] & col_ok[None, None, :, :]
        t = jnp.where(jnp.asarray(ok)[None], t, NEG)
        t = t.transpose(0, 1, 3, 2, 4).reshape(H, NA_TQ, NA_NKB * NA_KB)
        out.append(t)
    del krl, kc, blk
    return jnp.stack(out, axis=1)


def _na_call(q, k, v, bias, n_batch, n_lat, n_ctx):
    hd = HEAD_DIM
    H = q.shape[1] // hd
    J = n_lat // NA_TQ
    nb = n_lat // NA_KB
    cb = (n_batch * n_lat) // n_ctx

    def kblk(t):
        return lambda b, h, j: (b * nb + jnp.clip(2 * j - 1 + t, 0, nb - 1), h)

    def qmap(b, h, j):
        return (b * J + j, h)

    def ctxb(b, h, j):
        return (cb + b, h)

    def bmap(b, h, j):
        return (h, jnp.where(j == 0, 0, jnp.where(j == J - 1, 2, 1)), 0, 0)

    return pl.pallas_call(
        _na_kernel,
        grid=(n_batch, H, J),
        in_specs=[pl.BlockSpec((NA_TQ, hd), qmap)]
        + [pl.BlockSpec((NA_KB, hd), kblk(t)) for t in range(NA_NKB)]
        + [pl.BlockSpec((NA_KB, hd), kblk(t)) for t in range(NA_NKB)]
        + [pl.BlockSpec((n_ctx, hd), ctxb), pl.BlockSpec((n_ctx, hd), ctxb),
           pl.BlockSpec((None, None, NA_TQ, NA_NKB * NA_KB), bmap)],
        out_specs=pl.BlockSpec((NA_TQ, hd), qmap),
        out_shape=jax.ShapeDtypeStruct((n_batch * n_lat, H * hd), BF),
        compiler_params=_cparams(("parallel", "parallel", "parallel")),
        name="neighbourhood_attn",
    )(q, k, k, k, k, v, v, v, v, k, v, bias)


def _ctx_attn_kernel(sink_ref, q_ref, k_ref, v_ref, o_ref, *, use_sink):
    s = _dot_nt(q_ref[...], k_ref[...])
    m = jnp.max(s, axis=-1, keepdims=True)
    if use_sink:
        sk = sink_ref[pl.program_id(1)]
        m = jnp.maximum(m, sk)
    p = jnp.exp(s - m)
    l = jnp.sum(p, axis=-1, keepdims=True)
    if use_sink:
        l = l + jnp.exp(sk - m)
    o_ref[...] = (_dot(p.astype(BF), v_ref[...]) / l).astype(BF)


def _ctx_attn_call(sink, q, k, v, n_batch, n_lat, n_ctx, G, d, dv, use_sink, name):
    H = q.shape[1] // d
    cb = (n_batch * n_lat) // n_ctx
    return pl.pallas_call(
        functools.partial(_ctx_attn_kernel, use_sink=use_sink),
        grid=(n_batch, H),
        in_specs=[
            pl.BlockSpec(memory_space=pltpu.SMEM),
            pl.BlockSpec((n_ctx, d), lambda b, h: (cb + b, h)),
            pl.BlockSpec((n_ctx, d), lambda b, h: (cb + b, h // G)),
            pl.BlockSpec((n_ctx, dv), lambda b, h: (cb + b, h // G)),
        ],
        out_specs=pl.BlockSpec((n_ctx, dv), lambda b, h: (b, h)),
        out_shape=jax.ShapeDtypeStruct((n_batch * n_ctx, H * dv), BF),
        compiler_params=_cparams(("parallel", "parallel")),
        name=name,
    )(sink, q, k, v)


def _merge_kernel(x_ref, h_ref, oa_ref, om_ref, ow_ref, on_ref, wg_ref, bg_ref, wb_ref, wo_ref,
                  g1_ref, nf_ref, sh2_ref, sc2_ref, wr_ref, br_ref,
                  xo_ref, h2_ref, ei_ref, wt_ref, y_scr):
    j = pl.program_id(1)

    @pl.when(j == 0)
    def _():
        y_scr[...] = jnp.zeros_like(y_scr)

    h = h_ref[...]
    acc = None
    for i, o_ref in enumerate((oa_ref, om_ref, ow_ref, on_ref)):
        gate = _sigmoid(_dot(h, wg_ref[i]) + bg_ref[i])
        y = gate * _dot(o_ref[...], wb_ref[i])
        acc = y if acc is None else acc + y
    y_scr[...] += _dot(acc.astype(BF), wo_ref[...])

    @pl.when(j == pl.num_programs(1) - 1)
    def _():
        x = x_ref[...] + g1_ref[...] * y_scr[...]
        xo_ref[...] = x
        ms = jnp.mean(x * x, axis=-1, keepdims=True)
        h2 = x * lax.rsqrt(ms + EPS) * nf_ref[...]
        h2 = h2 * (1.0 + sc2_ref[...]) + sh2_ref[...]
        h2_ref[...] = h2
        hi = h2.astype(BF)
        lo = (h2 - hi.astype(F32)).astype(BF)
        logits = _dot(hi, wr_ref[0]) + (_dot(lo, wr_ref[0]) + _dot(hi, wr_ref[1]))
        scores = _sigmoid(logits)
        biased = scores + br_ref[...]
        lane = lax.broadcasted_iota(jnp.int32, logits.shape, 1)
        lane_f = lane.astype(F32)
        ninf = -jnp.inf

        def top2(vals):
            t1 = jnp.max(vals, axis=-1, keepdims=True)
            i1 = jnp.min(jnp.where(vals == t1, lane_f, float(LANES)), axis=-1, keepdims=True)
            vals2 = jnp.where(lane_f == i1, ninf, vals)
            t2 = jnp.max(vals2, axis=-1, keepdims=True)
            i2 = jnp.min(jnp.where(vals2 == t2, lane_f, float(LANES)), axis=-1, keepdims=True)
            return t1, i1, t2, i2

        per = N_EXPERTS // N_GROUPS
        best, gi = None, None
        for g in range(N_GROUPS):
            ing = (lane >= g * per) & (lane < (g + 1) * per)
            t1, _, t2, _ = top2(jnp.where(ing, biased, ninf))
            gs = t1 + t2
            if best is None:
                best, gi = gs, jnp.zeros_like(gs)
            else:
                better = gs > best
                best = jnp.where(better, gs, best)
                gi = jnp.where(better, float(g), gi)
        lane_grp = (lane >> 2).astype(F32)
        vals = jnp.where(lane < N_EXPERTS, jnp.where(lane_grp == gi, biased, NEG), ninf)
        _, i1, _, i2 = top2(vals)
        w1 = jnp.sum(jnp.where(lane_f == i1, scores, 0.0), axis=-1, keepdims=True)
        w2 = jnp.sum(jnp.where(lane_f == i2, scores, 0.0), axis=-1, keepdims=True)
        den = w1 + w2
        ei_ref[...] = jnp.where(lane == 0, i1, jnp.where(lane == 1, i2, 0.0)).astype(jnp.int32)
        wt_ref[...] = jnp.where(lane == 0, w1 / den, jnp.where(lane == 1, w2 / den, 0.0))


def _merge_call(X, h, outs, wg, bg, wb, wo, modl, norm_f, wr_p, br_p, rows, n_batch, n_lat, tm, tn):
    D = X.shape[1]
    bw = wb.shape[1]
    tpb = n_lat // tm

    def rowmap(i, j):
        return (i, 0)

    return pl.pallas_call(
        _merge_kernel,
        grid=(rows // tm, D // tn),
        in_specs=[
            pl.BlockSpec((tm, D), rowmap),
            pl.BlockSpec((tm, D), rowmap),
        ] + [pl.BlockSpec((tm, bw), rowmap)] * 4 + [
            pl.BlockSpec((4, D, tn), lambda i, j: (0, 0, j)),
            pl.BlockSpec((4, 1, tn), lambda i, j: (0, 0, j)),
            pl.BlockSpec((4, bw, tn), lambda i, j: (0, 0, j)),
            pl.BlockSpec((tn, D), lambda i, j: (j, 0)),
            _mod_spec(2, tpb, n_batch, D),
            pl.BlockSpec((1, D), lambda i, j: (0, 0)),
            _mod_spec(3, tpb, n_batch, D),
            _mod_spec(4, tpb, n_batch, D),
            pl.BlockSpec((2, D, LANES), lambda i, j: (0, 0, 0)),
            pl.BlockSpec((1, LANES), lambda i, j: (0, 0)),
        ],
        out_specs=[
            pl.BlockSpec((tm, D), rowmap),
            pl.BlockSpec((tm, D), rowmap),
            pl.BlockSpec((tm, LANES), rowmap),
            pl.BlockSpec((tm, LANES), rowmap),
        ],
        out_shape=[
            jax.ShapeDtypeStruct((rows, D), F32),
            jax.ShapeDtypeStruct((rows, D), F32),
            jax.ShapeDtypeStruct((rows, LANES), jnp.int32),
            jax.ShapeDtypeStruct((rows, LANES), F32),
        ],
        scratch_shapes=[pltpu.VMEM((tm, D), F32)],
        compiler_params=_cparams(("parallel", "arbitrary")),
        name="merge_residual_router",
    )(X, h, *outs, wg, bg, wb, wo, modl, norm_f.reshape(1, D), modl, modl, wr_p, br_p)


def _row_dma_loop(n_rows, make_copy):
    def issue(r, carry):
        for k in range(2):
            make_copy(r, k).start()
        return carry

    def drain(r, carry):
        for k in range(2):
            make_copy(r, k).wait()
        return carry

    lax.fori_loop(0, n_rows, issue, 0)
    lax.fori_loop(0, n_rows, drain, 0)


def _scatter_kernel(pos_ref, h_ref, xs_in_ref, xs_ref, sem):
    del xs_in_ref

    def make_copy(r, k):
        return pltpu.make_async_copy(h_ref.at[pl.ds(r, 1), :],
                                     xs_ref.at[pl.ds(pos_ref[2 * r + k], 1), :], sem)

    _row_dma_loop(h_ref.shape[0], make_copy)


def _scatter_call(h2, pos, n_slots, tm):
    rows, D = h2.shape
    return pl.pallas_call(
        _scatter_kernel,
        grid=(rows // tm,),
        in_specs=[
            pl.BlockSpec((2 * tm,), lambda i: (i,), memory_space=pltpu.SMEM),
            pl.BlockSpec((tm, D), lambda i: (i, 0)),
            pl.BlockSpec(memory_space=pl.ANY),
        ],
        out_specs=pl.BlockSpec(memory_space=pl.ANY),
        out_shape=jax.ShapeDtypeStruct((n_slots, D), F32),
        scratch_shapes=[pltpu.SemaphoreType.DMA(())],
        input_output_aliases={2: 0},
        compiler_params=_cparams(("arbitrary",)),
        name="moe_scatter_rows",
    )(pos, h2, jnp.zeros((n_slots, D), F32))


def _experts_kernel(te_ref, nu_ref, x_ref, w1_ref, w3_ref, w2_ref, y_ref):
    del te_ref
    i = pl.program_id(0)

    @pl.when(i < nu_ref[0])
    def _():
        x = x_ref[...].astype(BF)
        a = _dot(x, w1_ref[...])
        b = _dot(x, w3_ref[...])
        hid = (a * _sigmoid(a)) * b
        y_ref[...] = _dot(hid.astype(BF), w2_ref[...])

    @pl.when(i >= nu_ref[0])
    def _():
        y_ref[...] = jnp.zeros_like(y_ref)


def _experts_call(xs, tile_expert, n_used, w1, w3, w2, tg):
    S, D = xs.shape
    de = w1.shape[2]
    grid_spec = pltpu.PrefetchScalarGridSpec(
        num_scalar_prefetch=2,
        grid=(S // tg,),
        in_specs=[
            pl.BlockSpec((tg, D), lambda i, te, nu: (i, 0)),
            pl.BlockSpec((None, D, de), lambda i, te, nu: (te[i], 0, 0)),
            pl.BlockSpec((None, D, de), lambda i, te, nu: (te[i], 0, 0)),
            pl.BlockSpec((None, de, D), lambda i, te, nu: (te[i], 0, 0)),
        ],
        out_specs=pl.BlockSpec((tg, D), lambda i, te, nu: (i, 0)),
    )
    return pl.pallas_call(
        _experts_kernel,
        grid_spec=grid_spec,
        out_shape=jax.ShapeDtypeStruct((S, D), F32),
        compiler_params=_cparams(("arbitrary",)),
        name="moe_experts",
    )(tile_expert, n_used, xs, w1, w3, w2)


def _combine_kernel(pos_ref, x_ref, wt_ref, g2_ref, ys_ref, o_ref, buf, sem):
    def make_copy(r, k):
        return pltpu.make_async_copy(ys_ref.at[pl.ds(pos_ref[2 * r + k], 1), :],
                                     buf.at[k, pl.ds(r, 1), :], sem)

    _row_dma_loop(x_ref.shape[0], make_copy)
    wt = wt_ref[...]
    moe = wt[:, 0:1] * buf[0] + wt[:, 1:2] * buf[1]
    o_ref[...] = x_ref[...] + g2_ref[...] * moe


def _combine_call(X, ys, pos, wts, modl, n_batch, n_lat, tm):
    rows, D = X.shape
    tpb = n_lat // tm
    return pl.pallas_call(
        _combine_kernel,
        grid=(rows // tm,),
        in_specs=[
            pl.BlockSpec((2 * tm,), lambda i: (i,), memory_space=pltpu.SMEM),
            pl.BlockSpec((tm, D), lambda i: (i, 0)),
            pl.BlockSpec((tm, LANES), lambda i: (i, 0)),
            _mod_spec(5, tpb, n_batch, D),
            pl.BlockSpec(memory_space=pl.ANY),
        ],
        out_specs=pl.BlockSpec((tm, D), lambda i: (i, 0)),
        out_shape=jax.ShapeDtypeStruct((rows, D), F32),
        scratch_shapes=[pltpu.VMEM((2, tm, D), F32), pltpu.SemaphoreType.DMA(())],
        compiler_params=_cparams(("arbitrary",)),
        name="moe_combine_residual",
    )(pos, X, wts, modl, ys)


def _route_positions(eidx, tg):
    rows = eidx.shape[0]
    e_flat = eidx.reshape(-1)
    onehot = (e_flat[:, None] == jnp.arange(N_EXPERTS, dtype=jnp.int32)[None, :]).astype(jnp.int32)
    csum = jnp.cumsum(onehot, axis=0)
    counts = csum[-1]
    rank = jnp.sum((csum - onehot) * onehot, axis=1)
    padded = ((counts + tg - 1) // tg) * tg
    ends = jnp.cumsum(padded)
    offsets = ends - padded
    pos = jnp.sum(onehot * offsets[None, :], axis=1) + rank
    n_tiles = (2 * rows + N_EXPERTS * tg) // tg
    tile_start = jnp.arange(n_tiles, dtype=jnp.int32) * tg
    tile_expert = jnp.sum((tile_start[:, None] >= ends[None, :]).astype(jnp.int32), axis=1)
    n_used = (ends[-1] // tg).astype(jnp.int32)
    last_e = jnp.sum((((n_used - 1) * tg) >= ends).astype(jnp.int32))
    tile_expert = jnp.where(jnp.arange(n_tiles) < n_used, tile_expert, last_e).astype(jnp.int32)
    return pos.astype(jnp.int32), tile_expert, n_used.reshape(1)


def _rope_tables(n_lat, tm):
    t = jnp.arange(n_lat)
    rows, cols = t // GRID_W, t % GRID_W

    def tab(d2, pos):
        freqs = ROPE_THETA ** (-jnp.arange(d2, dtype=F32) / d2)
        ang = pos.astype(F32)[:, None] * freqs[None, :]
        return jnp.cos(ang), jnp.sin(ang)

    cr, sr = tab(32, rows)
    cc, sc = tab(32, cols)
    cos_a = jnp.concatenate([cr, cr, cc, cc], axis=-1)
    sin_a = jnp.concatenate([-sr, sr, -sc, sc], axis=-1)
    cr, sr = tab(16, rows)
    cc, sc = tab(16, cols)
    one, zero = jnp.ones((n_lat, 64), F32), jnp.zeros((n_lat, 64), F32)
    cos_m = jnp.concatenate([cr, cr, cc, cc, one], axis=-1)
    sin_m = jnp.concatenate([-sr, sr, -sc, sc, zero], axis=-1)
    ident_c, ident_s = jnp.ones((tm, LANES), F32), jnp.zeros((tm, LANES), F32)
    return tuple(jnp.concatenate([a, b], axis=0) for a, b in
                 ((cos_a, ident_c), (sin_a, ident_s), (cos_m, ident_c), (sin_m, ident_s)))


def _pad_mla_heads(w, n_heads):
    lead = w.shape[:-1]
    w = w.reshape(lead + (n_heads, MLA_QK))
    w = jnp.pad(w, [(0, 0)] * len(lead) + [(0, 0), (0, MLA_PAD - MLA_QK)])
    return w.reshape(lead + (n_heads * MLA_PAD,))


def kernel(x, c, ctx, c_ctx, w_mod, b_mod, norm_mix, norm_ffn, w_in, mla_qa_norm, mla_w_uq, mla_kva_norm,
           mla_w_ukv, qn_att, kn_att, qn_mla, kn_mla, qn_win, kn_win, qn_na, kn_na, win_sink, na_rpb,
           w_branch, w_gate, b_gate, w_out, w_router, b_router, moe_w1, moe_w3, moe_w2):
    B, N, D = x.shape
    n_ctx = ctx.shape[1]
    L = w_mod.shape[0]
    R = B * N + B * n_ctx
    tm = 512
    tm_moe = 256
    tg = 256
    tn_merge = min(256, D // 2)
    tq = min(512, N)

    X = jnp.concatenate([x.reshape(B * N, D), ctx.reshape(B * n_ctx, D)], axis=0)
    cvec = jnp.zeros((8, D), F32).at[:B].set(c).at[B].set(c_ctx)
    mod = _mod_call(cvec, w_mod, b_mod, min(1024, D)).reshape(L, 8, 6, 1, D)
    tabs = _rope_tables(N, tm)

    wr_hi = w_router.astype(BF)
    wr_lo = (w_router - wr_hi.astype(F32)).astype(BF)
    wr_p = jnp.pad(jnp.stack([wr_hi, wr_lo]), ((0, 0), (0, 0), (0, LANES - N_EXPERTS)))
    br_p = jnp.pad(b_router.astype(F32), (0, LANES - N_EXPERTS)).reshape(1, LANES)

    for l in range(L):
        last = l == L - 1
        rows = B * N if last else R
        modl = mod[l]
        w_in_l = w_in[l]
        w_in_p = jnp.concatenate(
            [w_in_l[:, :KR_END], jnp.zeros((D, 64), F32), w_in_l[:, KR_END:]], axis=1).astype(BF)
        wuq_p = _pad_mla_heads(mla_w_uq[l], 4).astype(BF)
        wukv = mla_w_ukv[l].astype(BF)
        gains = [g.reshape(1, -1) for g in (
            qn_att[l], kn_att[l], qn_win[l], kn_win[l], qn_na[l], kn_na[l],
            _pad_mla_heads(qn_mla[l], 1), _pad_mla_heads(kn_mla[l], 1), mla_qa_norm[l], mla_kva_norm[l])]

        h, proj = _proj_call(X, norm_mix[l], modl, w_in_p, B, N, tm, P_TOTAL // 2)
        (qa, ka, va, qm, km, vm, qw, kw, vw, qn, kn, vn) = _heads_call(proj, tabs, gains, wuq_p, wukv, B, N, tm)

        sink = win_sink[l].astype(F32)
        o_att = _flash_call(qa, ka, va, B, N, n_ctx, 2, HEAD_DIM, HEAD_DIM, tq, tq, "dense_gqa")
        o_mla = _flash_call(qm, km, vm, B, N, n_ctx, 1, MLA_PAD, HEAD_DIM, tq, tq, "latent_attn")
        o_win = _window_call(sink, qw, kw, vw, B, N, n_ctx, 2, tq)
        bias = _na_bias_tables(na_rpb[l], N // GRID_W)
        o_na = _na_call(qn, kn, vn, bias, B, N, n_ctx)
        outs = [o_att, o_mla, o_win, o_na]
        if not last:
            outs_c = [
                _ctx_attn_call(sink, qa, ka, va, B, N, n_ctx, 2, HEAD_DIM, HEAD_DIM, False, "ctx_dense_gqa"),
                _ctx_attn_call(sink, qm, km, vm, B, N, n_ctx, 1, MLA_PAD, HEAD_DIM, False, "ctx_latent_attn"),
                _ctx_attn_call(sink, qw, kw, vw, B, N, n_ctx, 2, HEAD_DIM, HEAD_DIM, True, "ctx_window_attn"),
                _ctx_attn_call(sink, qn, kn, vn, B, N, n_ctx, 1, HEAD_DIM, HEAD_DIM, False, "ctx_neighbourhood"),
            ]
            outs = [jnp.concatenate([a, b], axis=0) for a, b in zip(outs, outs_c)]

        X, h2, eidx, wts = _merge_call(
            X, h, outs, w_gate[l].astype(BF), b_gate[l].reshape(4, 1, D), w_branch[l].astype(BF),
            w_out[l].astype(BF), modl, norm_ffn[l], wr_p, br_p, rows, B, N, tm, tn_merge)

        pos, tile_expert, n_used = _route_positions(eidx[:, :2], tg)
        n_slots = 2 * rows + N_EXPERTS * tg
        xs = _scatter_call(h2, pos, n_slots, tm_moe)
        ys = _experts_call(xs, tile_expert, n_used, moe_w1[l].astype(BF), moe_w3[l].astype(BF),
                           moe_w2[l].astype(BF), tg)
        X = _combine_call(X, ys, pos, wts, modl, B, N, tm_moe)

    return X.reshape(B, N, D)
```

```python
import functools

import numpy as np
import jax
import jax.numpy as jnp
from jax import lax
from jax.experimental import pallas as pl
from jax.experimental.pallas import tpu as pltpu

BF = jnp.bfloat16
F32 = jnp.float32

GRID_W = 64
HEAD_DIM = 128
ROPE_THETA = 10000.0
EPS = 1e-6
NEG = -1e30
LOG2E = 1.4426950408889634
WINDOW = 128
NA_WIN_H = 8
NA_WIN_W = 16
N_EXPERTS = 16
N_GROUPS = 4
MLA_QK = 192
MLA_PAD = 256
LANES = 128
VMEM_LIMIT = 56 * 1024 * 1024

P_AQ, P_AK, P_AV = 0, 512, 768
P_MCQ, P_MCKV, P_MKR = 1024, 1408, 1664
P_WQ, P_WK, P_WV = 1792, 2304, 2560
P_NQ, P_NK, P_NV = 2816, 3328, 3840
P_TOTAL = 4352
KR_END = 1728


def _cparams(sem):
    return pltpu.CompilerParams(dimension_semantics=sem, vmem_limit_bytes=VMEM_LIMIT)


def _sigmoid(z):
    return 1.0 / (1.0 + jnp.exp(-z))


def _dot(a, b):
    return jnp.dot(a, b, preferred_element_type=F32)


def _dot_nt(a, b):
    return lax.dot_general(a, b, (((1,), (1,)), ((), ())), preferred_element_type=F32)


def _mod_kernel(c_ref, w_ref, b_ref, o_ref):
    c = c_ref[...]
    s = c * _sigmoid(c)
    o_ref[...] = _dot(s.astype(BF), w_ref[...].astype(BF)) + b_ref[...]


def _mod_call(cvec, w_mod, b_mod, tn):
    L, D, D6 = w_mod.shape
    return pl.pallas_call(
        _mod_kernel,
        grid=(L, D6 // tn),
        in_specs=[
            pl.BlockSpec((8, D), lambda l, j: (0, 0)),
            pl.BlockSpec((None, D, tn), lambda l, j: (l, 0, j)),
            pl.BlockSpec((None, 1, tn), lambda l, j: (l, 0, j)),
        ],
        out_specs=pl.BlockSpec((None, 8, tn), lambda l, j: (l, 0, j)),
        out_shape=jax.ShapeDtypeStruct((L, 8, D6), F32),
        compiler_params=_cparams(("parallel", "parallel")),
        name="adaln_mod",
    )(cvec, w_mod, b_mod.reshape(L, 1, D6))


def _mod_spec(comp, tiles_per_batch, n_batch, D):
    return pl.BlockSpec(
        (None, None, 1, D),
        lambda i, *_: (jnp.minimum(i // tiles_per_batch, n_batch), comp, 0, 0))


def _proj_kernel(x_ref, g_ref, sh_ref, sc_ref, w_ref, h_ref, o_ref, h_scr):
    @pl.when(pl.program_id(1) == 0)
    def _():
        x = x_ref[...]
        ms = jnp.mean(x * x, axis=-1, keepdims=True)
        xn = x * lax.rsqrt(ms + EPS) * g_ref[...]
        h = (xn * (1.0 + sc_ref[...]) + sh_ref[...]).astype(BF)
        h_scr[...] = h
        h_ref[...] = h

    o_ref[...] = _dot(h_scr[...], w_ref[...])


def _proj_call(X, norm_g, modl, w_in_p, n_batch, n_lat, tm, tn):
    R, D = X.shape
    P = w_in_p.shape[1]
    tpb = n_lat // tm
    return pl.pallas_call(
        _proj_kernel,
        grid=(R // tm, P // tn),
        in_specs=[
            pl.BlockSpec((tm, D), lambda i, j: (i, 0)),
            pl.BlockSpec((1, D), lambda i, j: (0, 0)),
            _mod_spec(0, tpb, n_batch, D),
            _mod_spec(1, tpb, n_batch, D),
            pl.BlockSpec((D, tn), lambda i, j: (0, j)),
        ],
        out_specs=[
            pl.BlockSpec((tm, D), lambda i, j: (i, 0)),
            pl.BlockSpec((tm, tn), lambda i, j: (i, j)),
        ],
        out_shape=[jax.ShapeDtypeStruct((R, D), BF), jax.ShapeDtypeStruct((R, P), F32)],
        scratch_shapes=[pltpu.VMEM((tm, D), BF)],
        compiler_params=_cparams(("parallel", "arbitrary")),
        name="norm_mod_proj",
    )(X, norm_g.reshape(1, D), modl, modl, w_in_p)


def _rms(x, g, n):
    ss = jnp.sum(x * x, axis=-1, keepdims=True)
    return x * lax.rsqrt(ss / n + EPS) * g


def _rope(x, cos, sin_signed, half):
    lane = lax.broadcasted_iota(jnp.int32, x.shape, 1)
    first = (lane & (2 * half - 1)) < half
    xr = jnp.where(first, pltpu.roll(x, LANES - half, 1), pltpu.roll(x, half, 1))
    return x * cos + xr * sin_signed


def _heads_kernel(p_ref, ca_ref, sa_ref, cm_ref, sm_ref,
                  qa_g, ka_g, qw_g, kw_g, qn_g, kn_g, qm_g, km_g, cqn_g, ckvn_g, wuq_ref, wukv_ref,
                  qa_ref, ka_ref, va_ref, qm_ref, km_ref, vm_ref,
                  qw_ref, kw_ref, vw_ref, qn_ref, kn_ref, vn_ref):
    hd = HEAD_DIM
    sc128 = HEAD_DIM ** -0.5
    sc192 = MLA_QK ** -0.5
    ca, sa, cm, sm = ca_ref[...], sa_ref[...], cm_ref[...], sm_ref[...]

    def sl(off, h, w=hd):
        return p_ref[:, off + h * w: off + (h + 1) * w]

    for (q_off, k_off, qg, kg, q_out, k_out, q_scale) in (
            (P_AQ, P_AK, qa_g, ka_g, qa_ref, ka_ref, sc128 * LOG2E),
            (P_WQ, P_WK, qw_g, kw_g, qw_ref, kw_ref, sc128)):
        for h in range(4):
            q = _rope(_rms(sl(q_off, h), qg[...], hd), ca, sa, 32)
            q_out[:, h * hd:(h + 1) * hd] = (q * q_scale).astype(BF)
        for h in range(2):
            k = _rope(_rms(sl(k_off, h), kg[...], hd), ca, sa, 32)
            k_out[:, h * hd:(h + 1) * hd] = k.astype(BF)
    vw_ref[...] = p_ref[:, P_WV:P_WV + 2 * hd].astype(BF)
    ones = jnp.ones((p_ref.shape[0], hd), BF)
    for h in range(2):
        va_ref[:, 2 * h * hd:(2 * h + 1) * hd] = sl(P_AV, h).astype(BF)
        va_ref[:, (2 * h + 1) * hd:(2 * h + 2) * hd] = ones

    for h in range(4):
        qn_ref[:, h * hd:(h + 1) * hd] = (_rms(sl(P_NQ, h), qn_g[...], hd) * sc128).astype(BF)
        kn_ref[:, h * hd:(h + 1) * hd] = _rms(sl(P_NK, h), kn_g[...], hd).astype(BF)
    vn_ref[...] = p_ref[:, P_NV:P_NV + 4 * hd].astype(BF)

    cq = _rms(p_ref[:, P_MCQ:P_MCQ + 384], cqn_g[...], 384)
    qf = _dot(cq.astype(BF), wuq_ref[...])
    ckv = _rms(p_ref[:, P_MCKV:P_MCKV + 256], ckvn_g[...], 256)
    kvf = _dot(ckv.astype(BF), wukv_ref[...])
    kr = p_ref[:, P_MKR:P_MKR + hd]
    kr_ss = jnp.sum(kr * kr, axis=-1, keepdims=True)
    qg0, qg1 = qm_g[:, :hd], qm_g[:, hd:]
    kg0, kg1 = km_g[:, :hd], km_g[:, hd:]
    for h in range(4):
        q0 = qf[:, h * MLA_PAD: h * MLA_PAD + hd]
        q1 = qf[:, h * MLA_PAD + hd: (h + 1) * MLA_PAD]
        ss = jnp.sum(q0 * q0, axis=-1, keepdims=True) + jnp.sum(q1 * q1, axis=-1, keepdims=True)
        r = lax.rsqrt(ss / MLA_QK + EPS)
        qm_ref[:, h * MLA_PAD: h * MLA_PAD + hd] = (q0 * r * qg0 * (sc192 * LOG2E)).astype(BF)
        qm_ref[:, h * MLA_PAD + hd: (h + 1) * MLA_PAD] = (
            _rope(q1 * r * qg1, cm, sm, 16) * (sc192 * LOG2E)).astype(BF)
        k0 = kvf[:, h * 2 * hd: h * 2 * hd + hd]
        ss = jnp.sum(k0 * k0, axis=-1, keepdims=True) + kr_ss
        r = lax.rsqrt(ss / MLA_QK + EPS)
        km_ref[:, h * MLA_PAD: h * MLA_PAD + hd] = (k0 * r * kg0).astype(BF)
        km_ref[:, h * MLA_PAD + hd: (h + 1) * MLA_PAD] = _rope(kr * r * kg1, cm, sm, 16).astype(BF)
        vm_ref[:, 2 * h * hd:(2 * h + 1) * hd] = kvf[:, h * 2 * hd + hd: (h + 1) * 2 * hd].astype(BF)
        vm_ref[:, (2 * h + 1) * hd:(2 * h + 2) * hd] = ones


def _heads_call(proj, tabs, gains, wuq_p, wukv, n_batch, n_lat, tm):
    R, P = proj.shape
    tpb = n_lat // tm
    tab_spec = pl.BlockSpec((tm, LANES), lambda i: (jnp.where(i < n_batch * tpb, i % tpb, tpb), 0))

    def full(a):
        return pl.BlockSpec(a.shape, lambda i: (0,) * a.ndim)

    widths = (512, 256, 512, 4 * MLA_PAD, 4 * MLA_PAD, 1024, 512, 256, 256, 512, 512, 512)
    return pl.pallas_call(
        _heads_kernel,
        grid=(R // tm,),
        in_specs=[pl.BlockSpec((tm, P), lambda i: (i, 0))] + [tab_spec] * 4
        + [full(g) for g in gains] + [full(wuq_p), full(wukv)],
        out_specs=[pl.BlockSpec((tm, w), lambda i: (i, 0)) for w in widths],
        out_shape=[jax.ShapeDtypeStruct((R, w), BF) for w in widths],
        compiler_params=_cparams(("parallel",)),
        name="head_prep",
    )(proj, *tabs, *gains, wuq_p, wukv)


def _flash_kernel(q_ref, k_ref, v_ref, kc_ref, vc_ref, o_ref, q_scr, s_scr, m_scr, acc_scr, *, G, d, dv, w):
    tq = q_ref.shape[0]
    n_blk = k_ref.shape[0] // w
    for g in range(G):
        q_scr[g * tq:(g + 1) * tq, :] = q_ref[:, g * d:(g + 1) * d]
    m_scr[...] = jnp.full_like(m_scr, NEG)
    acc_scr[...] = jnp.zeros_like(acc_scr)

    def scores(k):
        return _dot_nt(q_scr[...], k)

    def absorb(s, v):
        m_prev = m_scr[...]
        m_next = jnp.maximum(m_prev, s.max(axis=1, keepdims=True))
        alpha = jnp.exp2(m_prev - m_next)
        p = jnp.concatenate(
            [jnp.exp2(s[:, c * LANES:(c + 1) * LANES] - m_next) for c in range(s.shape[1] // LANES)],
            axis=1)
        pv = _dot(p.astype(BF), v)
        acc_scr[...] = jnp.concatenate([alpha, alpha], axis=1) * acc_scr[...] + pv
        m_scr[...] = m_next

    absorb(scores(kc_ref[...]), vc_ref[...])
    s_scr[0] = scores(k_ref[0:w, :])

    def pair(jj, carry):
        b0 = pl.multiple_of(2 * jj * w, w)
        b1 = pl.multiple_of(b0 + w, w)
        b2 = pl.multiple_of(jnp.minimum(b0 + 2 * w, (n_blk - 1) * w), w)
        s_scr[1] = scores(k_ref[pl.ds(b1, w), :])
        absorb(s_scr[0], v_ref[pl.ds(b0, w), :])
        s_scr[0] = scores(k_ref[pl.ds(b2, w), :])
        absorb(s_scr[1], v_ref[pl.ds(b1, w), :])
        return carry

    lax.fori_loop(0, n_blk // 2, pair, 0)

    acc = acc_scr[...]
    o = acc[:, :dv] / acc[:, dv:]
    for g in range(G):
        o_ref[:, g * dv:(g + 1) * dv] = o[g * tq:(g + 1) * tq].astype(BF)


def _flash_call(q, k, v, n_batch, n_lat, n_ctx, G, d, tq, w, name):
    dv = LANES
    hkv = k.shape[1] // d
    nq = n_lat // tq
    cb = (n_batch * n_lat) // n_ctx
    assert (n_lat // w) % 2 == 0
    return pl.pallas_call(
        functools.partial(_flash_kernel, G=G, d=d, dv=dv, w=w),
        grid=(n_batch, hkv, nq),
        in_specs=[
            pl.BlockSpec((tq, G * d), lambda b, h, i: (b * nq + i, h)),
            pl.BlockSpec((n_lat, d), lambda b, h, i: (b, h)),
            pl.BlockSpec((n_lat, 2 * dv), lambda b, h, i: (b, h)),
            pl.BlockSpec((n_ctx, d), lambda b, h, i: (cb + b, h)),
            pl.BlockSpec((n_ctx, 2 * dv), lambda b, h, i: (cb + b, h)),
        ],
        out_specs=pl.BlockSpec((tq, G * dv), lambda b, h, i: (b * nq + i, h)),
        out_shape=jax.ShapeDtypeStruct((n_batch * n_lat, hkv * G * dv), BF),
        scratch_shapes=[pltpu.VMEM((G * tq, d), BF), pltpu.VMEM((2, G * tq, w), F32),
                        pltpu.VMEM((G * tq, LANES), F32), pltpu.VMEM((G * tq, 2 * dv), F32)],
        compiler_params=_cparams(("parallel", "parallel", "parallel")),
        name=name,
    )(q, k, v, k, v)


def _window_kernel(sink_ref, q_ref, kp_ref, km_ref, kn_ref, vp_ref, vm_ref, vn_ref, kc_ref, vc_ref,
                   o_ref, *, G):
    hd = HEAD_DIM
    h = pl.program_id(1)
    i = pl.program_id(2)
    ni = pl.num_programs(2)
    tq = q_ref.shape[0]
    kcat = jnp.concatenate([kp_ref[...], km_ref[...], kn_ref[...]], axis=0)
    vcat = jnp.concatenate([vp_ref[...], vm_ref[...], vn_ref[...]], axis=0)
    r = lax.broadcasted_iota(jnp.int32, (tq, tq + 2 * WINDOW), 0)
    c = lax.broadcasted_iota(jnp.int32, (tq, tq + 2 * WINDOW), 1)
    rel = c - WINDOW - r
    lo = jnp.where(i > 0, 0, WINDOW)
    hi = jnp.where(i < ni - 1, tq + 2 * WINDOW, tq + WINDOW)
    valid = (jnp.abs(rel) <= WINDOW) & (c >= lo) & (c < hi)
    for g in range(G):
        q = q_ref[:, g * hd:(g + 1) * hd]
        s = jnp.where(valid, _dot_nt(q, kcat), NEG)
        sc = _dot_nt(q, kc_ref[...])
        sk = sink_ref[h * G + g]
        m = jnp.maximum(jnp.max(s, axis=-1, keepdims=True), jnp.max(sc, axis=-1, keepdims=True))
        m = jnp.maximum(m, sk)
        p = jnp.exp(s - m)
        pc = jnp.exp(sc - m)
        l = jnp.sum(p, axis=-1, keepdims=True) + jnp.sum(pc, axis=-1, keepdims=True) + jnp.exp(sk - m)
        o = _dot(p.astype(BF), vcat) + _dot(pc.astype(BF), vc_ref[...])
        o_ref[:, g * hd:(g + 1) * hd] = (o / l).astype(BF)


def _window_call(sink, q, k, v, n_batch, n_lat, n_ctx, G, tq):
    hd = HEAD_DIM
    hkv = k.shape[1] // hd
    nq = n_lat // tq
    bpt = tq // WINDOW
    nblk = k.shape[0] // WINDOW
    cb = (n_batch * n_lat) // n_ctx

    def prev(b, h, i):
        return (jnp.maximum((b * nq + i) * bpt - 1, 0), h)

    def nxt(b, h, i):
        return (jnp.minimum((b * nq + i + 1) * bpt, nblk - 1), h)

    def main(b, h, i):
        return (b * nq + i, h)

    def ctxb(b, h, i):
        return (cb + b, h)

    return pl.pallas_call(
        functools.partial(_window_kernel, G=G),
        grid=(n_batch, hkv, nq),
        in_specs=[
            pl.BlockSpec(memory_space=pltpu.SMEM),
            pl.BlockSpec((tq, G * hd), main),
            pl.BlockSpec((WINDOW, hd), prev), pl.BlockSpec((tq, hd), main), pl.BlockSpec((WINDOW, hd), nxt),
            pl.BlockSpec((WINDOW, hd), prev), pl.BlockSpec((tq, hd), main), pl.BlockSpec((WINDOW, hd), nxt),
            pl.BlockSpec((n_ctx, hd), ctxb), pl.BlockSpec((n_ctx, hd), ctxb),
        ],
        out_specs=pl.BlockSpec((tq, G * hd), main),
        out_shape=jax.ShapeDtypeStruct((n_batch * n_lat, hkv * G * hd), BF),
        compiler_params=_cparams(("parallel", "parallel", "parallel")),
        name="window_attn",
    )(sink, q, k, k, k, v, v, v, k, v)


NA_TQ = 8 * GRID_W
NA_KB = 4 * GRID_W
NA_NKB = 4


def _na_kernel(q_ref, k0, k1, k2, k3, v0, v1, v2, v3, kc_ref, vc_ref, bias_ref, o_ref):
    q = q_ref[...]
    kcat = jnp.concatenate([k0[...], k1[...], k2[...], k3[...]], axis=0)
    vcat = jnp.concatenate([v0[...], v1[...], v2[...], v3[...]], axis=0)
    s = _dot_nt(q, kcat) + bias_ref[...]
    sc = _dot_nt(q, kc_ref[...])
    m = jnp.maximum(jnp.max(s, axis=-1, keepdims=True), jnp.max(sc, axis=-1, keepdims=True))
    p = jnp.exp(s - m)
    pc = jnp.exp(sc - m)
    l = jnp.sum(p, axis=-1, keepdims=True) + jnp.sum(pc, axis=-1, keepdims=True)
    o = _dot(p.astype(BF), vcat) + _dot(pc.astype(BF), vc_ref[...])
    o_ref[...] = (o / l).astype(BF)


def _na_bias_tables(rpb, n_rows):
    H = rpb.shape[0]
    J = n_rows // 8
    nb = n_rows // 4
    kk = np.arange(NA_NKB * NA_KB)
    blk, within = kk // NA_KB, kk % NA_KB
    krl, kc = within // GRID_W, within % GRID_W
    qr = np.arange(8)
    cq = np.arange(GRID_W)
    cs = np.clip(cq - NA_WIN_W // 2, 0, GRID_W - NA_WIN_W)
    dx = np.arange(GRID_W)[None, :] - cq[:, None] + NA_WIN_W - 1
    col_ok = (np.arange(GRID_W)[None, :] >= cs[:, None]) & (np.arange(GRID_W)[None, :] < cs[:, None] + NA_WIN_W)
    dx1h = np.zeros((2 * NA_WIN_W - 1, GRID_W * GRID_W), np.float32)
    dxc = np.clip(dx, 0, 2 * NA_WIN_W - 2).reshape(-1)
    dx1h[dxc, np.arange(GRID_W * GRID_W)] = 1.0
    out = []
    for jv in (0, min(1, J - 1), J - 1):
        r = 8 * jv + qr
        kb_un = 2 * jv - 1 + np.arange(NA_NKB)
        kb = np.clip(kb_un, 0, nb - 1)
        dup = kb != kb_un
        krow = (4 * kb[:, None] + np.arange(4)[None, :]).reshape(-1)
        krow_dup = np.repeat(dup, 4)
        rs = np.clip(r - NA_WIN_H // 2, 0, n_rows - NA_WIN_H)
        row_ok = (krow[None, :] >= rs[:, None]) & (krow[None, :] < rs[:, None] + NA_WIN_H) & ~krow_dup[None, :]
        dy = np.clip(krow[None, :] - r[:, None] + NA_WIN_H - 1, 0, 2 * NA_WIN_H - 2)
        dy1h = np.zeros((8 * 16, 2 * NA_WIN_H - 1), np.float32)
        dy1h[np.arange(8 * 16), dy.reshape(-1)] = 1.0
        t = jnp.einsum("ay,hyx,xb->hab", jnp.asarray(dy1h), rpb.astype(F32), jnp.asarray(dx1h),
                       precision=lax.Precision.HIGHEST)
        t = t.reshape(H, 8, 16, GRID_W, GRID_W)
        ok = row_ok[:, :, None, None] & col_ok[None, None, :, :]
        t = jnp.where(jnp.asarray(ok)[None], t, NEG)
        t = t.transpose(0, 1, 3, 2, 4).reshape(H, NA_TQ, NA_NKB * NA_KB)
        out.append(t)
    del krl, kc, blk
    return jnp.stack(out, axis=1)


def _na_call(q, k, v, bias, n_batch, n_lat, n_ctx):
    hd = HEAD_DIM
    H = q.shape[1] // hd
    J = n_lat // NA_TQ
    nb = n_lat // NA_KB
    cb = (n_batch * n_lat) // n_ctx

    def kblk(t):
        return lambda b, h, j: (b * nb + jnp.clip(2 * j - 1 + t, 0, nb - 1), h)

    def qmap(b, h, j):
        return (b * J + j, h)

    def ctxb(b, h, j):
        return (cb + b, h)

    def bmap(b, h, j):
        return (h, jnp.where(j == 0, 0, jnp.where(j == J - 1, 2, 1)), 0, 0)

    return pl.pallas_call(
        _na_kernel,
        grid=(n_batch, H, J),
        in_specs=[pl.BlockSpec((NA_TQ, hd), qmap)]
        + [pl.BlockSpec((NA_KB, hd), kblk(t)) for t in range(NA_NKB)]
        + [pl.BlockSpec((NA_KB, hd), kblk(t)) for t in range(NA_NKB)]
        + [pl.BlockSpec((n_ctx, hd), ctxb), pl.BlockSpec((n_ctx, hd), ctxb),
           pl.BlockSpec((None, None, NA_TQ, NA_NKB * NA_KB), bmap)],
        out_specs=pl.BlockSpec((NA_TQ, hd), qmap),
        out_shape=jax.ShapeDtypeStruct((n_batch * n_lat, H * hd), BF),
        compiler_params=_cparams(("parallel", "parallel", "parallel")),
        name="neighbourhood_attn",
    )(q, k, k, k, k, v, v, v, v, k, v, bias)


def _ctx_attn_kernel(sink_ref, q_ref, k_ref, v_ref, o_ref, *, use_sink, base2):
    ex = jnp.exp2 if base2 else jnp.exp
    s = _dot_nt(q_ref[...], k_ref[...])
    m = jnp.max(s, axis=-1, keepdims=True)
    if use_sink:
        sk = sink_ref[pl.program_id(1)]
        m = jnp.maximum(m, sk)
    p = ex(s - m)
    l = jnp.sum(p, axis=-1, keepdims=True)
    if use_sink:
        l = l + ex(sk - m)
    o_ref[...] = (_dot(p.astype(BF), v_ref[...]) / l).astype(BF)


def _ctx_attn_call(sink, q, k, v, n_batch, n_lat, n_ctx, G, d, dv, v_stride, use_sink, base2, name):
    H = q.shape[1] // d
    cb = (n_batch * n_lat) // n_ctx
    return pl.pallas_call(
        functools.partial(_ctx_attn_kernel, use_sink=use_sink, base2=base2),
        grid=(n_batch, H),
        in_specs=[
            pl.BlockSpec(memory_space=pltpu.SMEM),
            pl.BlockSpec((n_ctx, d), lambda b, h: (cb + b, h)),
            pl.BlockSpec((n_ctx, d), lambda b, h: (cb + b, h // G)),
            pl.BlockSpec((n_ctx, dv), lambda b, h: (cb + b, (h // G) * v_stride)),
        ],
        out_specs=pl.BlockSpec((n_ctx, dv), lambda b, h: (b, h)),
        out_shape=jax.ShapeDtypeStruct((n_batch * n_ctx, H * dv), BF),
        compiler_params=_cparams(("parallel", "parallel")),
        name=name,
    )(sink, q, k, v)


def _merge_kernel(x_ref, h_ref, oa_ref, om_ref, ow_ref, on_ref, wg_ref, bg_ref, wb_ref, wo_ref,
                  g1_ref, nf_ref, sh2_ref, sc2_ref, wr_ref, br_ref,
                  xo_ref, h2_ref, ei_ref, wt_ref, y_scr):
    j = pl.program_id(1)

    @pl.when(j == 0)
    def _():
        y_scr[...] = jnp.zeros_like(y_scr)

    h = h_ref[...]
    acc = None
    for i, o_ref in enumerate((oa_ref, om_ref, ow_ref, on_ref)):
        gate = _sigmoid(_dot(h, wg_ref[i]) + bg_ref[i])
        y = gate * _dot(o_ref[...], wb_ref[i])
        acc = y if acc is None else acc + y
    y_scr[...] += _dot(acc.astype(BF), wo_ref[...])

    @pl.when(j == pl.num_programs(1) - 1)
    def _():
        x = x_ref[...] + g1_ref[...] * y_scr[...]
        xo_ref[...] = x
        ms = jnp.mean(x * x, axis=-1, keepdims=True)
        h2 = x * lax.rsqrt(ms + EPS) * nf_ref[...]
        h2 = h2 * (1.0 + sc2_ref[...]) + sh2_ref[...]
        h2_ref[...] = h2
        hi = h2.astype(BF)
        lo = (h2 - hi.astype(F32)).astype(BF)
        logits = _dot(hi, wr_ref[0]) + (_dot(lo, wr_ref[0]) + _dot(hi, wr_ref[1]))
        scores = _sigmoid(logits)
        biased = scores + br_ref[...]
        lane = lax.broadcasted_iota(jnp.int32, logits.shape, 1)
        lane_f = lane.astype(F32)
        ninf = -jnp.inf

        def top2(vals):
            t1 = jnp.max(vals, axis=-1, keepdims=True)
            i1 = jnp.min(jnp.where(vals == t1, lane_f, float(LANES)), axis=-1, keepdims=True)
            vals2 = jnp.where(lane_f == i1, ninf, vals)
            t2 = jnp.max(vals2, axis=-1, keepdims=True)
            i2 = jnp.min(jnp.where(vals2 == t2, lane_f, float(LANES)), axis=-1, keepdims=True)
            return t1, i1, t2, i2

        per = N_EXPERTS // N_GROUPS
        best, gi = None, None
        for g in range(N_GROUPS):
            ing = (lane >= g * per) & (lane < (g + 1) * per)
            t1, _, t2, _ = top2(jnp.where(ing, biased, ninf))
            gs = t1 + t2
            if best is None:
                best, gi = gs, jnp.zeros_like(gs)
            else:
                better = gs > best
                best = jnp.where(better, gs, best)
                gi = jnp.where(better, float(g), gi)
        lane_grp = (lane >> 2).astype(F32)
        vals = jnp.where(lane < N_EXPERTS, jnp.where(lane_grp == gi, biased, NEG), ninf)
        _, i1, _, i2 = top2(vals)
        w1 = jnp.sum(jnp.where(lane_f == i1, scores, 0.0), axis=-1, keepdims=True)
        w2 = jnp.sum(jnp.where(lane_f == i2, scores, 0.0), axis=-1, keepdims=True)
        den = w1 + w2
        ei_ref[...] = jnp.where(lane == 0, i1, jnp.where(lane == 1, i2, 0.0)).astype(jnp.int32)
        wt_ref[...] = jnp.where(lane == 0, w1 / den, jnp.where(lane == 1, w2 / den, 0.0))


def _merge_call(X, h, outs, wg, bg, wb, wo, modl, norm_f, wr_p, br_p, rows, n_batch, n_lat, tm, tn):
    D = X.shape[1]
    bw = wb.shape[1]
    tpb = n_lat // tm

    def rowmap(i, j):
        return (i, 0)

    return pl.pallas_call(
        _merge_kernel,
        grid=(rows // tm, D // tn),
        in_specs=[
            pl.BlockSpec((tm, D), rowmap),
            pl.BlockSpec((tm, D), rowmap),
        ] + [pl.BlockSpec((tm, bw), rowmap)] * 4 + [
            pl.BlockSpec((4, D, tn), lambda i, j: (0, 0, j)),
            pl.BlockSpec((4, 1, tn), lambda i, j: (0, 0, j)),
            pl.BlockSpec((4, bw, tn), lambda i, j: (0, 0, j)),
            pl.BlockSpec((tn, D), lambda i, j: (j, 0)),
            _mod_spec(2, tpb, n_batch, D),
            pl.BlockSpec((1, D), lambda i, j: (0, 0)),
            _mod_spec(3, tpb, n_batch, D),
            _mod_spec(4, tpb, n_batch, D),
            pl.BlockSpec((2, D, LANES), lambda i, j: (0, 0, 0)),
            pl.BlockSpec((1, LANES), lambda i, j: (0, 0)),
        ],
        out_specs=[
            pl.BlockSpec((tm, D), rowmap),
            pl.BlockSpec((tm, D), rowmap),
            pl.BlockSpec((tm, LANES), rowmap),
            pl.BlockSpec((tm, LANES), rowmap),
        ],
        out_shape=[
            jax.ShapeDtypeStruct((rows, D), F32),
            jax.ShapeDtypeStruct((rows, D), F32),
            jax.ShapeDtypeStruct((rows, LANES), jnp.int32),
            jax.ShapeDtypeStruct((rows, LANES), F32),
        ],
        scratch_shapes=[pltpu.VMEM((tm, D), F32)],
        compiler_params=_cparams(("parallel", "arbitrary")),
        name="merge_residual_router",
    )(X, h, *outs, wg, bg, wb, wo, modl, norm_f.reshape(1, D), modl, modl, wr_p, br_p)


def _row_dma_loop(n_rows, make_copy):
    def issue(r, carry):
        for k in range(2):
            make_copy(r, k).start()
        return carry

    def drain(r, carry):
        for k in range(2):
            make_copy(r, k).wait()
        return carry

    lax.fori_loop(0, n_rows, issue, 0)
    lax.fori_loop(0, n_rows, drain, 0)


def _scatter_kernel(pos_ref, h_ref, xs_in_ref, xs_ref, sem):
    del xs_in_ref

    def make_copy(r, k):
        return pltpu.make_async_copy(h_ref.at[pl.ds(r, 1), :],
                                     xs_ref.at[pl.ds(pos_ref[2 * r + k], 1), :], sem)

    _row_dma_loop(h_ref.shape[0], make_copy)


def _scatter_call(h2, pos, n_slots, tm):
    rows, D = h2.shape
    return pl.pallas_call(
        _scatter_kernel,
        grid=(rows // tm,),
        in_specs=[
            pl.BlockSpec((2 * tm,), lambda i: (i,), memory_space=pltpu.SMEM),
            pl.BlockSpec((tm, D), lambda i: (i, 0)),
            pl.BlockSpec(memory_space=pl.ANY),
        ],
        out_specs=pl.BlockSpec(memory_space=pl.ANY),
        out_shape=jax.ShapeDtypeStruct((n_slots, D), F32),
        scratch_shapes=[pltpu.SemaphoreType.DMA(())],
        input_output_aliases={2: 0},
        compiler_params=_cparams(("arbitrary",)),
        name="moe_scatter_rows",
    )(pos, h2, jnp.zeros((n_slots, D), F32))


def _experts_kernel(te_ref, nu_ref, x_ref, w1_ref, w3_ref, w2_ref, y_ref):
    del te_ref
    i = pl.program_id(0)

    @pl.when(i < nu_ref[0])
    def _():
        x = x_ref[...].astype(BF)
        a = _dot(x, w1_ref[...])
        b = _dot(x, w3_ref[...])
        hid = (a * _sigmoid(a)) * b
        y_ref[...] = _dot(hid.astype(BF), w2_ref[...])

    @pl.when(i >= nu_ref[0])
    def _():
        y_ref[...] = jnp.zeros_like(y_ref)


def _experts_call(xs, tile_expert, n_used, w1, w3, w2, tg):
    S, D = xs.shape
    de = w1.shape[2]
    grid_spec = pltpu.PrefetchScalarGridSpec(
        num_scalar_prefetch=2,
        grid=(S // tg,),
        in_specs=[
            pl.BlockSpec((tg, D), lambda i, te, nu: (i, 0)),
            pl.BlockSpec((None, D, de), lambda i, te, nu: (te[i], 0, 0)),
            pl.BlockSpec((None, D, de), lambda i, te, nu: (te[i], 0, 0)),
            pl.BlockSpec((None, de, D), lambda i, te, nu: (te[i], 0, 0)),
        ],
        out_specs=pl.BlockSpec((tg, D), lambda i, te, nu: (i, 0)),
    )
    return pl.pallas_call(
        _experts_kernel,
        grid_spec=grid_spec,
        out_shape=jax.ShapeDtypeStruct((S, D), F32),
        compiler_params=_cparams(("arbitrary",)),
        name="moe_experts",
    )(tile_expert, n_used, xs, w1, w3, w2)


def _combine_kernel(pos_ref, x_ref, wt_ref, g2_ref, ys_ref, o_ref, buf, sem):
    def make_copy(r, k):
        return pltpu.make_async_copy(ys_ref.at[pl.ds(pos_ref[2 * r + k], 1), :],
                                     buf.at[k, pl.ds(r, 1), :], sem)

    _row_dma_loop(x_ref.shape[0], make_copy)
    wt = wt_ref[...]
    moe = wt[:, 0:1] * buf[0] + wt[:, 1:2] * buf[1]
    o_ref[...] = x_ref[...] + g2_ref[...] * moe


def _combine_call(X, ys, pos, wts, modl, n_batch, n_lat, tm):
    rows, D = X.shape
    tpb = n_lat // tm
    return pl.pallas_call(
        _combine_kernel,
        grid=(rows // tm,),
        in_specs=[
            pl.BlockSpec((2 * tm,), lambda i: (i,), memory_space=pltpu.SMEM),
            pl.BlockSpec((tm, D), lambda i: (i, 0)),
            pl.BlockSpec((tm, LANES), lambda i: (i, 0)),
            _mod_spec(5, tpb, n_batch, D),
            pl.BlockSpec(memory_space=pl.ANY),
        ],
        out_specs=pl.BlockSpec((tm, D), lambda i: (i, 0)),
        out_shape=jax.ShapeDtypeStruct((rows, D), F32),
        scratch_shapes=[pltpu.VMEM((2, tm, D), F32), pltpu.SemaphoreType.DMA(())],
        compiler_params=_cparams(("arbitrary",)),
        name="moe_combine_residual",
    )(pos, X, wts, modl, ys)


def _route_positions(eidx, tg):
    rows = eidx.shape[0]
    e_flat = eidx.reshape(-1)
    onehot = (e_flat[:, None] == jnp.arange(N_EXPERTS, dtype=jnp.int32)[None, :]).astype(jnp.int32)
    csum = jnp.cumsum(onehot, axis=0)
    counts = csum[-1]
    rank = jnp.sum((csum - onehot) * onehot, axis=1)
    padded = ((counts + tg - 1) // tg) * tg
    ends = jnp.cumsum(padded)
    offsets = ends - padded
    pos = jnp.sum(onehot * offsets[None, :], axis=1) + rank
    n_tiles = (2 * rows + N_EXPERTS * tg) // tg
    tile_start = jnp.arange(n_tiles, dtype=jnp.int32) * tg
    tile_expert = jnp.sum((tile_start[:, None] >= ends[None, :]).astype(jnp.int32), axis=1)
    n_used = (ends[-1] // tg).astype(jnp.int32)
    last_e = jnp.sum((((n_used - 1) * tg) >= ends).astype(jnp.int32))
    tile_expert = jnp.where(jnp.arange(n_tiles) < n_used, tile_expert, last_e).astype(jnp.int32)
    return pos.astype(jnp.int32), tile_expert, n_used.reshape(1)


def _rope_tables(n_lat, tm):
    t = jnp.arange(n_lat)
    rows, cols = t // GRID_W, t % GRID_W

    def tab(d2, pos):
        freqs = ROPE_THETA ** (-jnp.arange(d2, dtype=F32) / d2)
        ang = pos.astype(F32)[:, None] * freqs[None, :]
        return jnp.cos(ang), jnp.sin(ang)

    cr, sr = tab(32, rows)
    cc, sc = tab(32, cols)
    cos_a = jnp.concatenate([cr, cr, cc, cc], axis=-1)
    sin_a = jnp.concatenate([-sr, sr, -sc, sc], axis=-1)
    cr, sr = tab(16, rows)
    cc, sc = tab(16, cols)
    one, zero = jnp.ones((n_lat, 64), F32), jnp.zeros((n_lat, 64), F32)
    cos_m = jnp.concatenate([cr, cr, cc, cc, one], axis=-1)
    sin_m = jnp.concatenate([-sr, sr, -sc, sc, zero], axis=-1)
    ident_c, ident_s = jnp.ones((tm, LANES), F32), jnp.zeros((tm, LANES), F32)
    return tuple(jnp.concatenate([a, b], axis=0) for a, b in
                 ((cos_a, ident_c), (sin_a, ident_s), (cos_m, ident_c), (sin_m, ident_s)))


def _pad_mla_heads(w, n_heads):
    lead = w.shape[:-1]
    w = w.reshape(lead + (n_heads, MLA_QK))
    w = jnp.pad(w, [(0, 0)] * len(lead) + [(0, 0), (0, MLA_PAD - MLA_QK)])
    return w.reshape(lead + (n_heads * MLA_PAD,))


def kernel(x, c, ctx, c_ctx, w_mod, b_mod, norm_mix, norm_ffn, w_in, mla_qa_norm, mla_w_uq, mla_kva_norm,
           mla_w_ukv, qn_att, kn_att, qn_mla, kn_mla, qn_win, kn_win, qn_na, kn_na, win_sink, na_rpb,
           w_branch, w_gate, b_gate, w_out, w_router, b_router, moe_w1, moe_w3, moe_w2):
    B, N, D = x.shape
    n_ctx = ctx.shape[1]
    L = w_mod.shape[0]
    R = B * N + B * n_ctx
    tm = 512
    tm_moe = 256
    tg = 256
    tn_merge = min(256, D // 2)
    tq = min(512, N)
    tk = min(512, N // 2)

    X = jnp.concatenate([x.reshape(B * N, D), ctx.reshape(B * n_ctx, D)], axis=0)
    cvec = jnp.zeros((8, D), F32).at[:B].set(c).at[B].set(c_ctx)
    mod = _mod_call(cvec, w_mod, b_mod, min(1024, D)).reshape(L, 8, 6, 1, D)
    tabs = _rope_tables(N, tm)

    wr_hi = w_router.astype(BF)
    wr_lo = (w_router - wr_hi.astype(F32)).astype(BF)
    wr_p = jnp.pad(jnp.stack([wr_hi, wr_lo]), ((0, 0), (0, 0), (0, LANES - N_EXPERTS)))
    br_p = jnp.pad(b_router.astype(F32), (0, LANES - N_EXPERTS)).reshape(1, LANES)

    for l in range(L):
        last = l == L - 1
        rows = B * N if last else R
        modl = mod[l]
        w_in_l = w_in[l]
        w_in_p = jnp.concatenate(
            [w_in_l[:, :KR_END], jnp.zeros((D, 64), F32), w_in_l[:, KR_END:]], axis=1).astype(BF)
        wuq_p = _pad_mla_heads(mla_w_uq[l], 4).astype(BF)
        wukv = mla_w_ukv[l].astype(BF)
        gains = [g.reshape(1, -1) for g in (
            qn_att[l], kn_att[l], qn_win[l], kn_win[l], qn_na[l], kn_na[l],
            _pad_mla_heads(qn_mla[l], 1), _pad_mla_heads(kn_mla[l], 1), mla_qa_norm[l], mla_kva_norm[l])]

        h, proj = _proj_call(X, norm_mix[l], modl, w_in_p, B, N, tm, P_TOTAL // 2)
        (qa, ka, va, qm, km, vm, qw, kw, vw, qn, kn, vn) = _heads_call(proj, tabs, gains, wuq_p, wukv, B, N, tm)

        sink = win_sink[l].astype(F32)
        o_att = _flash_call(qa, ka, va, B, N, n_ctx, 2, HEAD_DIM, tq, tk, "dense_gqa")
        o_mla = _flash_call(qm, km, vm, B, N, n_ctx, 1, MLA_PAD, 2 * tq, tk, "latent_attn")
        o_win = _window_call(sink, qw, kw, vw, B, N, n_ctx, 2, tq)
        bias = _na_bias_tables(na_rpb[l], N // GRID_W)
        o_na = _na_call(qn, kn, vn, bias, B, N, n_ctx)
        outs = [o_att, o_mla, o_win, o_na]
        if not last:
            outs_c = [
                _ctx_attn_call(sink, qa, ka, va, B, N, n_ctx, 2, HEAD_DIM, HEAD_DIM, 2, False, True,
                               "ctx_dense_gqa"),
                _ctx_attn_call(sink, qm, km, vm, B, N, n_ctx, 1, MLA_PAD, HEAD_DIM, 2, False, True,
                               "ctx_latent_attn"),
                _ctx_attn_call(sink, qw, kw, vw, B, N, n_ctx, 2, HEAD_DIM, HEAD_DIM, 1, True, False,
                               "ctx_window_attn"),
                _ctx_attn_call(sink, qn, kn, vn, B, N, n_ctx, 1, HEAD_DIM, HEAD_DIM, 1, False, False,
                               "ctx_neighbourhood"),
            ]
            outs = [jnp.concatenate([a, b], axis=0) for a, b in zip(outs, outs_c)]

        X, h2, eidx, wts = _merge_call(
            X, h, outs, w_gate[l].astype(BF), b_gate[l].reshape(4, 1, D), w_branch[l].astype(BF),
            w_out[l].astype(BF), modl, norm_ffn[l], wr_p, br_p, rows, B, N, tm, tn_merge)

        pos, tile_expert, n_used = _route_positions(eidx[:, :2], tg)
        n_slots = 2 * rows + N_EXPERTS * tg
        xs = _scatter_call(h2, pos, n_slots, tm_moe)
        ys = _experts_call(xs, tile_expert, n_used, moe_w1[l].astype(BF), moe_w3[l].astype(BF),
                           moe_w2[l].astype(BF), tg)
        X = _combine_call(X, ys, pos, wts, modl, B, N, tm_moe)

    return X.reshape(B, N, D)
```

```python
import functools

import numpy as np
import jax
import jax.numpy as jnp
from jax import lax
from jax.experimental import pallas as pl
from jax.experimental.pallas import tpu as pltpu

BF = jnp.bfloat16
F32 = jnp.float32

GRID_W = 64
HEAD_DIM = 128
ROPE_THETA = 10000.0
EPS = 1e-6
NEG = -1e30
LOG2E = 1.4426950408889634
WINDOW = 128
NA_WIN_H = 8
NA_WIN_W = 16
N_EXPERTS = 16
N_GROUPS = 4
MLA_QK = 192
MLA_PAD = 256
LANES = 128
VMEM_LIMIT = 56 * 1024 * 1024

P_AQ, P_AK, P_AV = 0, 512, 768
P_MCQ, P_MCKV, P_MKR = 1024, 1408, 1664
P_WQ, P_WK, P_WV = 1792, 2304, 2560
P_NQ, P_NK, P_NV = 2816, 3328, 3840
P_TOTAL = 4352
KR_END = 1728


def _cparams(sem):
    return pltpu.CompilerParams(dimension_semantics=sem, vmem_limit_bytes=VMEM_LIMIT)


def _sigmoid(z):
    return 1.0 / (1.0 + jnp.exp(-z))


def _dot(a, b):
    return jnp.dot(a, b, preferred_element_type=F32)


def _dot_nt(a, b):
    return lax.dot_general(a, b, (((1,), (1,)), ((), ())), preferred_element_type=F32)


def _mod_kernel(c_ref, w_ref, b_ref, o_ref):
    c = c_ref[...]
    s = c * _sigmoid(c)
    o_ref[...] = _dot(s.astype(BF), w_ref[...].astype(BF)) + b_ref[...]


def _mod_call(cvec, w_mod, b_mod, tn):
    L, D, D6 = w_mod.shape
    return pl.pallas_call(
        _mod_kernel,
        grid=(L, D6 // tn),
        in_specs=[
            pl.BlockSpec((8, D), lambda l, j: (0, 0)),
            pl.BlockSpec((None, D, tn), lambda l, j: (l, 0, j)),
            pl.BlockSpec((None, 1, tn), lambda l, j: (l, 0, j)),
        ],
        out_specs=pl.BlockSpec((None, 8, tn), lambda l, j: (l, 0, j)),
        out_shape=jax.ShapeDtypeStruct((L, 8, D6), F32),
        compiler_params=_cparams(("parallel", "parallel")),
        name="adaln_mod",
    )(cvec, w_mod, b_mod.reshape(L, 1, D6))


def _mod_spec(comp, tiles_per_batch, n_batch, D):
    return pl.BlockSpec(
        (None, None, 1, D),
        lambda i, *_: (jnp.minimum(i // tiles_per_batch, n_batch), comp, 0, 0))


def _proj_kernel(x_ref, g_ref, sh_ref, sc_ref, w_ref, h_ref, o_ref, h_scr):
    @pl.when(pl.program_id(1) == 0)
    def _():
        x = x_ref[...]
        ms = jnp.mean(x * x, axis=-1, keepdims=True)
        xn = x * lax.rsqrt(ms + EPS) * g_ref[...]
        h = (xn * (1.0 + sc_ref[...]) + sh_ref[...]).astype(BF)
        h_scr[...] = h
        h_ref[...] = h

    o_ref[...] = _dot(h_scr[...], w_ref[...])


def _proj_call(X, norm_g, modl, w_in_p, n_batch, n_lat, tm, tn):
    R, D = X.shape
    P = w_in_p.shape[1]
    tpb = n_lat // tm
    return pl.pallas_call(
        _proj_kernel,
        grid=(R // tm, P // tn),
        in_specs=[
            pl.BlockSpec((tm, D), lambda i, j: (i, 0)),
            pl.BlockSpec((1, D), lambda i, j: (0, 0)),
            _mod_spec(0, tpb, n_batch, D),
            _mod_spec(1, tpb, n_batch, D),
            pl.BlockSpec((D, tn), lambda i, j: (0, j)),
        ],
        out_specs=[
            pl.BlockSpec((tm, D), lambda i, j: (i, 0)),
            pl.BlockSpec((tm, tn), lambda i, j: (i, j)),
        ],
        out_shape=[jax.ShapeDtypeStruct((R, D), BF), jax.ShapeDtypeStruct((R, P), F32)],
        scratch_shapes=[pltpu.VMEM((tm, D), BF)],
        compiler_params=_cparams(("parallel", "arbitrary")),
        name="norm_mod_proj",
    )(X, norm_g.reshape(1, D), modl, modl, w_in_p)


def _rms(x, g, n):
    ss = jnp.sum(x * x, axis=-1, keepdims=True)
    return x * lax.rsqrt(ss / n + EPS) * g


def _rope(x, cos, sin_signed, half):
    lane = lax.broadcasted_iota(jnp.int32, x.shape, 1)
    first = (lane & (2 * half - 1)) < half
    xr = jnp.where(first, pltpu.roll(x, LANES - half, 1), pltpu.roll(x, half, 1))
    return x * cos + xr * sin_signed


def _heads_kernel(p_ref, ca_ref, sa_ref, cm_ref, sm_ref,
                  qa_g, ka_g, qw_g, kw_g, qn_g, kn_g, qm_g, km_g, cqn_g, ckvn_g, wuq_ref, wukv_ref,
                  qa_ref, ka_ref, va_ref, qm_ref, km_ref, vm_ref,
                  qw_ref, kw_ref, vw_ref, qn_ref, kn_ref, vn_ref):
    hd = HEAD_DIM
    sc128 = HEAD_DIM ** -0.5
    sc192 = MLA_QK ** -0.5
    ca, sa, cm, sm = ca_ref[...], sa_ref[...], cm_ref[...], sm_ref[...]

    def sl(off, h, w=hd):
        return p_ref[:, off + h * w: off + (h + 1) * w]

    for (q_off, k_off, qg, kg, q_out, k_out, q_scale) in (
            (P_AQ, P_AK, qa_g, ka_g, qa_ref, ka_ref, sc128 * LOG2E),
            (P_WQ, P_WK, qw_g, kw_g, qw_ref, kw_ref, sc128)):
        for h in range(4):
            q = _rope(_rms(sl(q_off, h), qg[...], hd), ca, sa, 32)
            q_out[:, h * hd:(h + 1) * hd] = (q * q_scale).astype(BF)
        for h in range(2):
            k = _rope(_rms(sl(k_off, h), kg[...], hd), ca, sa, 32)
            k_out[:, h * hd:(h + 1) * hd] = k.astype(BF)
    vw_ref[...] = p_ref[:, P_WV:P_WV + 2 * hd].astype(BF)
    ones = jnp.ones((p_ref.shape[0], hd), BF)
    for h in range(2):
        va_ref[:, 2 * h * hd:(2 * h + 1) * hd] = sl(P_AV, h).astype(BF)
        va_ref[:, (2 * h + 1) * hd:(2 * h + 2) * hd] = ones

    for h in range(4):
        qn_ref[:, h * hd:(h + 1) * hd] = (_rms(sl(P_NQ, h), qn_g[...], hd) * sc128).astype(BF)
        kn_ref[:, h * hd:(h + 1) * hd] = _rms(sl(P_NK, h), kn_g[...], hd).astype(BF)
    vn_ref[...] = p_ref[:, P_NV:P_NV + 4 * hd].astype(BF)

    cq = _rms(p_ref[:, P_MCQ:P_MCQ + 384], cqn_g[...], 384)
    qf = _dot(cq.astype(BF), wuq_ref[...])
    ckv = _rms(p_ref[:, P_MCKV:P_MCKV + 256], ckvn_g[...], 256)
    kvf = _dot(ckv.astype(BF), wukv_ref[...])
    kr = p_ref[:, P_MKR:P_MKR + hd]
    kr_ss = jnp.sum(kr * kr, axis=-1, keepdims=True)
    qg0, qg1 = qm_g[:, :hd], qm_g[:, hd:]
    kg0, kg1 = km_g[:, :hd], km_g[:, hd:]
    for h in range(4):
        q0 = qf[:, h * MLA_PAD: h * MLA_PAD + hd]
        q1 = qf[:, h * MLA_PAD + hd: (h + 1) * MLA_PAD]
        ss = jnp.sum(q0 * q0, axis=-1, keepdims=True) + jnp.sum(q1 * q1, axis=-1, keepdims=True)
        r = lax.rsqrt(ss / MLA_QK + EPS)
        qm_ref[:, h * MLA_PAD: h * MLA_PAD + hd] = (q0 * r * qg0 * (sc192 * LOG2E)).astype(BF)
        qm_ref[:, h * MLA_PAD + hd: (h + 1) * MLA_PAD] = (
            _rope(q1 * r * qg1, cm, sm, 16) * (sc192 * LOG2E)).astype(BF)
        k0 = kvf[:, h * 2 * hd: h * 2 * hd + hd]
        ss = jnp.sum(k0 * k0, axis=-1, keepdims=True) + kr_ss
        r = lax.rsqrt(ss / MLA_QK + EPS)
        km_ref[:, h * MLA_PAD: h * MLA_PAD + hd] = (k0 * r * kg0).astype(BF)
        km_ref[:, h * MLA_PAD + hd: (h + 1) * MLA_PAD] = _rope(kr * r * kg1, cm, sm, 16).astype(BF)
        vm_ref[:, 2 * h * hd:(2 * h + 1) * hd] = kvf[:, h * 2 * hd + hd: (h + 1) * 2 * hd].astype(BF)
        vm_ref[:, (2 * h + 1) * hd:(2 * h + 2) * hd] = ones


def _heads_call(proj, tabs, gains, wuq_p, wukv, n_batch, n_lat, tm):
    R, P = proj.shape
    tpb = n_lat // tm
    tab_spec = pl.BlockSpec((tm, LANES), lambda i: (jnp.where(i < n_batch * tpb, i % tpb, tpb), 0))

    def full(a):
        return pl.BlockSpec(a.shape, lambda i: (0,) * a.ndim)

    widths = (512, 256, 512, 4 * MLA_PAD, 4 * MLA_PAD, 1024, 512, 256, 256, 512, 512, 512)
    return pl.pallas_call(
        _heads_kernel,
        grid=(R // tm,),
        in_specs=[pl.BlockSpec((tm, P), lambda i: (i, 0))] + [tab_spec] * 4
        + [full(g) for g in gains] + [full(wuq_p), full(wukv)],
        out_specs=[pl.BlockSpec((tm, w), lambda i: (i, 0)) for w in widths],
        out_shape=[jax.ShapeDtypeStruct((R, w), BF) for w in widths],
        compiler_params=_cparams(("parallel",)),
        name="head_prep",
    )(proj, *tabs, *gains, wuq_p, wukv)


def _flash_kernel(q_ref, k_ref, v_ref, kc_ref, vc_ref, o_ref, q_scr, s_scr, m_scr, acc_scr,
                  *, G, d, dv, w, unroll):
    tq = q_ref.shape[0]
    n_blk = k_ref.shape[0] // w
    for g in range(G):
        q_scr[g * tq:(g + 1) * tq, :] = q_ref[:, g * d:(g + 1) * d]
    m_scr[...] = jnp.full_like(m_scr, NEG)
    acc_scr[...] = jnp.zeros_like(acc_scr)

    def scores(k):
        return _dot_nt(q_scr[...], k)

    def absorb(s, v):
        m_prev = m_scr[...]
        m_next = jnp.maximum(m_prev, s.max(axis=1, keepdims=True))
        alpha = jnp.exp2(m_prev - m_next)
        p = jnp.concatenate(
            [jnp.exp2(s[:, c * LANES:(c + 1) * LANES] - m_next) for c in range(s.shape[1] // LANES)],
            axis=1)
        pv = _dot(p.astype(BF), v)
        acc_scr[...] = jnp.concatenate([alpha, alpha], axis=1) * acc_scr[...] + pv
        m_scr[...] = m_next

    absorb(scores(kc_ref[...]), vc_ref[...])
    s_scr[0] = scores(k_ref[0:w, :])

    def group(jj, carry):
        base = jj * (unroll * w)
        for u in range(unroll):
            cur = pl.multiple_of(base + u * w, w)
            nxt = pl.multiple_of(jnp.minimum(base + (u + 1) * w, (n_blk - 1) * w), w)
            s_scr[(u + 1) % 2] = scores(k_ref[pl.ds(nxt, w), :])
            absorb(s_scr[u % 2], v_ref[pl.ds(cur, w), :])
        return carry

    lax.fori_loop(0, n_blk // unroll, group, 0)

    acc = acc_scr[...]
    o = acc[:, :dv] / acc[:, dv:]
    for g in range(G):
        o_ref[:, g * dv:(g + 1) * dv] = o[g * tq:(g + 1) * tq].astype(BF)


def _flash_call(q, k, v, n_batch, n_lat, n_ctx, G, d, tq, w, name):
    dv = LANES
    hkv = k.shape[1] // d
    nq = n_lat // tq
    cb = (n_batch * n_lat) // n_ctx
    unroll = 4 if (n_lat // w) % 4 == 0 else 2
    assert (n_lat // w) % unroll == 0
    return pl.pallas_call(
        functools.partial(_flash_kernel, G=G, d=d, dv=dv, w=w, unroll=unroll),
        grid=(n_batch, hkv, nq),
        in_specs=[
            pl.BlockSpec((tq, G * d), lambda b, h, i: (b * nq + i, h)),
            pl.BlockSpec((n_lat, d), lambda b, h, i: (b, h)),
            pl.BlockSpec((n_lat, 2 * dv), lambda b, h, i: (b, h)),
            pl.BlockSpec((n_ctx, d), lambda b, h, i: (cb + b, h)),
            pl.BlockSpec((n_ctx, 2 * dv), lambda b, h, i: (cb + b, h)),
        ],
        out_specs=pl.BlockSpec((tq, G * dv), lambda b, h, i: (b * nq + i, h)),
        out_shape=jax.ShapeDtypeStruct((n_batch * n_lat, hkv * G * dv), BF),
        scratch_shapes=[pltpu.VMEM((G * tq, d), BF), pltpu.VMEM((2, G * tq, w), F32),
                        pltpu.VMEM((G * tq, LANES), F32), pltpu.VMEM((G * tq, 2 * dv), F32)],
        compiler_params=_cparams(("parallel", "parallel", "parallel")),
        name=name,
    )(q, k, v, k, v)


def _window_kernel(sink_ref, q_ref, kp_ref, km_ref, kn_ref, vp_ref, vm_ref, vn_ref, kc_ref, vc_ref,
                   o_ref, *, G):
    hd = HEAD_DIM
    h = pl.program_id(1)
    i = pl.program_id(2)
    ni = pl.num_programs(2)
    tq = q_ref.shape[0]
    kcat = jnp.concatenate([kp_ref[...], km_ref[...], kn_ref[...]], axis=0)
    vcat = jnp.concatenate([vp_ref[...], vm_ref[...], vn_ref[...]], axis=0)
    r = lax.broadcasted_iota(jnp.int32, (tq, tq + 2 * WINDOW), 0)
    c = lax.broadcasted_iota(jnp.int32, (tq, tq + 2 * WINDOW), 1)
    rel = c - WINDOW - r
    lo = jnp.where(i > 0, 0, WINDOW)
    hi = jnp.where(i < ni - 1, tq + 2 * WINDOW, tq + WINDOW)
    valid = (jnp.abs(rel) <= WINDOW) & (c >= lo) & (c < hi)
    for g in range(G):
        q = q_ref[:, g * hd:(g + 1) * hd]
        s = jnp.where(valid, _dot_nt(q, kcat), NEG)
        sc = _dot_nt(q, kc_ref[...])
        sk = sink_ref[h * G + g]
        m = jnp.maximum(jnp.max(s, axis=-1, keepdims=True), jnp.max(sc, axis=-1, keepdims=True))
        m = jnp.maximum(m, sk)
        p = jnp.exp(s - m)
        pc = jnp.exp(sc - m)
        l = jnp.sum(p, axis=-1, keepdims=True) + jnp.sum(pc, axis=-1, keepdims=True) + jnp.exp(sk - m)
        o = _dot(p.astype(BF), vcat) + _dot(pc.astype(BF), vc_ref[...])
        o_ref[:, g * hd:(g + 1) * hd] = (o / l).astype(BF)


def _window_call(sink, q, k, v, n_batch, n_lat, n_ctx, G, tq):
    hd = HEAD_DIM
    hkv = k.shape[1] // hd
    nq = n_lat // tq
    bpt = tq // WINDOW
    nblk = k.shape[0] // WINDOW
    cb = (n_batch * n_lat) // n_ctx

    def prev(b, h, i):
        return (jnp.maximum((b * nq + i) * bpt - 1, 0), h)

    def nxt(b, h, i):
        return (jnp.minimum((b * nq + i + 1) * bpt, nblk - 1), h)

    def main(b, h, i):
        return (b * nq + i, h)

    def ctxb(b, h, i):
        return (cb + b, h)

    return pl.pallas_call(
        functools.partial(_window_kernel, G=G),
        grid=(n_batch, hkv, nq),
        in_specs=[
            pl.BlockSpec(memory_space=pltpu.SMEM),
            pl.BlockSpec((tq, G * hd), main),
            pl.BlockSpec((WINDOW, hd), prev), pl.BlockSpec((tq, hd), main), pl.BlockSpec((WINDOW, hd), nxt),
            pl.BlockSpec((WINDOW, hd), prev), pl.BlockSpec((tq, hd), main), pl.BlockSpec((WINDOW, hd), nxt),
            pl.BlockSpec((n_ctx, hd), ctxb), pl.BlockSpec((n_ctx, hd), ctxb),
        ],
        out_specs=pl.BlockSpec((tq, G * hd), main),
        out_shape=jax.ShapeDtypeStruct((n_batch * n_lat, hkv * G * hd), BF),
        compiler_params=_cparams(("parallel", "parallel", "parallel")),
        name="window_attn",
    )(sink, q, k, k, k, v, v, v, k, v)


NA_TQ = 8 * GRID_W
NA_KB = 4 * GRID_W
NA_NKB = 4


def _na_kernel(q_ref, k0, k1, k2, k3, v0, v1, v2, v3, kc_ref, vc_ref, bias_ref, o_ref):
    q = q_ref[...]
    kcat = jnp.concatenate([k0[...], k1[...], k2[...], k3[...]], axis=0)
    vcat = jnp.concatenate([v0[...], v1[...], v2[...], v3[...]], axis=0)
    s = _dot_nt(q, kcat) + bias_ref[...]
    sc = _dot_nt(q, kc_ref[...])
    m = jnp.maximum(jnp.max(s, axis=-1, keepdims=True), jnp.max(sc, axis=-1, keepdims=True))
    p = jnp.exp(s - m)
    pc = jnp.exp(sc - m)
    l = jnp.sum(p, axis=-1, keepdims=True) + jnp.sum(pc, axis=-1, keepdims=True)
    o = _dot(p.astype(BF), vcat) + _dot(pc.astype(BF), vc_ref[...])
    o_ref[...] = (o / l).astype(BF)


def _na_bias_tables(rpb, n_rows):
    H = rpb.shape[0]
    J = n_rows // 8
    nb = n_rows // 4
    kk = np.arange(NA_NKB * NA_KB)
    blk, within = kk // NA_KB, kk % NA_KB
    krl, kc = within // GRID_W, within % GRID_W
    qr = np.arange(8)
    cq = np.arange(GRID_W)
    cs = np.clip(cq - NA_WIN_W // 2, 0, GRID_W - NA_WIN_W)
    dx = np.arange(GRID_W)[None, :] - cq[:, None] + NA_WIN_W - 1
    col_ok = (np.arange(GRID_W)[None, :] >= cs[:, None]) & (np.arange(GRID_W)[None, :] < cs[:, None] + NA_WIN_W)
    dx1h = np.zeros((2 * NA_WIN_W - 1, GRID_W * GRID_W), np.float32)
    dxc = np.clip(dx, 0, 2 * NA_WIN_W - 2).reshape(-1)
    dx1h[dxc, np.arange(GRID_W * GRID_W)] = 1.0
    out = []
    for jv in (0, min(1, J - 1), J - 1):
        r = 8 * jv + qr
        kb_un = 2 * jv - 1 + np.arange(NA_NKB)
        kb = np.clip(kb_un, 0, nb - 1)
        dup = kb != kb_un
        krow = (4 * kb[:, None] + np.arange(4)[None, :]).reshape(-1)
        krow_dup = np.repeat(dup, 4)
        rs = np.clip(r - NA_WIN_H // 2, 0, n_rows - NA_WIN_H)
        row_ok = (krow[None, :] >= rs[:, None]) & (krow[None, :] < rs[:, None] + NA_WIN_H) & ~krow_dup[None, :]
        dy = np.clip(krow[None, :] - r[:, None] + NA_WIN_H - 1, 0, 2 * NA_WIN_H - 2)
        dy1h = np.zeros((8 * 16, 2 * NA_WIN_H - 1), np.float32)
        dy1h[np.arange(8 * 16), dy.reshape(-1)] = 1.0
        t = jnp.einsum("ay,hyx,xb->hab", jnp.asarray(dy1h), rpb.astype(F32), jnp.asarray(dx1h),
                       precision=lax.Precision.HIGHEST)
        t = t.reshape(H, 8, 16, GRID_W, GRID_W)
        ok = row_ok[:, :, None, None] & col_ok[None, None, :, :]
        t = jnp.where(jnp.asarray(ok)[None], t, NEG)
        t = t.transpose(0, 1, 3, 2, 4).reshape(H, NA_TQ, NA_NKB * NA_KB)
        out.append(t)
    del krl, kc, blk
    return jnp.stack(out, axis=1)


def _na_call(q, k, v, bias, n_batch, n_lat, n_ctx):
    hd = HEAD_DIM
    H = q.shape[1] // hd
    J = n_lat // NA_TQ
    nb = n_lat // NA_KB
    cb = (n_batch * n_lat) // n_ctx

    def kblk(t):
        return lambda b, h, j: (b * nb + jnp.clip(2 * j - 1 + t, 0, nb - 1), h)

    def qmap(b, h, j):
        return (b * J + j, h)

    def ctxb(b, h, j):
        return (cb + b, h)

    def bmap(b, h, j):
        return (h, jnp.where(j == 0, 0, jnp.where(j == J - 1, 2, 1)), 0, 0)

    return pl.pallas_call(
        _na_kernel,
        grid=(n_batch, H, J),
        in_specs=[pl.BlockSpec((NA_TQ, hd), qmap)]
        + [pl.BlockSpec((NA_KB, hd), kblk(t)) for t in range(NA_NKB)]
        + [pl.BlockSpec((NA_KB, hd), kblk(t)) for t in range(NA_NKB)]
        + [pl.BlockSpec((n_ctx, hd), ctxb), pl.BlockSpec((n_ctx, hd), ctxb),
           pl.BlockSpec((None, None, NA_TQ, NA_NKB * NA_KB), bmap)],
        out_specs=pl.BlockSpec((NA_TQ, hd), qmap),
        out_shape=jax.ShapeDtypeStruct((n_batch * n_lat, H * hd), BF),
        compiler_params=_cparams(("parallel", "parallel", "parallel")),
        name="neighbourhood_attn",
    )(q, k, k, k, k, v, v, v, v, k, v, bias)


def _ctx_attn_kernel(sink_ref, q_ref, k_ref, v_ref, o_ref, *, use_sink, base2):
    ex = jnp.exp2 if base2 else jnp.exp
    s = _dot_nt(q_ref[...], k_ref[...])
    m = jnp.max(s, axis=-1, keepdims=True)
    if use_sink:
        sk = sink_ref[pl.program_id(1)]
        m = jnp.maximum(m, sk)
    p = ex(s - m)
    l = jnp.sum(p, axis=-1, keepdims=True)
    if use_sink:
        l = l + ex(sk - m)
    o_ref[...] = (_dot(p.astype(BF), v_ref[...]) / l).astype(BF)


def _ctx_attn_call(sink, q, k, v, n_batch, n_lat, n_ctx, G, d, dv, v_stride, use_sink, base2, name):
    H = q.shape[1] // d
    cb = (n_batch * n_lat) // n_ctx
    return pl.pallas_call(
        functools.partial(_ctx_attn_kernel, use_sink=use_sink, base2=base2),
        grid=(n_batch, H),
        in_specs=[
            pl.BlockSpec(memory_space=pltpu.SMEM),
            pl.BlockSpec((n_ctx, d), lambda b, h: (cb + b, h)),
            pl.BlockSpec((n_ctx, d), lambda b, h: (cb + b, h // G)),
            pl.BlockSpec((n_ctx, dv), lambda b, h: (cb + b, (h // G) * v_stride)),
        ],
        out_specs=pl.BlockSpec((n_ctx, dv), lambda b, h: (b, h)),
        out_shape=jax.ShapeDtypeStruct((n_batch * n_ctx, H * dv), BF),
        compiler_params=_cparams(("parallel", "parallel")),
        name=name,
    )(sink, q, k, v)


def _merge_kernel(x_ref, h_ref, oa_ref, om_ref, ow_ref, on_ref, wg_ref, bg_ref, wb_ref, wo_ref,
                  g1_ref, nf_ref, sh2_ref, sc2_ref, wr_ref, br_ref,
                  xo_ref, h2_ref, ei_ref, wt_ref, y_scr):
    j = pl.program_id(1)

    @pl.when(j == 0)
    def _():
        y_scr[...] = jnp.zeros_like(y_scr)

    h = h_ref[...]
    acc = None
    for i, o_ref in enumerate((oa_ref, om_ref, ow_ref, on_ref)):
        gate = _sigmoid(_dot(h, wg_ref[i]) + bg_ref[i])
        y = gate * _dot(o_ref[...], wb_ref[i])
        acc = y if acc is None else acc + y
    y_scr[...] += _dot(acc.astype(BF), wo_ref[...])

    @pl.when(j == pl.num_programs(1) - 1)
    def _():
        x = x_ref[...] + g1_ref[...] * y_scr[...]
        xo_ref[...] = x
        ms = jnp.mean(x * x, axis=-1, keepdims=True)
        h2 = x * lax.rsqrt(ms + EPS) * nf_ref[...]
        h2 = h2 * (1.0 + sc2_ref[...]) + sh2_ref[...]
        h2_ref[...] = h2
        hi = h2.astype(BF)
        lo = (h2 - hi.astype(F32)).astype(BF)
        logits = _dot(hi, wr_ref[0]) + (_dot(lo, wr_ref[0]) + _dot(hi, wr_ref[1]))
        scores = _sigmoid(logits)
        biased = scores + br_ref[...]
        lane = lax.broadcasted_iota(jnp.int32, logits.shape, 1)
        lane_f = lane.astype(F32)
        ninf = -jnp.inf

        def top2(vals):
            t1 = jnp.max(vals, axis=-1, keepdims=True)
            i1 = jnp.min(jnp.where(vals == t1, lane_f, float(LANES)), axis=-1, keepdims=True)
            vals2 = jnp.where(lane_f == i1, ninf, vals)
            t2 = jnp.max(vals2, axis=-1, keepdims=True)
            i2 = jnp.min(jnp.where(vals2 == t2, lane_f, float(LANES)), axis=-1, keepdims=True)
            return t1, i1, t2, i2

        per = N_EXPERTS // N_GROUPS
        best, gi = None, None
        for g in range(N_GROUPS):
            ing = (lane >= g * per) & (lane < (g + 1) * per)
            t1, _, t2, _ = top2(jnp.where(ing, biased, ninf))
            gs = t1 + t2
            if best is None:
                best, gi = gs, jnp.zeros_like(gs)
            else:
                better = gs > best
                best = jnp.where(better, gs, best)
                gi = jnp.where(better, float(g), gi)
        lane_grp = (lane >> 2).astype(F32)
        vals = jnp.where(lane < N_EXPERTS, jnp.where(lane_grp == gi, biased, NEG), ninf)
        _, i1, _, i2 = top2(vals)
        w1 = jnp.sum(jnp.where(lane_f == i1, scores, 0.0), axis=-1, keepdims=True)
        w2 = jnp.sum(jnp.where(lane_f == i2, scores, 0.0), axis=-1, keepdims=True)
        den = w1 + w2
        ei_ref[...] = jnp.where(lane == 0, i1, jnp.where(lane == 1, i2, 0.0)).astype(jnp.int32)
        wt_ref[...] = jnp.where(lane == 0, w1 / den, jnp.where(lane == 1, w2 / den, 0.0))


def _merge_call(X, h, outs, wg, bg, wb, wo, modl, norm_f, wr_p, br_p, rows, n_batch, n_lat, tm, tn):
    D = X.shape[1]
    bw = wb.shape[1]
    tpb = n_lat // tm

    def rowmap(i, j):
        return (i, 0)

    return pl.pallas_call(
        _merge_kernel,
        grid=(rows // tm, D // tn),
        in_specs=[
            pl.BlockSpec((tm, D), rowmap),
            pl.BlockSpec((tm, D), rowmap),
        ] + [pl.BlockSpec((tm, bw), rowmap)] * 4 + [
            pl.BlockSpec((4, D, tn), lambda i, j: (0, 0, j)),
            pl.BlockSpec((4, 1, tn), lambda i, j: (0, 0, j)),
            pl.BlockSpec((4, bw, tn), lambda i, j: (0, 0, j)),
            pl.BlockSpec((tn, D), lambda i, j: (j, 0)),
            _mod_spec(2, tpb, n_batch, D),
            pl.BlockSpec((1, D), lambda i, j: (0, 0)),
            _mod_spec(3, tpb, n_batch, D),
            _mod_spec(4, tpb, n_batch, D),
            pl.BlockSpec((2, D, LANES), lambda i, j: (0, 0, 0)),
            pl.BlockSpec((1, LANES), lambda i, j: (0, 0)),
        ],
        out_specs=[
            pl.BlockSpec((tm, D), rowmap),
            pl.BlockSpec((tm, D), rowmap),
            pl.BlockSpec((tm, LANES), rowmap),
            pl.BlockSpec((tm, LANES), rowmap),
        ],
        out_shape=[
            jax.ShapeDtypeStruct((rows, D), F32),
            jax.ShapeDtypeStruct((rows, D), F32),
            jax.ShapeDtypeStruct((rows, LANES), jnp.int32),
            jax.ShapeDtypeStruct((rows, LANES), F32),
        ],
        scratch_shapes=[pltpu.VMEM((tm, D), F32)],
        compiler_params=_cparams(("parallel", "arbitrary")),
        name="merge_residual_router",
    )(X, h, *outs, wg, bg, wb, wo, modl, norm_f.reshape(1, D), modl, modl, wr_p, br_p)


def _row_dma_loop(n_rows, make_copy):
    def issue(r, carry):
        for k in range(2):
            make_copy(r, k).start()
        return carry

    def drain(r, carry):
        for k in range(2):
            make_copy(r, k).wait()
        return carry

    lax.fori_loop(0, n_rows, issue, 0, unroll=8)
    lax.fori_loop(0, n_rows, drain, 0, unroll=8)


def _scatter_kernel(pos_ref, h_ref, xs_in_ref, xs_ref, sem):
    del xs_in_ref

    def make_copy(r, k):
        return pltpu.make_async_copy(h_ref.at[pl.ds(r, 1), :],
                                     xs_ref.at[pl.ds(pos_ref[2 * r + k], 1), :], sem)

    _row_dma_loop(h_ref.shape[0], make_copy)


def _scatter_call(h2, pos, n_slots, tm):
    rows, D = h2.shape
    return pl.pallas_call(
        _scatter_kernel,
        grid=(rows // tm,),
        in_specs=[
            pl.BlockSpec((2 * tm,), lambda i: (i,), memory_space=pltpu.SMEM),
            pl.BlockSpec((tm, D), lambda i: (i, 0)),
            pl.BlockSpec(memory_space=pl.ANY),
        ],
        out_specs=pl.BlockSpec(memory_space=pl.ANY),
        out_shape=jax.ShapeDtypeStruct((n_slots, D), F32),
        scratch_shapes=[pltpu.SemaphoreType.DMA(())],
        input_output_aliases={2: 0},
        compiler_params=_cparams(("arbitrary",)),
        name="moe_scatter_rows",
    )(pos, h2, jnp.zeros((n_slots, D), F32))


def _experts_kernel(te_ref, nu_ref, x_ref, w1_ref, w3_ref, w2_ref, y_ref, w1_scr, w3_scr, w2_scr):
    i = pl.program_id(0)

    @pl.when((i == 0) | (te_ref[i] != te_ref[jnp.maximum(i - 1, 0)]))
    def _():
        w1_scr[...] = w1_ref[...].astype(BF)
        w3_scr[...] = w3_ref[...].astype(BF)
        w2_scr[...] = w2_ref[...].astype(BF)

    @pl.when(i < nu_ref[0])
    def _():
        x = x_ref[...].astype(BF)
        a = _dot(x, w1_scr[...])
        b = _dot(x, w3_scr[...])
        hid = (a * _sigmoid(a)) * b
        y_ref[...] = _dot(hid.astype(BF), w2_scr[...])

    @pl.when(i >= nu_ref[0])
    def _():
        y_ref[...] = jnp.zeros_like(y_ref)


def _experts_call(xs, tile_expert, n_used, w1, w3, w2, layer, tg):
    S, D = xs.shape
    de = w1.shape[3]
    grid_spec = pltpu.PrefetchScalarGridSpec(
        num_scalar_prefetch=2,
        grid=(S // tg,),
        in_specs=[
            pl.BlockSpec((tg, D), lambda i, te, nu: (i, 0)),
            pl.BlockSpec((None, None, D, de), lambda i, te, nu: (layer, te[i], 0, 0)),
            pl.BlockSpec((None, None, D, de), lambda i, te, nu: (layer, te[i], 0, 0)),
            pl.BlockSpec((None, None, de, D), lambda i, te, nu: (layer, te[i], 0, 0)),
        ],
        out_specs=pl.BlockSpec((tg, D), lambda i, te, nu: (i, 0)),
        scratch_shapes=[pltpu.VMEM((D, de), BF), pltpu.VMEM((D, de), BF), pltpu.VMEM((de, D), BF)],
    )
    return pl.pallas_call(
        _experts_kernel,
        grid_spec=grid_spec,
        out_shape=jax.ShapeDtypeStruct((S, D), F32),
        compiler_params=_cparams(("arbitrary",)),
        name="moe_experts",
    )(tile_expert, n_used, xs, w1, w3, w2)


def _combine_kernel(pos_ref, x_ref, wt_ref, g2_ref, ys_ref, o_ref, buf, sem):
    def make_copy(r, k):
        return pltpu.make_async_copy(ys_ref.at[pl.ds(pos_ref[2 * r + k], 1), :],
                                     buf.at[k, pl.ds(r, 1), :], sem)

    _row_dma_loop(x_ref.shape[0], make_copy)
    wt = wt_ref[...]
    moe = wt[:, 0:1] * buf[0] + wt[:, 1:2] * buf[1]
    o_ref[...] = x_ref[...] + g2_ref[...] * moe


def _combine_call(X, ys, pos, wts, modl, n_batch, n_lat, tm):
    rows, D = X.shape
    tpb = n_lat // tm
    return pl.pallas_call(
        _combine_kernel,
        grid=(rows // tm,),
        in_specs=[
            pl.BlockSpec((2 * tm,), lambda i: (i,), memory_space=pltpu.SMEM),
            pl.BlockSpec((tm, D), lambda i: (i, 0)),
            pl.BlockSpec((tm, LANES), lambda i: (i, 0)),
            _mod_spec(5, tpb, n_batch, D),
            pl.BlockSpec(memory_space=pl.ANY),
        ],
        out_specs=pl.BlockSpec((tm, D), lambda i: (i, 0)),
        out_shape=jax.ShapeDtypeStruct((rows, D), F32),
        scratch_shapes=[pltpu.VMEM((2, tm, D), F32), pltpu.SemaphoreType.DMA(())],
        compiler_params=_cparams(("arbitrary",)),
        name="moe_combine_residual",
    )(pos, X, wts, modl, ys)


def _route_positions(eidx, tg):
    rows = eidx.shape[0]
    e_flat = eidx.reshape(-1)
    onehot = (e_flat[:, None] == jnp.arange(N_EXPERTS, dtype=jnp.int32)[None, :]).astype(jnp.int32)
    csum = jnp.cumsum(onehot, axis=0)
    counts = csum[-1]
    rank = jnp.sum((csum - onehot) * onehot, axis=1)
    padded = ((counts + tg - 1) // tg) * tg
    ends = jnp.cumsum(padded)
    offsets = ends - padded
    pos = jnp.sum(onehot * offsets[None, :], axis=1) + rank
    n_tiles = (2 * rows + N_EXPERTS * tg) // tg
    tile_start = jnp.arange(n_tiles, dtype=jnp.int32) * tg
    tile_expert = jnp.sum((tile_start[:, None] >= ends[None, :]).astype(jnp.int32), axis=1)
    n_used = (ends[-1] // tg).astype(jnp.int32)
    last_e = jnp.sum((((n_used - 1) * tg) >= ends).astype(jnp.int32))
    tile_expert = jnp.where(jnp.arange(n_tiles) < n_used, tile_expert, last_e).astype(jnp.int32)
    return pos.astype(jnp.int32), tile_expert, n_used.reshape(1)


def _rope_tables(n_lat, tm):
    t = jnp.arange(n_lat)
    rows, cols = t // GRID_W, t % GRID_W

    def tab(d2, pos):
        freqs = ROPE_THETA ** (-jnp.arange(d2, dtype=F32) / d2)
        ang = pos.astype(F32)[:, None] * freqs[None, :]
        return jnp.cos(ang), jnp.sin(ang)

    cr, sr = tab(32, rows)
    cc, sc = tab(32, cols)
    cos_a = jnp.concatenate([cr, cr, cc, cc], axis=-1)
    sin_a = jnp.concatenate([-sr, sr, -sc, sc], axis=-1)
    cr, sr = tab(16, rows)
    cc, sc = tab(16, cols)
    one, zero = jnp.ones((n_lat, 64), F32), jnp.zeros((n_lat, 64), F32)
    cos_m = jnp.concatenate([cr, cr, cc, cc, one], axis=-1)
    sin_m = jnp.concatenate([-sr, sr, -sc, sc, zero], axis=-1)
    ident_c, ident_s = jnp.ones((tm, LANES), F32), jnp.zeros((tm, LANES), F32)
    return tuple(jnp.concatenate([a, b], axis=0) for a, b in
                 ((cos_a, ident_c), (sin_a, ident_s), (cos_m, ident_c), (sin_m, ident_s)))


def _pad_mla_heads(w, n_heads):
    lead = w.shape[:-1]
    w = w.reshape(lead + (n_heads, MLA_QK))
    w = jnp.pad(w, [(0, 0)] * len(lead) + [(0, 0), (0, MLA_PAD - MLA_QK)])
    return w.reshape(lead + (n_heads * MLA_PAD,))


def kernel(x, c, ctx, c_ctx, w_mod, b_mod, norm_mix, norm_ffn, w_in, mla_qa_norm, mla_w_uq, mla_kva_norm,
           mla_w_ukv, qn_att, kn_att, qn_mla, kn_mla, qn_win, kn_win, qn_na, kn_na, win_sink, na_rpb,
           w_branch, w_gate, b_gate, w_out, w_router, b_router, moe_w1, moe_w3, moe_w2):
    B, N, D = x.shape
    n_ctx = ctx.shape[1]
    L = w_mod.shape[0]
    R = B * N + B * n_ctx
    tm = 512
    tm_moe = 256
    tg = 256
    tn_merge = min(256, D // 2)
    tq = min(512, N)
    tk = min(512, N // 2)

    X = jnp.concatenate([x.reshape(B * N, D), ctx.reshape(B * n_ctx, D)], axis=0)
    cvec = jnp.zeros((8, D), F32).at[:B].set(c).at[B].set(c_ctx)
    mod = _mod_call(cvec, w_mod, b_mod, min(1024, D)).reshape(L, 8, 6, 1, D)
    tabs = _rope_tables(N, tm)

    wr_hi = w_router.astype(BF)
    wr_lo = (w_router - wr_hi.astype(F32)).astype(BF)
    wr_p = jnp.pad(jnp.stack([wr_hi, wr_lo]), ((0, 0), (0, 0), (0, LANES - N_EXPERTS)))
    br_p = jnp.pad(b_router.astype(F32), (0, LANES - N_EXPERTS)).reshape(1, LANES)

    for l in range(L):
        last = l == L - 1
        rows = B * N if last else R
        modl = mod[l]
        w_in_l = w_in[l]
        w_in_p = jnp.concatenate(
            [w_in_l[:, :KR_END], jnp.zeros((D, 64), F32), w_in_l[:, KR_END:]], axis=1).astype(BF)
        wuq_p = _pad_mla_heads(mla_w_uq[l], 4).astype(BF)
        wukv = mla_w_ukv[l].astype(BF)
        gains = [g.reshape(1, -1) for g in (
            qn_att[l], kn_att[l], qn_win[l], kn_win[l], qn_na[l], kn_na[l],
            _pad_mla_heads(qn_mla[l], 1), _pad_mla_heads(kn_mla[l], 1), mla_qa_norm[l], mla_kva_norm[l])]

        h, proj = _proj_call(X, norm_mix[l], modl, w_in_p, B, N, tm, P_TOTAL // 2)
        (qa, ka, va, qm, km, vm, qw, kw, vw, qn, kn, vn) = _heads_call(proj, tabs, gains, wuq_p, wukv, B, N, tm)

        sink = win_sink[l].astype(F32)
        o_att = _flash_call(qa, ka, va, B, N, n_ctx, 2, HEAD_DIM, tq, tk, "dense_gqa")
        o_mla = _flash_call(qm, km, vm, B, N, n_ctx, 1, MLA_PAD, 2 * tq, tk, "latent_attn")
        o_win = _window_call(sink, qw, kw, vw, B, N, n_ctx, 2, tq)
        bias = _na_bias_tables(na_rpb[l], N // GRID_W)
        o_na = _na_call(qn, kn, vn, bias, B, N, n_ctx)
        outs = [o_att, o_mla, o_win, o_na]
        if not last:
            outs_c = [
                _ctx_attn_call(sink, qa, ka, va, B, N, n_ctx, 2, HEAD_DIM, HEAD_DIM, 2, False, True,
                               "ctx_dense_gqa"),
                _ctx_attn_call(sink, qm, km, vm, B, N, n_ctx, 1, MLA_PAD, HEAD_DIM, 2, False, True,
                               "ctx_latent_attn"),
                _ctx_attn_call(sink, qw, kw, vw, B, N, n_ctx, 2, HEAD_DIM, HEAD_DIM, 1, True, False,
                               "ctx_window_attn"),
                _ctx_attn_call(sink, qn, kn, vn, B, N, n_ctx, 1, HEAD_DIM, HEAD_DIM, 1, False, False,
                               "ctx_neighbourhood"),
            ]
            outs = [jnp.concatenate([a, b], axis=0) for a, b in zip(outs, outs_c)]

        X, h2, eidx, wts = _merge_call(
            X, h, outs, w_gate[l].astype(BF), b_gate[l].reshape(4, 1, D), w_branch[l].astype(BF),
            w_out[l].astype(BF), modl, norm_ffn[l], wr_p, br_p, rows, B, N, tm, tn_merge)

        pos, tile_expert, n_used = _route_positions(eidx[:, :2], tg)
        n_slots = 2 * rows + N_EXPERTS * tg
        xs = _scatter_call(h2, pos, n_slots, tm_moe)
        ys = _experts_call(xs, tile_expert, n_used, moe_w1, moe_w3, moe_w2, l, tg)
        X = _combine_call(X, ys, pos, wts, modl, B, N, tm_moe)

    return X.reshape(B, N, D)
```

```python
import functools

import numpy as np
import jax
import jax.numpy as jnp
from jax import lax
from jax.experimental import pallas as pl
from jax.experimental.pallas import tpu as pltpu

BF = jnp.bfloat16
F32 = jnp.float32

GRID_W = 64
HEAD_DIM = 128
ROPE_THETA = 10000.0
EPS = 1e-6
NEG = -1e30
LOG2E = 1.4426950408889634
WINDOW = 128
NA_WIN_H = 8
NA_WIN_W = 16
N_EXPERTS = 16
N_GROUPS = 4
MLA_QK = 192
MLA_PAD = 256
LANES = 128
VMEM_LIMIT = 56 * 1024 * 1024

P_AQ, P_AK, P_AV = 0, 512, 768
P_MCQ, P_MCKV, P_MKR = 1024, 1408, 1664
P_WQ, P_WK, P_WV = 1792, 2304, 2560
P_NQ, P_NK, P_NV = 2816, 3328, 3840
P_TOTAL = 4352
KR_END = 1728


def _cparams(sem):
    return pltpu.CompilerParams(dimension_semantics=sem, vmem_limit_bytes=VMEM_LIMIT)


def _sigmoid(z):
    return 1.0 / (1.0 + jnp.exp(-z))


def _dot(a, b):
    return jnp.dot(a, b, preferred_element_type=F32)


def _dot_nt(a, b):
    return lax.dot_general(a, b, (((1,), (1,)), ((), ())), preferred_element_type=F32)


def _mod_kernel(c_ref, w_ref, b_ref, o_ref):
    c = c_ref[...]
    s = c * _sigmoid(c)
    o_ref[...] = _dot(s.astype(BF), w_ref[...].astype(BF)) + b_ref[...]


def _mod_call(cvec, w_mod, b_mod, tn):
    L, D, D6 = w_mod.shape
    return pl.pallas_call(
        _mod_kernel,
        grid=(L, D6 // tn),
        in_specs=[
            pl.BlockSpec((8, D), lambda l, j: (0, 0)),
            pl.BlockSpec((None, D, tn), lambda l, j: (l, 0, j)),
            pl.BlockSpec((None, 1, tn), lambda l, j: (l, 0, j)),
        ],
        out_specs=pl.BlockSpec((None, 8, tn), lambda l, j: (l, 0, j)),
        out_shape=jax.ShapeDtypeStruct((L, 8, D6), F32),
        compiler_params=_cparams(("parallel", "parallel")),
        name="adaln_mod",
    )(cvec, w_mod, b_mod.reshape(L, 1, D6))


def _mod_spec(comp, tiles_per_batch, n_batch, D):
    return pl.BlockSpec(
        (None, None, 1, D),
        lambda i, *_: (jnp.minimum(i // tiles_per_batch, n_batch), comp, 0, 0))


def _proj_kernel(x_ref, g_ref, sh_ref, sc_ref, w_ref, h_ref, o_ref, h_scr):
    @pl.when(pl.program_id(1) == 0)
    def _():
        x = x_ref[...]
        ms = jnp.mean(x * x, axis=-1, keepdims=True)
        xn = x * lax.rsqrt(ms + EPS) * g_ref[...]
        h = (xn * (1.0 + sc_ref[...]) + sh_ref[...]).astype(BF)
        h_scr[...] = h
        h_ref[...] = h

    o_ref[...] = _dot(h_scr[...], w_ref[...])


def _proj_call(X, norm_g, modl, w_in_p, n_batch, n_lat, tm, tn):
    R, D = X.shape
    P = w_in_p.shape[1]
    tpb = n_lat // tm
    return pl.pallas_call(
        _proj_kernel,
        grid=(R // tm, P // tn),
        in_specs=[
            pl.BlockSpec((tm, D), lambda i, j: (i, 0)),
            pl.BlockSpec((1, D), lambda i, j: (0, 0)),
            _mod_spec(0, tpb, n_batch, D),
            _mod_spec(1, tpb, n_batch, D),
            pl.BlockSpec((D, tn), lambda i, j: (0, j)),
        ],
        out_specs=[
            pl.BlockSpec((tm, D), lambda i, j: (i, 0)),
            pl.BlockSpec((tm, tn), lambda i, j: (i, j)),
        ],
        out_shape=[jax.ShapeDtypeStruct((R, D), BF), jax.ShapeDtypeStruct((R, P), F32)],
        scratch_shapes=[pltpu.VMEM((tm, D), BF)],
        compiler_params=_cparams(("parallel", "arbitrary")),
        name="norm_mod_proj",
    )(X, norm_g.reshape(1, D), modl, modl, w_in_p)


def _rms(x, g, n):
    ss = jnp.sum(x * x, axis=-1, keepdims=True)
    return x * lax.rsqrt(ss / n + EPS) * g


def _rope(x, cos, sin_signed, half):
    lane = lax.broadcasted_iota(jnp.int32, x.shape, 1)
    first = (lane & (2 * half - 1)) < half
    xr = jnp.where(first, pltpu.roll(x, LANES - half, 1), pltpu.roll(x, half, 1))
    return x * cos + xr * sin_signed


def _heads_kernel(p_ref, ca_ref, sa_ref, cm_ref, sm_ref,
                  qa_g, ka_g, qw_g, kw_g, qn_g, kn_g, qm_g, km_g, cqn_g, ckvn_g, wuq_ref, wukv_ref,
                  qa_ref, ka_ref, va_ref, qm_ref, km_ref, vm_ref,
                  qw_ref, kw_ref, vw_ref, qn_ref, kn_ref, vn_ref):
    hd = HEAD_DIM
    sc128 = HEAD_DIM ** -0.5 * LOG2E
    sc192 = MLA_QK ** -0.5 * LOG2E
    ca, sa, cm, sm = ca_ref[...], sa_ref[...], cm_ref[...], sm_ref[...]

    def sl(off, h, w=hd):
        return p_ref[:, off + h * w: off + (h + 1) * w]

    ones = jnp.ones((p_ref.shape[0], hd), BF)

    def put_values(v_out, v_off, n_heads):
        for h in range(n_heads):
            v_out[:, 2 * h * hd:(2 * h + 1) * hd] = sl(v_off, h).astype(BF)
            v_out[:, (2 * h + 1) * hd:(2 * h + 2) * hd] = ones

    for (q_off, k_off, v_off, qg, kg, q_out, k_out, v_out) in (
            (P_AQ, P_AK, P_AV, qa_g, ka_g, qa_ref, ka_ref, va_ref),
            (P_WQ, P_WK, P_WV, qw_g, kw_g, qw_ref, kw_ref, vw_ref)):
        for h in range(4):
            q = _rope(_rms(sl(q_off, h), qg[...], hd), ca, sa, 32)
            q_out[:, h * hd:(h + 1) * hd] = (q * sc128).astype(BF)
        for h in range(2):
            k = _rope(_rms(sl(k_off, h), kg[...], hd), ca, sa, 32)
            k_out[:, h * hd:(h + 1) * hd] = k.astype(BF)
        put_values(v_out, v_off, 2)

    for h in range(4):
        qn_ref[:, h * hd:(h + 1) * hd] = (_rms(sl(P_NQ, h), qn_g[...], hd) * sc128).astype(BF)
        kn_ref[:, h * hd:(h + 1) * hd] = _rms(sl(P_NK, h), kn_g[...], hd).astype(BF)
    put_values(vn_ref, P_NV, 4)

    cq = _rms(p_ref[:, P_MCQ:P_MCQ + 384], cqn_g[...], 384)
    qf = _dot(cq.astype(BF), wuq_ref[...])
    ckv = _rms(p_ref[:, P_MCKV:P_MCKV + 256], ckvn_g[...], 256)
    kvf = _dot(ckv.astype(BF), wukv_ref[...])
    kr = p_ref[:, P_MKR:P_MKR + hd]
    kr_ss = jnp.sum(kr * kr, axis=-1, keepdims=True)
    qg0, qg1 = qm_g[:, :hd], qm_g[:, hd:]
    kg0, kg1 = km_g[:, :hd], km_g[:, hd:]
    for h in range(4):
        q0 = qf[:, h * MLA_PAD: h * MLA_PAD + hd]
        q1 = qf[:, h * MLA_PAD + hd: (h + 1) * MLA_PAD]
        ss = jnp.sum(q0 * q0, axis=-1, keepdims=True) + jnp.sum(q1 * q1, axis=-1, keepdims=True)
        r = lax.rsqrt(ss / MLA_QK + EPS)
        qm_ref[:, h * MLA_PAD: h * MLA_PAD + hd] = (q0 * r * qg0 * sc192).astype(BF)
        qm_ref[:, h * MLA_PAD + hd: (h + 1) * MLA_PAD] = (
            _rope(q1 * r * qg1, cm, sm, 16) * sc192).astype(BF)
        k0 = kvf[:, h * 2 * hd: h * 2 * hd + hd]
        ss = jnp.sum(k0 * k0, axis=-1, keepdims=True) + kr_ss
        r = lax.rsqrt(ss / MLA_QK + EPS)
        km_ref[:, h * MLA_PAD: h * MLA_PAD + hd] = (k0 * r * kg0).astype(BF)
        km_ref[:, h * MLA_PAD + hd: (h + 1) * MLA_PAD] = _rope(kr * r * kg1, cm, sm, 16).astype(BF)
        vm_ref[:, 2 * h * hd:(2 * h + 1) * hd] = kvf[:, h * 2 * hd + hd: (h + 1) * 2 * hd].astype(BF)
        vm_ref[:, (2 * h + 1) * hd:(2 * h + 2) * hd] = ones


def _heads_call(proj, tabs, gains, wuq_p, wukv, n_batch, n_lat, tm):
    R, P = proj.shape
    tpb = n_lat // tm
    tab_spec = pl.BlockSpec((tm, LANES), lambda i: (jnp.where(i < n_batch * tpb, i % tpb, tpb), 0))

    def full(a):
        return pl.BlockSpec(a.shape, lambda i: (0,) * a.ndim)

    widths = (512, 256, 512, 4 * MLA_PAD, 4 * MLA_PAD, 1024, 512, 256, 512, 512, 512, 1024)
    return pl.pallas_call(
        _heads_kernel,
        grid=(R // tm,),
        in_specs=[pl.BlockSpec((tm, P), lambda i: (i, 0))] + [tab_spec] * 4
        + [full(g) for g in gains] + [full(wuq_p), full(wukv)],
        out_specs=[pl.BlockSpec((tm, w), lambda i: (i, 0)) for w in widths],
        out_shape=[jax.ShapeDtypeStruct((R, w), BF) for w in widths],
        compiler_params=_cparams(("parallel",)),
        name="head_prep",
    )(proj, *tabs, *gains, wuq_p, wukv)


def _flash_kernel(q_ref, k_ref, v_ref, kc_ref, vc_ref, o_ref, q_scr, s_scr, m_scr, acc_scr,
                  *, G, d, dv, w, unroll):
    tq = q_ref.shape[0]
    n_blk = k_ref.shape[0] // w
    for g in range(G):
        q_scr[g * tq:(g + 1) * tq, :] = q_ref[:, g * d:(g + 1) * d]
    m_scr[...] = jnp.full_like(m_scr, NEG)
    acc_scr[...] = jnp.zeros_like(acc_scr)

    def scores(k):
        return _dot_nt(q_scr[...], k)

    def absorb(s, v):
        m_prev = m_scr[...]
        m_next = jnp.maximum(m_prev, s.max(axis=1, keepdims=True))
        alpha = jnp.exp2(m_prev - m_next)
        p = jnp.concatenate(
            [jnp.exp2(s[:, c * LANES:(c + 1) * LANES] - m_next) for c in range(s.shape[1] // LANES)],
            axis=1)
        pv = _dot(p.astype(BF), v)
        acc_scr[...] = jnp.concatenate([alpha, alpha], axis=1) * acc_scr[...] + pv
        m_scr[...] = m_next

    absorb(scores(kc_ref[...]), vc_ref[...])
    s_scr[0] = scores(k_ref[0:w, :])

    def group(jj, carry):
        base = jj * (unroll * w)
        for u in range(unroll):
            cur = pl.multiple_of(base + u * w, w)
            nxt = pl.multiple_of(jnp.minimum(base + (u + 1) * w, (n_blk - 1) * w), w)
            s_scr[(u + 1) % 2] = scores(k_ref[pl.ds(nxt, w), :])
            absorb(s_scr[u % 2], v_ref[pl.ds(cur, w), :])
        return carry

    lax.fori_loop(0, n_blk // unroll, group, 0)

    acc = acc_scr[...]
    o = acc[:, :dv] / acc[:, dv:]
    for g in range(G):
        o_ref[:, g * dv:(g + 1) * dv] = o[g * tq:(g + 1) * tq].astype(BF)


def _flash_call(q, k, v, n_batch, n_lat, n_ctx, G, d, tq, w, name):
    dv = LANES
    hkv = k.shape[1] // d
    nq = n_lat // tq
    cb = (n_batch * n_lat) // n_ctx
    unroll = 4 if (n_lat // w) % 4 == 0 else 2
    assert (n_lat // w) % unroll == 0
    return pl.pallas_call(
        functools.partial(_flash_kernel, G=G, d=d, dv=dv, w=w, unroll=unroll),
        grid=(n_batch, hkv, nq),
        in_specs=[
            pl.BlockSpec((tq, G * d), lambda b, h, i: (b * nq + i, h)),
            pl.BlockSpec((n_lat, d), lambda b, h, i: (b, h)),
            pl.BlockSpec((n_lat, 2 * dv), lambda b, h, i: (b, h)),
            pl.BlockSpec((n_ctx, d), lambda b, h, i: (cb + b, h)),
            pl.BlockSpec((n_ctx, 2 * dv), lambda b, h, i: (cb + b, h)),
        ],
        out_specs=pl.BlockSpec((tq, G * dv), lambda b, h, i: (b * nq + i, h)),
        out_shape=jax.ShapeDtypeStruct((n_batch * n_lat, hkv * G * dv), BF),
        scratch_shapes=[pltpu.VMEM((G * tq, d), BF), pltpu.VMEM((2, G * tq, w), F32),
                        pltpu.VMEM((G * tq, LANES), F32), pltpu.VMEM((G * tq, 2 * dv), F32)],
        compiler_params=_cparams(("parallel", "parallel", "parallel")),
        name=name,
    )(q, k, v, k, v)


def _window_kernel(sink_ref, q_ref, kp_ref, km_ref, kn_ref, vp_ref, vm_ref, vn_ref, kc_ref, vc_ref,
                   o_ref, *, G):
    hd = HEAD_DIM
    h = pl.program_id(1)
    i = pl.program_id(2)
    ni = pl.num_programs(2)
    tq = q_ref.shape[0]
    kcat = jnp.concatenate([kp_ref[...], km_ref[...], kn_ref[...]], axis=0)
    vcat = jnp.concatenate([vp_ref[...], vm_ref[...], vn_ref[...]], axis=0)
    r = lax.broadcasted_iota(jnp.int32, (tq, tq + 2 * WINDOW), 0)
    c = lax.broadcasted_iota(jnp.int32, (tq, tq + 2 * WINDOW), 1)
    rel = c - WINDOW - r
    lo = jnp.where(i > 0, 0, WINDOW)
    hi = jnp.where(i < ni - 1, tq + 2 * WINDOW, tq + WINDOW)
    valid = (jnp.abs(rel) <= WINDOW) & (c >= lo) & (c < hi)
    for g in range(G):
        q = q_ref[:, g * hd:(g + 1) * hd]
        s = jnp.where(valid, _dot_nt(q, kcat), NEG)
        sc = _dot_nt(q, kc_ref[...])
        sk = sink_ref[h * G + g] * LOG2E
        m = jnp.maximum(jnp.max(s, axis=-1, keepdims=True), jnp.max(sc, axis=-1, keepdims=True))
        m = jnp.maximum(m, sk)
        p = jnp.exp2(s - m)
        pc = jnp.exp2(sc - m)
        o = _dot(p.astype(BF), vcat) + _dot(pc.astype(BF), vc_ref[...])
        l = o[:, hd:] + jnp.exp2(sk - m)
        o_ref[:, g * hd:(g + 1) * hd] = (o[:, :hd] / l).astype(BF)


def _window_call(sink, q, k, v, n_batch, n_lat, n_ctx, G, tq):
    hd = HEAD_DIM
    hkv = k.shape[1] // hd
    nq = n_lat // tq
    bpt = tq // WINDOW
    nblk = k.shape[0] // WINDOW
    cb = (n_batch * n_lat) // n_ctx

    def prev(b, h, i):
        return (jnp.maximum((b * nq + i) * bpt - 1, 0), h)

    def nxt(b, h, i):
        return (jnp.minimum((b * nq + i + 1) * bpt, nblk - 1), h)

    def main(b, h, i):
        return (b * nq + i, h)

    def ctxb(b, h, i):
        return (cb + b, h)

    return pl.pallas_call(
        functools.partial(_window_kernel, G=G),
        grid=(n_batch, hkv, nq),
        in_specs=[
            pl.BlockSpec(memory_space=pltpu.SMEM),
            pl.BlockSpec((tq, G * hd), main),
            pl.BlockSpec((WINDOW, hd), prev), pl.BlockSpec((tq, hd), main), pl.BlockSpec((WINDOW, hd), nxt),
            pl.BlockSpec((WINDOW, 2 * hd), prev), pl.BlockSpec((tq, 2 * hd), main),
            pl.BlockSpec((WINDOW, 2 * hd), nxt),
            pl.BlockSpec((n_ctx, hd), ctxb), pl.BlockSpec((n_ctx, 2 * hd), ctxb),
        ],
        out_specs=pl.BlockSpec((tq, G * hd), main),
        out_shape=jax.ShapeDtypeStruct((n_batch * n_lat, hkv * G * hd), BF),
        compiler_params=_cparams(("parallel", "parallel", "parallel")),
        name="window_attn",
    )(sink, q, k, k, k, v, v, v, k, v)


NA_TQ = 8 * GRID_W
NA_KB = 4 * GRID_W
NA_NKB = 4


def _na_kernel(q_ref, k0, k1, k2, k3, v0, v1, v2, v3, kc_ref, vc_ref, bias_ref, o_ref):
    hd = HEAD_DIM
    q = q_ref[...]
    kcat = jnp.concatenate([k0[...], k1[...], k2[...], k3[...]], axis=0)
    vcat = jnp.concatenate([v0[...], v1[...], v2[...], v3[...]], axis=0)
    s = _dot_nt(q, kcat) + bias_ref[...]
    sc = _dot_nt(q, kc_ref[...])
    m = jnp.maximum(jnp.max(s, axis=-1, keepdims=True), jnp.max(sc, axis=-1, keepdims=True))
    p = jnp.exp2(s - m)
    pc = jnp.exp2(sc - m)
    o = _dot(p.astype(BF), vcat) + _dot(pc.astype(BF), vc_ref[...])
    o_ref[...] = (o[:, :hd] / o[:, hd:]).astype(BF)


def _na_bias_tables(rpb, n_rows):
    H = rpb.shape[0]
    J = n_rows // 8
    nb = n_rows // 4
    n_dy, n_dx = 2 * NA_WIN_H - 1, 2 * NA_WIN_W - 1
    cq = np.arange(GRID_W)
    kc = np.arange(GRID_W)
    cs = np.clip(cq - NA_WIN_W // 2, 0, GRID_W - NA_WIN_W)
    dx = kc[None, :] - cq[:, None] + NA_WIN_W - 1
    col_ok = (kc[None, :] >= cs[:, None]) & (kc[None, :] < cs[:, None] + NA_WIN_W)
    dx1h = np.zeros((n_dx, GRID_W * GRID_W), np.float32)
    dx1h[np.clip(dx, 0, n_dx - 1).reshape(-1), np.arange(GRID_W * GRID_W)] = 1.0
    blocks = jnp.einsum("hyx,xb->hyb", rpb.astype(F32) * LOG2E, jnp.asarray(dx1h),
                        precision=lax.Precision.HIGHEST)
    blocks = jnp.where(jnp.asarray(col_ok.reshape(-1))[None, None, :], blocks, NEG)
    blocks = jnp.concatenate([blocks, jnp.full((H, 1, GRID_W * GRID_W), NEG, F32)], axis=1)
    blocks = blocks.reshape(H, n_dy + 1, GRID_W, GRID_W)
    idx = np.zeros((3, 8, 4 * NA_NKB), np.int32)
    for v, jv in enumerate((0, min(1, J - 1), J - 1)):
        r = 8 * jv + np.arange(8)
        kb_un = 2 * jv - 1 + np.arange(NA_NKB)
        kb = np.clip(kb_un, 0, nb - 1)
        krow = (4 * kb[:, None] + np.arange(4)[None, :]).reshape(-1)
        krow_dup = np.repeat(kb != kb_un, 4)
        rs = np.clip(r - NA_WIN_H // 2, 0, n_rows - NA_WIN_H)
        row_ok = (krow[None, :] >= rs[:, None]) & (krow[None, :] < rs[:, None] + NA_WIN_H) & ~krow_dup[None, :]
        dy = krow[None, :] - r[:, None] + NA_WIN_H - 1
        idx[v] = np.where(row_ok, np.clip(dy, 0, n_dy - 1), n_dy)
    t = jnp.take(blocks, jnp.asarray(idx.reshape(-1)), axis=1)
    t = t.reshape(H, 3, 8, 4 * NA_NKB, GRID_W, GRID_W).transpose(0, 1, 2, 4, 3, 5)
    return t.reshape(H, 3, NA_TQ, NA_NKB * NA_KB)


def _na_call(q, k, v, bias, n_batch, n_lat, n_ctx):
    hd = HEAD_DIM
    H = q.shape[1] // hd
    J = n_lat // NA_TQ
    nb = n_lat // NA_KB
    cb = (n_batch * n_lat) // n_ctx

    def kblk(t):
        return lambda b, h, j: (b * nb + jnp.clip(2 * j - 1 + t, 0, nb - 1), h)

    def qmap(b, h, j):
        return (b * J + j, h)

    def ctxb(b, h, j):
        return (cb + b, h)

    def bmap(b, h, j):
        return (h, jnp.where(j == 0, 0, jnp.where(j == J - 1, 2, 1)), 0, 0)

    return pl.pallas_call(
        _na_kernel,
        grid=(n_batch, H, J),
        in_specs=[pl.BlockSpec((NA_TQ, hd), qmap)]
        + [pl.BlockSpec((NA_KB, hd), kblk(t)) for t in range(NA_NKB)]
        + [pl.BlockSpec((NA_KB, 2 * hd), kblk(t)) for t in range(NA_NKB)]
        + [pl.BlockSpec((n_ctx, hd), ctxb), pl.BlockSpec((n_ctx, 2 * hd), ctxb),
           pl.BlockSpec((None, None, NA_TQ, NA_NKB * NA_KB), bmap)],
        out_specs=pl.BlockSpec((NA_TQ, hd), qmap),
        out_shape=jax.ShapeDtypeStruct((n_batch * n_lat, H * hd), BF),
        compiler_params=_cparams(("parallel", "parallel", "parallel")),
        name="neighbourhood_attn",
    )(q, k, k, k, k, v, v, v, v, k, v, bias)


def _ctx_attn_kernel(sink_ref, q_ref, k_ref, v_ref, o_ref, *, use_sink):
    s = _dot_nt(q_ref[...], k_ref[...])
    m = jnp.max(s, axis=-1, keepdims=True)
    if use_sink:
        sk = sink_ref[pl.program_id(1)] * LOG2E
        m = jnp.maximum(m, sk)
    p = jnp.exp2(s - m)
    l = jnp.sum(p, axis=-1, keepdims=True)
    if use_sink:
        l = l + jnp.exp2(sk - m)
    o_ref[...] = (_dot(p.astype(BF), v_ref[...]) / l).astype(BF)


def _ctx_attn_call(sink, q, k, v, n_batch, n_lat, n_ctx, G, d, dv, use_sink, name):
    v_stride = 2
    H = q.shape[1] // d
    cb = (n_batch * n_lat) // n_ctx
    return pl.pallas_call(
        functools.partial(_ctx_attn_kernel, use_sink=use_sink),
        grid=(n_batch, H),
        in_specs=[
            pl.BlockSpec(memory_space=pltpu.SMEM),
            pl.BlockSpec((n_ctx, d), lambda b, h: (cb + b, h)),
            pl.BlockSpec((n_ctx, d), lambda b, h: (cb + b, h // G)),
            pl.BlockSpec((n_ctx, dv), lambda b, h: (cb + b, (h // G) * v_stride)),
        ],
        out_specs=pl.BlockSpec((n_ctx, dv), lambda b, h: (b, h)),
        out_shape=jax.ShapeDtypeStruct((n_batch * n_ctx, H * dv), BF),
        compiler_params=_cparams(("parallel", "parallel")),
        name=name,
    )(sink, q, k, v)


def _merge_kernel(x_ref, h_ref, oa_ref, om_ref, ow_ref, on_ref, wg_ref, bg_ref, wb_ref, wo_ref,
                  g1_ref, nf_ref, sh2_ref, sc2_ref, wr_ref, br_ref,
                  xo_ref, h2_ref, ei_ref, wt_ref, y_scr):
    j = pl.program_id(1)

    @pl.when(j == 0)
    def _():
        y_scr[...] = jnp.zeros_like(y_scr)

    h = h_ref[...]
    acc = None
    for i, o_ref in enumerate((oa_ref, om_ref, ow_ref, on_ref)):
        gate = _sigmoid(_dot(h, wg_ref[i]) + bg_ref[i])
        y = gate * _dot(o_ref[...], wb_ref[i])
        acc = y if acc is None else acc + y
    y_scr[...] += _dot(acc.astype(BF), wo_ref[...])

    @pl.when(j == pl.num_programs(1) - 1)
    def _():
        x = x_ref[...] + g1_ref[...] * y_scr[...]
        xo_ref[...] = x
        ms = jnp.mean(x * x, axis=-1, keepdims=True)
        h2 = x * lax.rsqrt(ms + EPS) * nf_ref[...]
        h2 = h2 * (1.0 + sc2_ref[...]) + sh2_ref[...]
        h2_ref[...] = h2
        hi = h2.astype(BF)
        lo = (h2 - hi.astype(F32)).astype(BF)
        logits = _dot(hi, wr_ref[0]) + (_dot(lo, wr_ref[0]) + _dot(hi, wr_ref[1]))
        scores = _sigmoid(logits)
        biased = scores + br_ref[...]
        lane = lax.broadcasted_iota(jnp.int32, logits.shape, 1)
        lane_f = lane.astype(F32)
        ninf = -jnp.inf

        def top2(vals):
            t1 = jnp.max(vals, axis=-1, keepdims=True)
            i1 = jnp.min(jnp.where(vals == t1, lane_f, float(LANES)), axis=-1, keepdims=True)
            vals2 = jnp.where(lane_f == i1, ninf, vals)
            t2 = jnp.max(vals2, axis=-1, keepdims=True)
            i2 = jnp.min(jnp.where(vals2 == t2, lane_f, float(LANES)), axis=-1, keepdims=True)
            return t1, i1, t2, i2

        per = N_EXPERTS // N_GROUPS
        best, gi = None, None
        for g in range(N_GROUPS):
            ing = (lane >= g * per) & (lane < (g + 1) * per)
            t1, _, t2, _ = top2(jnp.where(ing, biased, ninf))
            gs = t1 + t2
            if best is None:
                best, gi = gs, jnp.zeros_like(gs)
            else:
                better = gs > best
                best = jnp.where(better, gs, best)
                gi = jnp.where(better, float(g), gi)
        lane_grp = (lane >> 2).astype(F32)
        vals = jnp.where(lane < N_EXPERTS, jnp.where(lane_grp == gi, biased, NEG), ninf)
        _, i1, _, i2 = top2(vals)
        w1 = jnp.sum(jnp.where(lane_f == i1, scores, 0.0), axis=-1, keepdims=True)
        w2 = jnp.sum(jnp.where(lane_f == i2, scores, 0.0), axis=-1, keepdims=True)
        den = w1 + w2
        ei_ref[...] = jnp.where(lane == 0, i1, jnp.where(lane == 1, i2, 0.0)).astype(jnp.int32)
        wt_ref[...] = jnp.where(lane == 0, w1 / den, jnp.where(lane == 1, w2 / den, 0.0))


def _merge_call(X, h, outs, wg, bg, wb, wo, modl, norm_f, wr_p, br_p, rows, n_batch, n_lat, tm, tn):
    D = X.shape[1]
    bw = wb.shape[1]
    tpb = n_lat // tm

    def rowmap(i, j):
        return (i, 0)

    return pl.pallas_call(
        _merge_kernel,
        grid=(rows // tm, D // tn),
        in_specs=[
            pl.BlockSpec((tm, D), rowmap),
            pl.BlockSpec((tm, D), rowmap),
        ] + [pl.BlockSpec((tm, bw), rowmap)] * 4 + [
            pl.BlockSpec((4, D, tn), lambda i, j: (0, 0, j)),
            pl.BlockSpec((4, 1, tn), lambda i, j: (0, 0, j)),
            pl.BlockSpec((4, bw, tn), lambda i, j: (0, 0, j)),
            pl.BlockSpec((tn, D), lambda i, j: (j, 0)),
            _mod_spec(2, tpb, n_batch, D),
            pl.BlockSpec((1, D), lambda i, j: (0, 0)),
            _mod_spec(3, tpb, n_batch, D),
            _mod_spec(4, tpb, n_batch, D),
            pl.BlockSpec((2, D, LANES), lambda i, j: (0, 0, 0)),
            pl.BlockSpec((1, LANES), lambda i, j: (0, 0)),
        ],
        out_specs=[
            pl.BlockSpec((tm, D), rowmap),
            pl.BlockSpec((tm, D), rowmap),
            pl.BlockSpec((tm, LANES), rowmap),
            pl.BlockSpec((tm, LANES), rowmap),
        ],
        out_shape=[
            jax.ShapeDtypeStruct((rows, D), F32),
            jax.ShapeDtypeStruct((rows, D), F32),
            jax.ShapeDtypeStruct((rows, LANES), jnp.int32),
            jax.ShapeDtypeStruct((rows, LANES), F32),
        ],
        scratch_shapes=[pltpu.VMEM((tm, D), F32)],
        compiler_params=_cparams(("parallel", "arbitrary")),
        name="merge_residual_router",
    )(X, h, *outs, wg, bg, wb, wo, modl, norm_f.reshape(1, D), modl, modl, wr_p, br_p)


def _row_dma_loop(n_rows, make_copy):
    def issue(r, carry):
        for k in range(2):
            make_copy(r, k).start()
        return carry

    def drain(r, carry):
        for k in range(2):
            make_copy(r, k).wait()
        return carry

    lax.fori_loop(0, n_rows, issue, 0, unroll=8)
    lax.fori_loop(0, n_rows, drain, 0, unroll=8)


def _scatter_kernel(fill_ref, pos_ref, h_ref, xs_ref, zbuf, sem, zsem):
    tg = zbuf.shape[0]

    @pl.when(pl.program_id(0) == 0)
    def _():
        zbuf[...] = jnp.zeros_like(zbuf)

        def fill(t, carry):
            @pl.when(fill_ref[t] != 0)
            def _():
                cp = pltpu.make_async_copy(zbuf, xs_ref.at[pl.ds(pl.multiple_of(t * tg, tg), tg), :], zsem)
                cp.start()
                cp.wait()
            return carry

        lax.fori_loop(0, fill_ref.shape[0], fill, 0)

    def make_copy(r, k):
        return pltpu.make_async_copy(h_ref.at[pl.ds(r, 1), :],
                                     xs_ref.at[pl.ds(pos_ref[2 * r + k], 1), :], sem)

    _row_dma_loop(h_ref.shape[0], make_copy)


def _scatter_call(h2, pos, fill, n_slots, tm, tg):
    rows, D = h2.shape
    grid_spec = pltpu.PrefetchScalarGridSpec(
        num_scalar_prefetch=1,
        grid=(rows // tm,),
        in_specs=[
            pl.BlockSpec((2 * tm,), lambda i, f: (i,), memory_space=pltpu.SMEM),
            pl.BlockSpec((tm, D), lambda i, f: (i, 0)),
        ],
        out_specs=pl.BlockSpec(memory_space=pl.ANY),
        scratch_shapes=[pltpu.VMEM((tg, D), F32), pltpu.SemaphoreType.DMA(()), pltpu.SemaphoreType.DMA(())],
    )
    return pl.pallas_call(
        _scatter_kernel,
        grid_spec=grid_spec,
        out_shape=jax.ShapeDtypeStruct((n_slots, D), F32),
        compiler_params=_cparams(("arbitrary",)),
        name="moe_scatter_rows",
    )(fill, pos, h2)


def _experts_kernel(te_ref, nu_ref, x_ref, w1_ref, w3_ref, w2_ref, y_ref, w1_scr, w3_scr, w2_scr):
    i = pl.program_id(0)

    @pl.when((i == 0) | (te_ref[i] != te_ref[jnp.maximum(i - 1, 0)]))
    def _():
        w1_scr[...] = w1_ref[...].astype(BF)
        w3_scr[...] = w3_ref[...].astype(BF)
        w2_scr[...] = w2_ref[...].astype(BF)

    @pl.when(i < nu_ref[0])
    def _():
        x = x_ref[...].astype(BF)
        a = _dot(x, w1_scr[...])
        b = _dot(x, w3_scr[...])
        hid = (a * _sigmoid(a)) * b
        y_ref[...] = _dot(hid.astype(BF), w2_scr[...])

    @pl.when(i >= nu_ref[0])
    def _():
        y_ref[...] = jnp.zeros_like(y_ref)


def _experts_call(xs, tile_expert, n_used, w1, w3, w2, layer, tg):
    S, D = xs.shape
    de = w1.shape[3]
    grid_spec = pltpu.PrefetchScalarGridSpec(
        num_scalar_prefetch=2,
        grid=(S // tg,),
        in_specs=[
            pl.BlockSpec((tg, D), lambda i, te, nu: (i, 0)),
            pl.BlockSpec((None, None, D, de), lambda i, te, nu: (layer, te[i], 0, 0)),
            pl.BlockSpec((None, None, D, de), lambda i, te, nu: (layer, te[i], 0, 0)),
            pl.BlockSpec((None, None, de, D), lambda i, te, nu: (layer, te[i], 0, 0)),
        ],
        out_specs=pl.BlockSpec((tg, D), lambda i, te, nu: (i, 0)),
        scratch_shapes=[pltpu.VMEM((D, de), BF), pltpu.VMEM((D, de), BF), pltpu.VMEM((de, D), BF)],
    )
    return pl.pallas_call(
        _experts_kernel,
        grid_spec=grid_spec,
        out_shape=jax.ShapeDtypeStruct((S, D), F32),
        compiler_params=_cparams(("arbitrary",)),
        name="moe_experts",
    )(tile_expert, n_used, xs, w1, w3, w2)


def _combine_kernel(pos_ref, x_ref, wt_ref, g2_ref, ys_ref, o_ref, buf, sem):
    def make_copy(r, k):
        return pltpu.make_async_copy(ys_ref.at[pl.ds(pos_ref[2 * r + k], 1), :],
                                     buf.at[k, pl.ds(r, 1), :], sem)

    _row_dma_loop(x_ref.shape[0], make_copy)
    wt = wt_ref[...]
    moe = wt[:, 0:1] * buf[0] + wt[:, 1:2] * buf[1]
    o_ref[...] = x_ref[...] + g2_ref[...] * moe


def _combine_call(X, ys, pos, wts, modl, n_batch, n_lat, tm):
    rows, D = X.shape
    tpb = n_lat // tm
    return pl.pallas_call(
        _combine_kernel,
        grid=(rows // tm,),
        in_specs=[
            pl.BlockSpec((2 * tm,), lambda i: (i,), memory_space=pltpu.SMEM),
            pl.BlockSpec((tm, D), lambda i: (i, 0)),
            pl.BlockSpec((tm, LANES), lambda i: (i, 0)),
            _mod_spec(5, tpb, n_batch, D),
            pl.BlockSpec(memory_space=pl.ANY),
        ],
        out_specs=pl.BlockSpec((tm, D), lambda i: (i, 0)),
        out_shape=jax.ShapeDtypeStruct((rows, D), F32),
        scratch_shapes=[pltpu.VMEM((2, tm, D), F32), pltpu.SemaphoreType.DMA(())],
        compiler_params=_cparams(("arbitrary",)),
        name="moe_combine_residual",
    )(pos, X, wts, modl, ys)


def _route_positions(eidx, tg):
    rows = eidx.shape[0]
    e_flat = eidx.reshape(-1)
    onehot = (e_flat[:, None] == jnp.arange(N_EXPERTS, dtype=jnp.int32)[None, :]).astype(jnp.int32)
    csum = jnp.cumsum(onehot, axis=0)
    counts = csum[-1]
    rank = jnp.sum((csum - onehot) * onehot, axis=1)
    padded = ((counts + tg - 1) // tg) * tg
    ends = jnp.cumsum(padded)
    offsets = ends - padded
    pos = jnp.sum(onehot * offsets[None, :], axis=1) + rank
    n_tiles = (2 * rows + N_EXPERTS * tg) // tg
    tile_start = jnp.arange(n_tiles, dtype=jnp.int32) * tg
    tile_expert = jnp.sum((tile_start[:, None] >= ends[None, :]).astype(jnp.int32), axis=1)
    n_used = (ends[-1] // tg).astype(jnp.int32)
    used = jnp.arange(n_tiles) < n_used
    te_1h = (tile_expert[:, None] == jnp.arange(N_EXPERTS, dtype=jnp.int32)[None, :]).astype(jnp.int32)
    real_end = jnp.sum(te_1h * (offsets + counts)[None, :], axis=1)
    fill = jnp.logical_not(used & (tile_start + tg <= real_end)).astype(jnp.int32)
    last_e = jnp.sum((((n_used - 1) * tg) >= ends).astype(jnp.int32))
    tile_expert = jnp.where(used, tile_expert, last_e).astype(jnp.int32)
    return pos.astype(jnp.int32), tile_expert, n_used.reshape(1), fill


def _rope_tables(n_lat, tm):
    t = jnp.arange(n_lat)
    rows, cols = t // GRID_W, t % GRID_W

    def tab(d2, pos):
        freqs = ROPE_THETA ** (-jnp.arange(d2, dtype=F32) / d2)
        ang = pos.astype(F32)[:, None] * freqs[None, :]
        return jnp.cos(ang), jnp.sin(ang)

    cr, sr = tab(32, rows)
    cc, sc = tab(32, cols)
    cos_a = jnp.concatenate([cr, cr, cc, cc], axis=-1)
    sin_a = jnp.concatenate([-sr, sr, -sc, sc], axis=-1)
    cr, sr = tab(16, rows)
    cc, sc = tab(16, cols)
    one, zero = jnp.ones((n_lat, 64), F32), jnp.zeros((n_lat, 64), F32)
    cos_m = jnp.concatenate([cr, cr, cc, cc, one], axis=-1)
    sin_m = jnp.concatenate([-sr, sr, -sc, sc, zero], axis=-1)
    ident_c, ident_s = jnp.ones((tm, LANES), F32), jnp.zeros((tm, LANES), F32)
    return tuple(jnp.concatenate([a, b], axis=0) for a, b in
                 ((cos_a, ident_c), (sin_a, ident_s), (cos_m, ident_c), (sin_m, ident_s)))


def _pad_mla_heads(w, n_heads):
    lead = w.shape[:-1]
    w = w.reshape(lead + (n_heads, MLA_QK))
    w = jnp.pad(w, [(0, 0)] * len(lead) + [(0, 0), (0, MLA_PAD - MLA_QK)])
    return w.reshape(lead + (n_heads * MLA_PAD,))


def kernel(x, c, ctx, c_ctx, w_mod, b_mod, norm_mix, norm_ffn, w_in, mla_qa_norm, mla_w_uq, mla_kva_norm,
           mla_w_ukv, qn_att, kn_att, qn_mla, kn_mla, qn_win, kn_win, qn_na, kn_na, win_sink, na_rpb,
           w_branch, w_gate, b_gate, w_out, w_router, b_router, moe_w1, moe_w3, moe_w2):
    B, N, D = x.shape
    n_ctx = ctx.shape[1]
    L = w_mod.shape[0]
    R = B * N + B * n_ctx
    tm = 512
    tm_moe = 256
    tg = 512
    tn_merge = min(256, D // 2)
    tq = min(512, N)
    tk = min(512, N // 2)

    X = jnp.concatenate([x.reshape(B * N, D), ctx.reshape(B * n_ctx, D)], axis=0)
    cvec = jnp.zeros((8, D), F32).at[:B].set(c).at[B].set(c_ctx)
    mod = _mod_call(cvec, w_mod, b_mod, min(1024, D)).reshape(L, 8, 6, 1, D)
    tabs = _rope_tables(N, tm)

    wr_hi = w_router.astype(BF)
    wr_lo = (w_router - wr_hi.astype(F32)).astype(BF)
    wr_p = jnp.pad(jnp.stack([wr_hi, wr_lo]), ((0, 0), (0, 0), (0, LANES - N_EXPERTS)))
    br_p = jnp.pad(b_router.astype(F32), (0, LANES - N_EXPERTS)).reshape(1, LANES)

    for l in range(L):
        last = l == L - 1
        rows = B * N if last else R
        modl = mod[l]
        w_in_l = w_in[l]
        w_in_p = jnp.concatenate(
            [w_in_l[:, :KR_END], jnp.zeros((D, 64), F32), w_in_l[:, KR_END:]], axis=1).astype(BF)
        wuq_p = _pad_mla_heads(mla_w_uq[l], 4).astype(BF)
        wukv = mla_w_ukv[l].astype(BF)
        gains = [g.reshape(1, -1) for g in (
            qn_att[l], kn_att[l], qn_win[l], kn_win[l], qn_na[l], kn_na[l],
            _pad_mla_heads(qn_mla[l], 1), _pad_mla_heads(kn_mla[l], 1), mla_qa_norm[l], mla_kva_norm[l])]

        h, proj = _proj_call(X, norm_mix[l], modl, w_in_p, B, N, tm, P_TOTAL // 2)
        (qa, ka, va, qm, km, vm, qw, kw, vw, qn, kn, vn) = _heads_call(proj, tabs, gains, wuq_p, wukv, B, N, tm)

        sink = win_sink[l].astype(F32)
        o_att = _flash_call(qa, ka, va, B, N, n_ctx, 2, HEAD_DIM, tq, tk, "dense_gqa")
        o_mla = _flash_call(qm, km, vm, B, N, n_ctx, 1, MLA_PAD, 2 * tq, tk, "latent_attn")
        o_win = _window_call(sink, qw, kw, vw, B, N, n_ctx, 2, tq)
        bias = _na_bias_tables(na_rpb[l], N // GRID_W)
        o_na = _na_call(qn, kn, vn, bias, B, N, n_ctx)
        outs = [o_att, o_mla, o_win, o_na]
        if not last:
            outs_c = [
                _ctx_attn_call(sink, qa, ka, va, B, N, n_ctx, 2, HEAD_DIM, HEAD_DIM, False, "ctx_dense_gqa"),
                _ctx_attn_call(sink, qm, km, vm, B, N, n_ctx, 1, MLA_PAD, HEAD_DIM, False, "ctx_latent_attn"),
                _ctx_attn_call(sink, qw, kw, vw, B, N, n_ctx, 2, HEAD_DIM, HEAD_DIM, True, "ctx_window_attn"),
                _ctx_attn_call(sink, qn, kn, vn, B, N, n_ctx, 1, HEAD_DIM, HEAD_DIM, False, "ctx_neighbourhood"),
            ]
            outs = [jnp.concatenate([a, b], axis=0) for a, b in zip(outs, outs_c)]

        X, h2, eidx, wts = _merge_call(
            X, h, outs, w_gate[l].astype(BF), b_gate[l].reshape(4, 1, D), w_branch[l].astype(BF),
            w_out[l].astype(BF), modl, norm_ffn[l], wr_p, br_p, rows, B, N, tm, tn_merge)

        pos, tile_expert, n_used, fill = _route_positions(eidx[:, :2], tg)
        n_slots = 2 * rows + N_EXPERTS * tg
        xs = _scatter_call(h2, pos, fill, n_slots, tm_moe, tg)
        ys = _experts_call(xs, tile_expert, n_used, moe_w1, moe_w3, moe_w2, l, tg)
        X = _combine_call(X, ys, pos, wts, modl, B, N, tm_moe)

    return X.reshape(B, N, D)
```

```python
import functools

import numpy as np
import jax
import jax.numpy as jnp
from jax import lax
from jax.experimental import pallas as pl
from jax.experimental.pallas import tpu as pltpu

BF = jnp.bfloat16
F32 = jnp.float32

GRID_W = 64
HEAD_DIM = 128
ROPE_THETA = 10000.0
EPS = 1e-6
NEG = -1e30
LOG2E = 1.4426950408889634
WINDOW = 128
NA_WIN_H = 8
NA_WIN_W = 16
N_EXPERTS = 16
N_GROUPS = 4
MLA_QK = 192
MLA_PAD = 256
LANES = 128
VMEM_LIMIT = 56 * 1024 * 1024

P_AQ, P_AK, P_AV = 0, 512, 768
P_MCQ, P_MCKV, P_MKR = 1024, 1408, 1664
P_WQ, P_WK, P_WV = 1792, 2304, 2560
P_NQ, P_NK, P_NV = 2816, 3328, 3840
P_TOTAL = 4352
KR_END = 1728


def _cparams(sem):
    return pltpu.CompilerParams(dimension_semantics=sem, vmem_limit_bytes=VMEM_LIMIT)


def _sigmoid(z):
    return 1.0 / (1.0 + jnp.exp(-z))


def _dot(a, b):
    return jnp.dot(a, b, preferred_element_type=F32)


def _dot_nt(a, b):
    return lax.dot_general(a, b, (((1,), (1,)), ((), ())), preferred_element_type=F32)


def _mod_kernel(c_ref, w_ref, b_ref, o_ref):
    c = c_ref[...]
    s = c * _sigmoid(c)
    o_ref[...] = _dot(s.astype(BF), w_ref[...].astype(BF)) + b_ref[...]


def _mod_call(cvec, w_mod, b_mod, tn):
    L, D, D6 = w_mod.shape
    return pl.pallas_call(
        _mod_kernel,
        grid=(L, D6 // tn),
        in_specs=[
            pl.BlockSpec((8, D), lambda l, j: (0, 0)),
            pl.BlockSpec((None, D, tn), lambda l, j: (l, 0, j)),
            pl.BlockSpec((None, 1, tn), lambda l, j: (l, 0, j)),
        ],
        out_specs=pl.BlockSpec((None, 8, tn), lambda l, j: (l, 0, j)),
        out_shape=jax.ShapeDtypeStruct((L, 8, D6), F32),
        compiler_params=_cparams(("parallel", "parallel")),
        name="adaln_mod",
    )(cvec, w_mod, b_mod.reshape(L, 1, D6))


def _mod_spec(comp, tiles_per_batch, n_batch, D):
    return pl.BlockSpec(
        (None, None, 1, D),
        lambda i, *_: (jnp.minimum(i // tiles_per_batch, n_batch), comp, 0, 0))


def _proj_kernel(x_ref, g_ref, sh_ref, sc_ref, w_ref, h_ref, o_ref, h_scr):
    @pl.when(pl.program_id(1) == 0)
    def _():
        x = x_ref[...]
        ms = jnp.mean(x * x, axis=-1, keepdims=True)
        xn = x * lax.rsqrt(ms + EPS) * g_ref[...]
        h = (xn * (1.0 + sc_ref[...]) + sh_ref[...]).astype(BF)
        h_scr[...] = h
        h_ref[...] = h

    o_ref[...] = _dot(h_scr[...], w_ref[...])


def _proj_call(X, norm_g, modl, w_in_p, n_batch, n_lat, tm, tn):
    R, D = X.shape
    P = w_in_p.shape[1]
    tpb = n_lat // tm
    return pl.pallas_call(
        _proj_kernel,
        grid=(R // tm, P // tn),
        in_specs=[
            pl.BlockSpec((tm, D), lambda i, j: (i, 0)),
            pl.BlockSpec((1, D), lambda i, j: (0, 0)),
            _mod_spec(0, tpb, n_batch, D),
            _mod_spec(1, tpb, n_batch, D),
            pl.BlockSpec((D, tn), lambda i, j: (0, j)),
        ],
        out_specs=[
            pl.BlockSpec((tm, D), lambda i, j: (i, 0)),
            pl.BlockSpec((tm, tn), lambda i, j: (i, j)),
        ],
        out_shape=[jax.ShapeDtypeStruct((R, D), BF), jax.ShapeDtypeStruct((R, P), F32)],
        scratch_shapes=[pltpu.VMEM((tm, D), BF)],
        compiler_params=_cparams(("parallel", "arbitrary")),
        name="norm_mod_proj",
    )(X, norm_g.reshape(1, D), modl, modl, w_in_p)


def _rms(x, g, n):
    ss = jnp.sum(x * x, axis=-1, keepdims=True)
    return x * lax.rsqrt(ss / n + EPS) * g


def _rope(x, cos, sin_signed, half):
    lane = lax.broadcasted_iota(jnp.int32, x.shape, 1)
    first = (lane & (2 * half - 1)) < half
    xr = jnp.where(first, pltpu.roll(x, LANES - half, 1), pltpu.roll(x, half, 1))
    return x * cos + xr * sin_signed


def _heads_kernel(p_ref, ca_ref, sa_ref, cm_ref, sm_ref,
                  qa_g, ka_g, qw_g, kw_g, qn_g, kn_g, qm_g, km_g, cqn_g, ckvn_g, wuq_ref, wukv_ref,
                  qa_ref, ka_ref, va_ref, qm_ref, km_ref, vm_ref,
                  qw_ref, kw_ref, vw_ref, qn_ref, kn_ref, vn_ref):
    hd = HEAD_DIM
    sc128 = HEAD_DIM ** -0.5 * LOG2E
    sc192 = MLA_QK ** -0.5 * LOG2E
    ca, sa, cm, sm = ca_ref[...], sa_ref[...], cm_ref[...], sm_ref[...]

    def sl(off, h, w=hd):
        return p_ref[:, off + h * w: off + (h + 1) * w]

    ones = jnp.ones((p_ref.shape[0], hd), BF)

    def put_values(v_out, v_off, n_heads):
        for h in range(n_heads):
            v_out[:, 2 * h * hd:(2 * h + 1) * hd] = sl(v_off, h).astype(BF)
            v_out[:, (2 * h + 1) * hd:(2 * h + 2) * hd] = ones

    for (q_off, k_off, v_off, qg, kg, q_out, k_out, v_out) in (
            (P_AQ, P_AK, P_AV, qa_g, ka_g, qa_ref, ka_ref, va_ref),
            (P_WQ, P_WK, P_WV, qw_g, kw_g, qw_ref, kw_ref, vw_ref)):
        for h in range(4):
            q = _rope(_rms(sl(q_off, h), qg[...], hd), ca, sa, 32)
            q_out[:, h * hd:(h + 1) * hd] = (q * sc128).astype(BF)
        for h in range(2):
            k = _rope(_rms(sl(k_off, h), kg[...], hd), ca, sa, 32)
            k_out[:, h * hd:(h + 1) * hd] = k.astype(BF)
        put_values(v_out, v_off, 2)

    for h in range(4):
        qn_ref[:, h * hd:(h + 1) * hd] = (_rms(sl(P_NQ, h), qn_g[...], hd) * sc128).astype(BF)
        kn_ref[:, h * hd:(h + 1) * hd] = _rms(sl(P_NK, h), kn_g[...], hd).astype(BF)
    put_values(vn_ref, P_NV, 4)

    cq = _rms(p_ref[:, P_MCQ:P_MCQ + 384], cqn_g[...], 384)
    qf = _dot(cq.astype(BF), wuq_ref[...])
    ckv = _rms(p_ref[:, P_MCKV:P_MCKV + 256], ckvn_g[...], 256)
    kvf = _dot(ckv.astype(BF), wukv_ref[...])
    kr = p_ref[:, P_MKR:P_MKR + hd]
    kr_ss = jnp.sum(kr * kr, axis=-1, keepdims=True)
    qg0, qg1 = qm_g[:, :hd], qm_g[:, hd:]
    kg0, kg1 = km_g[:, :hd], km_g[:, hd:]
    for h in range(4):
        q0 = qf[:, h * MLA_PAD: h * MLA_PAD + hd]
        q1 = qf[:, h * MLA_PAD + hd: (h + 1) * MLA_PAD]
        ss = jnp.sum(q0 * q0, axis=-1, keepdims=True) + jnp.sum(q1 * q1, axis=-1, keepdims=True)
        r = lax.rsqrt(ss / MLA_QK + EPS)
        qm_ref[:, h * MLA_PAD: h * MLA_PAD + hd] = (q0 * r * qg0 * sc192).astype(BF)
        qm_ref[:, h * MLA_PAD + hd: (h + 1) * MLA_PAD] = (
            _rope(q1 * r * qg1, cm, sm, 16) * sc192).astype(BF)
        k0 = kvf[:, h * 2 * hd: h * 2 * hd + hd]
        ss = jnp.sum(k0 * k0, axis=-1, keepdims=True) + kr_ss
        r = lax.rsqrt(ss / MLA_QK + EPS)
        km_ref[:, h * MLA_PAD: h * MLA_PAD + hd] = (k0 * r * kg0).astype(BF)
        km_ref[:, h * MLA_PAD + hd: (h + 1) * MLA_PAD] = _rope(kr * r * kg1, cm, sm, 16).astype(BF)
        vm_ref[:, 2 * h * hd:(2 * h + 1) * hd] = kvf[:, h * 2 * hd + hd: (h + 1) * 2 * hd].astype(BF)
        vm_ref[:, (2 * h + 1) * hd:(2 * h + 2) * hd] = ones


def _heads_call(proj, tabs, gains, wuq_p, wukv, n_batch, n_lat, tm):
    R, P = proj.shape
    tpb = n_lat // tm
    tab_spec = pl.BlockSpec((tm, LANES), lambda i: (jnp.where(i < n_batch * tpb, i % tpb, tpb), 0))

    def full(a):
        return pl.BlockSpec(a.shape, lambda i: (0,) * a.ndim)

    widths = (512, 256, 512, 4 * MLA_PAD, 4 * MLA_PAD, 1024, 512, 256, 512, 512, 512, 1024)
    return pl.pallas_call(
        _heads_kernel,
        grid=(R // tm,),
        in_specs=[pl.BlockSpec((tm, P), lambda i: (i, 0))] + [tab_spec] * 4
        + [full(g) for g in gains] + [full(wuq_p), full(wukv)],
        out_specs=[pl.BlockSpec((tm, w), lambda i: (i, 0)) for w in widths],
        out_shape=[jax.ShapeDtypeStruct((R, w), BF) for w in widths],
        compiler_params=_cparams(("parallel",)),
        name="head_prep",
    )(proj, *tabs, *gains, wuq_p, wukv)


def _flash_kernel(q_ref, k_ref, v_ref, kc_ref, vc_ref, o_ref, q_scr, s_scr, m_scr, acc_scr,
                  *, G, d, dv, w, unroll):
    tq = q_ref.shape[0]
    n_blk = k_ref.shape[0] // w
    for g in range(G):
        q_scr[g * tq:(g + 1) * tq, :] = q_ref[:, g * d:(g + 1) * d]
    m_scr[...] = jnp.full_like(m_scr, NEG)
    acc_scr[...] = jnp.zeros_like(acc_scr)

    def scores(k):
        return _dot_nt(q_scr[...], k)

    def absorb(s, v):
        m_prev = m_scr[...]
        m_next = jnp.maximum(m_prev, s.max(axis=1, keepdims=True))
        alpha = jnp.exp2(m_prev - m_next)
        p = jnp.concatenate(
            [jnp.exp2(s[:, c * LANES:(c + 1) * LANES] - m_next) for c in range(s.shape[1] // LANES)],
            axis=1)
        pv = _dot(p.astype(BF), v)
        acc_scr[...] = jnp.concatenate([alpha, alpha], axis=1) * acc_scr[...] + pv
        m_scr[...] = m_next

    absorb(scores(kc_ref[...]), vc_ref[...])
    s_scr[0] = scores(k_ref[0:w, :])

    def group(jj, carry):
        base = jj * (unroll * w)
        for u in range(unroll):
            cur = pl.multiple_of(base + u * w, w)
            nxt = pl.multiple_of(jnp.minimum(base + (u + 1) * w, (n_blk - 1) * w), w)
            s_scr[(u + 1) % 2] = scores(k_ref[pl.ds(nxt, w), :])
            absorb(s_scr[u % 2], v_ref[pl.ds(cur, w), :])
        return carry

    lax.fori_loop(0, n_blk // unroll, group, 0)

    acc = acc_scr[...]
    o = acc[:, :dv] / acc[:, dv:]
    for g in range(G):
        o_ref[:, g * dv:(g + 1) * dv] = o[g * tq:(g + 1) * tq].astype(BF)


def _flash_call(q, k, v, n_batch, n_lat, n_ctx, G, d, tq, w, name):
    dv = LANES
    hkv = k.shape[1] // d
    nq = n_lat // tq
    cb = (n_batch * n_lat) // n_ctx
    unroll = 4 if (n_lat // w) % 4 == 0 else 2
    assert (n_lat // w) % unroll == 0
    return pl.pallas_call(
        functools.partial(_flash_kernel, G=G, d=d, dv=dv, w=w, unroll=unroll),
        grid=(n_batch, hkv, nq),
        in_specs=[
            pl.BlockSpec((tq, G * d), lambda b, h, i: (b * nq + i, h)),
            pl.BlockSpec((n_lat, d), lambda b, h, i: (b, h)),
            pl.BlockSpec((n_lat, 2 * dv), lambda b, h, i: (b, h)),
            pl.BlockSpec((n_ctx, d), lambda b, h, i: (cb + b, h)),
            pl.BlockSpec((n_ctx, 2 * dv), lambda b, h, i: (cb + b, h)),
        ],
        out_specs=pl.BlockSpec((tq, G * dv), lambda b, h, i: (b * nq + i, h)),
        out_shape=jax.ShapeDtypeStruct((n_batch * n_lat, hkv * G * dv), BF),
        scratch_shapes=[pltpu.VMEM((G * tq, d), BF), pltpu.VMEM((2, G * tq, w), F32),
                        pltpu.VMEM((G * tq, LANES), F32), pltpu.VMEM((G * tq, 2 * dv), F32)],
        compiler_params=_cparams(("parallel", "parallel", "parallel")),
        name=name,
    )(q, k, v, k, v)


def _window_kernel(sink_ref, q_ref, kp_ref, km_ref, kn_ref, vp_ref, vm_ref, vn_ref, kc_ref, vc_ref,
                   o_ref, s_scr, sc_scr, *, G):
    hd = HEAD_DIM
    i = pl.program_id(1)
    ni = pl.num_programs(1)
    tq = q_ref.shape[0]
    n_heads = q_ref.shape[1] // hd
    r = lax.broadcasted_iota(jnp.int32, (tq, tq + 2 * WINDOW), 0)
    c = lax.broadcasted_iota(jnp.int32, (tq, tq + 2 * WINDOW), 1)
    rel = c - WINDOW - r
    lo = jnp.where(i > 0, 0, WINDOW)
    hi = jnp.where(i < ni - 1, tq + 2 * WINDOW, tq + WINDOW)
    valid = (jnp.abs(rel) <= WINDOW) & (c >= lo) & (c < hi)

    def kv_cat(p_ref, m_ref, n_ref, kvh, w):
        cols = slice(kvh * w, (kvh + 1) * w)
        return jnp.concatenate([p_ref[:, cols], m_ref[:, cols], n_ref[:, cols]], axis=0)

    for qh in range(n_heads):
        kvh = qh // G
        q = q_ref[:, qh * hd:(qh + 1) * hd]
        s_scr[qh] = jnp.where(valid, _dot_nt(q, kv_cat(kp_ref, km_ref, kn_ref, kvh, hd)), NEG)
        sc_scr[qh] = _dot_nt(q, kc_ref[:, kvh * hd:(kvh + 1) * hd])
    for qh in range(n_heads):
        kvh = qh // G
        s = s_scr[qh]
        sc = sc_scr[qh]
        sk = sink_ref[qh] * LOG2E
        m = jnp.maximum(jnp.max(s, axis=-1, keepdims=True), jnp.max(sc, axis=-1, keepdims=True))
        m = jnp.maximum(m, sk)
        p = jnp.exp2(s - m)
        pc = jnp.exp2(sc - m)
        o = (_dot(p.astype(BF), kv_cat(vp_ref, vm_ref, vn_ref, kvh, 2 * hd))
             + _dot(pc.astype(BF), vc_ref[:, kvh * 2 * hd:(kvh + 1) * 2 * hd]))
        l = o[:, hd:] + jnp.exp2(sk - m)
        o_ref[:, qh * hd:(qh + 1) * hd] = (o[:, :hd] / l).astype(BF)


def _window_call(sink, q, k, v, n_batch, n_lat, n_ctx, G, tq):
    hd = HEAD_DIM
    hkv = k.shape[1] // hd
    nq = n_lat // tq
    bpt = tq // WINDOW
    nblk = k.shape[0] // WINDOW
    cb = (n_batch * n_lat) // n_ctx

    def prev(b, i):
        return (jnp.maximum((b * nq + i) * bpt - 1, 0), 0)

    def nxt(b, i):
        return (jnp.minimum((b * nq + i + 1) * bpt, nblk - 1), 0)

    def main(b, i):
        return (b * nq + i, 0)

    def ctxb(b, i):
        return (cb + b, 0)

    kw, vw, qw = hkv * hd, hkv * 2 * hd, hkv * G * hd
    return pl.pallas_call(
        functools.partial(_window_kernel, G=G),
        grid=(n_batch, nq),
        in_specs=[
            pl.BlockSpec(memory_space=pltpu.SMEM),
            pl.BlockSpec((tq, qw), main),
            pl.BlockSpec((WINDOW, kw), prev), pl.BlockSpec((tq, kw), main), pl.BlockSpec((WINDOW, kw), nxt),
            pl.BlockSpec((WINDOW, vw), prev), pl.BlockSpec((tq, vw), main), pl.BlockSpec((WINDOW, vw), nxt),
            pl.BlockSpec((n_ctx, kw), ctxb), pl.BlockSpec((n_ctx, vw), ctxb),
        ],
        out_specs=pl.BlockSpec((tq, qw), main),
        out_shape=jax.ShapeDtypeStruct((n_batch * n_lat, qw), BF),
        scratch_shapes=[pltpu.VMEM((hkv * G, tq, tq + 2 * WINDOW), F32), pltpu.VMEM((hkv * G, tq, n_ctx), F32)],
        compiler_params=_cparams(("parallel", "parallel")),
        name="window_attn",
    )(sink, q, k, k, k, v, v, v, k, v)


NA_TQ = 8 * GRID_W
NA_KB = 4 * GRID_W
NA_NKB = 4


def _na_kernel(q_ref, k0, k1, k2, k3, v0, v1, v2, v3, kc_ref, vc_ref, bias_ref, o_ref, s_scr, sc_scr):
    hd = HEAD_DIM
    n_heads = q_ref.shape[1] // hd

    def cat(refs, h, w):
        return jnp.concatenate([r[:, h * w:(h + 1) * w] for r in refs], axis=0)

    for h in range(n_heads):
        q = q_ref[:, h * hd:(h + 1) * hd]
        s_scr[h] = _dot_nt(q, cat((k0, k1, k2, k3), h, hd)) + bias_ref[h]
        sc_scr[h] = _dot_nt(q, kc_ref[:, h * hd:(h + 1) * hd])
    for h in range(n_heads):
        s = s_scr[h]
        sc = sc_scr[h]
        m = jnp.maximum(jnp.max(s, axis=-1, keepdims=True), jnp.max(sc, axis=-1, keepdims=True))
        p = jnp.exp2(s - m)
        pc = jnp.exp2(sc - m)
        o = (_dot(p.astype(BF), cat((v0, v1, v2, v3), h, 2 * hd))
             + _dot(pc.astype(BF), vc_ref[:, h * 2 * hd:(h + 1) * 2 * hd]))
        o_ref[:, h * hd:(h + 1) * hd] = (o[:, :hd] / o[:, hd:]).astype(BF)


def _na_bias_tables(rpb, n_rows):
    H = rpb.shape[0]
    J = n_rows // 8
    nb = n_rows // 4
    n_dy, n_dx = 2 * NA_WIN_H - 1, 2 * NA_WIN_W - 1
    cq = np.arange(GRID_W)
    kc = np.arange(GRID_W)
    cs = np.clip(cq - NA_WIN_W // 2, 0, GRID_W - NA_WIN_W)
    dx = kc[None, :] - cq[:, None] + NA_WIN_W - 1
    col_ok = (kc[None, :] >= cs[:, None]) & (kc[None, :] < cs[:, None] + NA_WIN_W)
    dx1h = np.zeros((n_dx, GRID_W * GRID_W), np.float32)
    dx1h[np.clip(dx, 0, n_dx - 1).reshape(-1), np.arange(GRID_W * GRID_W)] = 1.0
    blocks = jnp.einsum("hyx,xb->hyb", rpb.astype(F32) * LOG2E, jnp.asarray(dx1h),
                        precision=lax.Precision.HIGHEST)
    blocks = jnp.where(jnp.asarray(col_ok.reshape(-1))[None, None, :], blocks, NEG)
    blocks = jnp.concatenate([blocks, jnp.full((H, 1, GRID_W * GRID_W), NEG, F32)], axis=1)
    blocks = blocks.reshape(H, n_dy + 1, GRID_W, GRID_W)
    idx = np.zeros((3, 8, 4 * NA_NKB), np.int32)
    for v, jv in enumerate((0, min(1, J - 1), J - 1)):
        r = 8 * jv + np.arange(8)
        kb_un = 2 * jv - 1 + np.arange(NA_NKB)
        kb = np.clip(kb_un, 0, nb - 1)
        krow = (4 * kb[:, None] + np.arange(4)[None, :]).reshape(-1)
        krow_dup = np.repeat(kb != kb_un, 4)
        rs = np.clip(r - NA_WIN_H // 2, 0, n_rows - NA_WIN_H)
        row_ok = (krow[None, :] >= rs[:, None]) & (krow[None, :] < rs[:, None] + NA_WIN_H) & ~krow_dup[None, :]
        dy = krow[None, :] - r[:, None] + NA_WIN_H - 1
        idx[v] = np.where(row_ok, np.clip(dy, 0, n_dy - 1), n_dy)
    t = jnp.take(blocks, jnp.asarray(idx.reshape(-1)), axis=1)
    t = t.reshape(H, 3, 8, 4 * NA_NKB, GRID_W, GRID_W).transpose(0, 1, 2, 4, 3, 5)
    return t.reshape(H, 3, NA_TQ, NA_NKB * NA_KB)


def _na_call(q, k, v, bias, n_batch, n_lat, n_ctx):
    hd = HEAD_DIM
    H = q.shape[1] // hd
    J = n_lat // NA_TQ
    nb = n_lat // NA_KB
    cb = (n_batch * n_lat) // n_ctx

    def kblk(t):
        return lambda b, j: (b * nb + jnp.clip(2 * j - 1 + t, 0, nb - 1), 0)

    def qmap(b, j):
        return (b * J + j, 0)

    def ctxb(b, j):
        return (cb + b, 0)

    def bmap(b, j):
        return (0, jnp.where(j == 0, 0, jnp.where(j == J - 1, 2, 1)), 0, 0)

    n_keys = NA_NKB * NA_KB
    return pl.pallas_call(
        _na_kernel,
        grid=(n_batch, J),
        in_specs=[pl.BlockSpec((NA_TQ, H * hd), qmap)]
        + [pl.BlockSpec((NA_KB, H * hd), kblk(t)) for t in range(NA_NKB)]
        + [pl.BlockSpec((NA_KB, H * 2 * hd), kblk(t)) for t in range(NA_NKB)]
        + [pl.BlockSpec((n_ctx, H * hd), ctxb), pl.BlockSpec((n_ctx, H * 2 * hd), ctxb),
           pl.BlockSpec((H, None, NA_TQ, n_keys), bmap)],
        out_specs=pl.BlockSpec((NA_TQ, H * hd), qmap),
        out_shape=jax.ShapeDtypeStruct((n_batch * n_lat, H * hd), BF),
        scratch_shapes=[pltpu.VMEM((H, NA_TQ, n_keys), F32), pltpu.VMEM((H, NA_TQ, n_ctx), F32)],
        compiler_params=_cparams(("parallel", "parallel")),
        name="neighbourhood_attn",
    )(q, k, k, k, k, v, v, v, v, k, v, bias)


def _ctx_attn_kernel(sink_ref, q_ref, k_ref, v_ref, o_ref, *, use_sink):
    s = _dot_nt(q_ref[...], k_ref[...])
    m = jnp.max(s, axis=-1, keepdims=True)
    if use_sink:
        sk = sink_ref[pl.program_id(1)] * LOG2E
        m = jnp.maximum(m, sk)
    p = jnp.exp2(s - m)
    l = jnp.sum(p, axis=-1, keepdims=True)
    if use_sink:
        l = l + jnp.exp2(sk - m)
    o_ref[...] = (_dot(p.astype(BF), v_ref[...]) / l).astype(BF)


def _ctx_attn_call(sink, q, k, v, n_batch, n_lat, n_ctx, G, d, dv, use_sink, name):
    v_stride = 2
    H = q.shape[1] // d
    cb = (n_batch * n_lat) // n_ctx
    return pl.pallas_call(
        functools.partial(_ctx_attn_kernel, use_sink=use_sink),
        grid=(n_batch, H),
        in_specs=[
            pl.BlockSpec(memory_space=pltpu.SMEM),
            pl.BlockSpec((n_ctx, d), lambda b, h: (cb + b, h)),
            pl.BlockSpec((n_ctx, d), lambda b, h: (cb + b, h // G)),
            pl.BlockSpec((n_ctx, dv), lambda b, h: (cb + b, (h // G) * v_stride)),
        ],
        out_specs=pl.BlockSpec((n_ctx, dv), lambda b, h: (b, h)),
        out_shape=jax.ShapeDtypeStruct((n_batch * n_ctx, H * dv), BF),
        compiler_params=_cparams(("parallel", "parallel")),
        name=name,
    )(sink, q, k, v)


def _merge_kernel(x_ref, h_ref, oa_ref, om_ref, ow_ref, on_ref, wg_ref, bg_ref, wb_ref, wo_ref,
                  g1_ref, nf_ref, sh2_ref, sc2_ref, wr_ref, br_ref,
                  xo_ref, h2_ref, ei_ref, wt_ref, y_scr):
    j = pl.program_id(1)

    @pl.when(j == 0)
    def _():
        y_scr[...] = jnp.zeros_like(y_scr)

    h = h_ref[...]
    acc = None
    for i, o_ref in enumerate((oa_ref, om_ref, ow_ref, on_ref)):
        gate = _sigmoid(_dot(h, wg_ref[i]) + bg_ref[i])
        y = gate * _dot(o_ref[...], wb_ref[i])
        acc = y if acc is None else acc + y
    y_scr[...] += _dot(acc.astype(BF), wo_ref[...])

    @pl.when(j == pl.num_programs(1) - 1)
    def _():
        x = x_ref[...] + g1_ref[...] * y_scr[...]
        xo_ref[...] = x
        ms = jnp.mean(x * x, axis=-1, keepdims=True)
        h2 = x * lax.rsqrt(ms + EPS) * nf_ref[...]
        h2 = h2 * (1.0 + sc2_ref[...]) + sh2_ref[...]
        h2_ref[...] = h2
        hi = h2.astype(BF)
        lo = (h2 - hi.astype(F32)).astype(BF)
        logits = _dot(hi, wr_ref[0]) + (_dot(lo, wr_ref[0]) + _dot(hi, wr_ref[1]))
        scores = _sigmoid(logits)
        biased = scores + br_ref[...]
        lane = lax.broadcasted_iota(jnp.int32, logits.shape, 1)
        lane_f = lane.astype(F32)
        ninf = -jnp.inf

        def top2(vals):
            t1 = jnp.max(vals, axis=-1, keepdims=True)
            i1 = jnp.min(jnp.where(vals == t1, lane_f, float(LANES)), axis=-1, keepdims=True)
            vals2 = jnp.where(lane_f == i1, ninf, vals)
            t2 = jnp.max(vals2, axis=-1, keepdims=True)
            i2 = jnp.min(jnp.where(vals2 == t2, lane_f, float(LANES)), axis=-1, keepdims=True)
            return t1, i1, t2, i2

        per = N_EXPERTS // N_GROUPS
        best, gi = None, None
        for g in range(N_GROUPS):
            ing = (lane >= g * per) & (lane < (g + 1) * per)
            t1, _, t2, _ = top2(jnp.where(ing, biased, ninf))
            gs = t1 + t2
            if best is None:
                best, gi = gs, jnp.zeros_like(gs)
            else:
                better = gs > best
                best = jnp.where(better, gs, best)
                gi = jnp.where(better, float(g), gi)
        lane_grp = (lane >> 2).astype(F32)
        vals = jnp.where(lane < N_EXPERTS, jnp.where(lane_grp == gi, biased, NEG), ninf)
        _, i1, _, i2 = top2(vals)
        w1 = jnp.sum(jnp.where(lane_f == i1, scores, 0.0), axis=-1, keepdims=True)
        w2 = jnp.sum(jnp.where(lane_f == i2, scores, 0.0), axis=-1, keepdims=True)
        den = w1 + w2
        ei_ref[...] = jnp.where(lane == 0, i1, jnp.where(lane == 1, i2, 0.0)).astype(jnp.int32)
        wt_ref[...] = jnp.where(lane == 0, w1 / den, jnp.where(lane == 1, w2 / den, 0.0))


def _merge_call(X, h, outs, wg, bg, wb, wo, modl, norm_f, wr_p, br_p, rows, n_batch, n_lat, tm, tn):
    D = X.shape[1]
    bw = wb.shape[1]
    tpb = n_lat // tm

    def rowmap(i, j):
        return (i, 0)

    return pl.pallas_call(
        _merge_kernel,
        grid=(rows // tm, D // tn),
        in_specs=[
            pl.BlockSpec((tm, D), rowmap),
            pl.BlockSpec((tm, D), rowmap),
        ] + [pl.BlockSpec((tm, bw), rowmap)] * 4 + [
            pl.BlockSpec((4, D, tn), lambda i, j: (0, 0, j)),
            pl.BlockSpec((4, 1, tn), lambda i, j: (0, 0, j)),
            pl.BlockSpec((4, bw, tn), lambda i, j: (0, 0, j)),
            pl.BlockSpec((tn, D), lambda i, j: (j, 0)),
            _mod_spec(2, tpb, n_batch, D),
            pl.BlockSpec((1, D), lambda i, j: (0, 0)),
            _mod_spec(3, tpb, n_batch, D),
            _mod_spec(4, tpb, n_batch, D),
            pl.BlockSpec((2, D, LANES), lambda i, j: (0, 0, 0)),
            pl.BlockSpec((1, LANES), lambda i, j: (0, 0)),
        ],
        out_specs=[
            pl.BlockSpec((tm, D), rowmap),
            pl.BlockSpec((tm, D), rowmap),
            pl.BlockSpec((tm, LANES), rowmap),
            pl.BlockSpec((tm, LANES), rowmap),
        ],
        out_shape=[
            jax.ShapeDtypeStruct((rows, D), F32),
            jax.ShapeDtypeStruct((rows, D), F32),
            jax.ShapeDtypeStruct((rows, LANES), jnp.int32),
            jax.ShapeDtypeStruct((rows, LANES), F32),
        ],
        scratch_shapes=[pltpu.VMEM((tm, D), F32)],
        compiler_params=_cparams(("parallel", "arbitrary")),
        name="merge_residual_router",
    )(X, h, *outs, wg, bg, wb, wo, modl, norm_f.reshape(1, D), modl, modl, wr_p, br_p)


def _row_dma_issue(n_rows, make_copy):
    def issue(r, carry):
        for k in range(2):
            make_copy(r, k).start()
        return carry

    lax.fori_loop(0, n_rows, issue, 0, unroll=8)


def _row_dma_drain(n_rows, make_copy):
    def drain(r, carry):
        for k in range(2):
            make_copy(r, k).wait()
        return carry

    lax.fori_loop(0, n_rows, drain, 0, unroll=8)


def _row_dma_loop(n_rows, make_copy):
    _row_dma_issue(n_rows, make_copy)
    _row_dma_drain(n_rows, make_copy)


def _scatter_kernel(fill_ref, pos_ref, h_ref, xs_ref, zbuf, sem, zsem):
    tg = zbuf.shape[0]

    @pl.when(pl.program_id(0) == 0)
    def _():
        zbuf[...] = jnp.zeros_like(zbuf)

        def fill(t, carry):
            @pl.when(fill_ref[t] != 0)
            def _():
                cp = pltpu.make_async_copy(zbuf, xs_ref.at[pl.ds(pl.multiple_of(t * tg, tg), tg), :], zsem)
                cp.start()
                cp.wait()
            return carry

        lax.fori_loop(0, fill_ref.shape[0], fill, 0)

    def make_copy(r, k):
        return pltpu.make_async_copy(h_ref.at[pl.ds(r, 1), :],
                                     xs_ref.at[pl.ds(pos_ref[2 * r + k], 1), :], sem)

    _row_dma_loop(h_ref.shape[0], make_copy)


def _scatter_call(h2, pos, fill, n_slots, tm, tg):
    rows, D = h2.shape
    grid_spec = pltpu.PrefetchScalarGridSpec(
        num_scalar_prefetch=1,
        grid=(rows // tm,),
        in_specs=[
            pl.BlockSpec((2 * tm,), lambda i, f: (i,), memory_space=pltpu.SMEM),
            pl.BlockSpec((tm, D), lambda i, f: (i, 0)),
        ],
        out_specs=pl.BlockSpec(memory_space=pl.ANY),
        scratch_shapes=[pltpu.VMEM((tg, D), F32), pltpu.SemaphoreType.DMA(()), pltpu.SemaphoreType.DMA(())],
    )
    return pl.pallas_call(
        _scatter_kernel,
        grid_spec=grid_spec,
        out_shape=jax.ShapeDtypeStruct((n_slots, D), F32),
        compiler_params=_cparams(("arbitrary",)),
        name="moe_scatter_rows",
    )(fill, pos, h2)


def _experts_kernel(te_ref, nu_ref, x_ref, w1_ref, w3_ref, w2_ref, y_ref, w1_scr, w3_scr, w2_scr):
    i = pl.program_id(0)

    @pl.when((i == 0) | (te_ref[i] != te_ref[jnp.maximum(i - 1, 0)]))
    def _():
        w1_scr[...] = w1_ref[...].astype(BF)
        w3_scr[...] = w3_ref[...].astype(BF)
        w2_scr[...] = w2_ref[...].astype(BF)

    @pl.when(i < nu_ref[0])
    def _():
        x = x_ref[...].astype(BF)
        a = _dot(x, w1_scr[...])
        b = _dot(x, w3_scr[...])
        hid = (a * _sigmoid(a)) * b
        y_ref[...] = _dot(hid.astype(BF), w2_scr[...])

    @pl.when(i >= nu_ref[0])
    def _():
        y_ref[...] = jnp.zeros_like(y_ref)


def _experts_call(xs, tile_expert, n_used, w1, w3, w2, layer, tg):
    S, D = xs.shape
    de = w1.shape[3]
    grid_spec = pltpu.PrefetchScalarGridSpec(
        num_scalar_prefetch=2,
        grid=(S // tg,),
        in_specs=[
            pl.BlockSpec((tg, D), lambda i, te, nu: (i, 0)),
            pl.BlockSpec((None, None, D, de), lambda i, te, nu: (layer, te[i], 0, 0)),
            pl.BlockSpec((None, None, D, de), lambda i, te, nu: (layer, te[i], 0, 0)),
            pl.BlockSpec((None, None, de, D), lambda i, te, nu: (layer, te[i], 0, 0)),
        ],
        out_specs=pl.BlockSpec((tg, D), lambda i, te, nu: (i, 0)),
        scratch_shapes=[pltpu.VMEM((D, de), BF), pltpu.VMEM((D, de), BF), pltpu.VMEM((de, D), BF)],
    )
    return pl.pallas_call(
        _experts_kernel,
        grid_spec=grid_spec,
        out_shape=jax.ShapeDtypeStruct((S, D), F32),
        compiler_params=_cparams(("arbitrary",)),
        name="moe_experts",
    )(tile_expert, n_used, xs, w1, w3, w2)


def _combine_kernel(pos_ref, pos_next_ref, x_ref, wt_ref, g2_ref, ys_ref, o_ref, buf, sem):
    i = pl.program_id(0)
    n = pl.num_programs(0)
    tm = x_ref.shape[0]
    slot = i % 2

    def gather(p_ref, s):
        def make_copy(r, k):
            return pltpu.make_async_copy(ys_ref.at[pl.ds(p_ref[2 * r + k], 1), :],
                                         buf.at[s, k, pl.ds(r, 1), :], sem.at[s])
        return make_copy

    @pl.when(i == 0)
    def _():
        _row_dma_issue(tm, gather(pos_ref, 0))

    @pl.when(i + 1 < n)
    def _():
        _row_dma_issue(tm, gather(pos_next_ref, 1 - slot))

    _row_dma_drain(tm, gather(pos_ref, slot))
    wt = wt_ref[...]
    moe = wt[:, 0:1] * buf[slot, 0] + wt[:, 1:2] * buf[slot, 1]
    o_ref[...] = x_ref[...] + g2_ref[...] * moe


def _combine_call(X, ys, pos, wts, modl, n_batch, n_lat, tm):
    rows, D = X.shape
    tpb = n_lat // tm
    n_steps = rows // tm
    return pl.pallas_call(
        _combine_kernel,
        grid=(n_steps,),
        in_specs=[
            pl.BlockSpec((2 * tm,), lambda i: (i,), memory_space=pltpu.SMEM),
            pl.BlockSpec((2 * tm,), lambda i: (jnp.minimum(i + 1, n_steps - 1),), memory_space=pltpu.SMEM),
            pl.BlockSpec((tm, D), lambda i: (i, 0)),
            pl.BlockSpec((tm, LANES), lambda i: (i, 0)),
            _mod_spec(5, tpb, n_batch, D),
            pl.BlockSpec(memory_space=pl.ANY),
        ],
        out_specs=pl.BlockSpec((tm, D), lambda i: (i, 0)),
        out_shape=jax.ShapeDtypeStruct((rows, D), F32),
        scratch_shapes=[pltpu.VMEM((2, 2, tm, D), F32), pltpu.SemaphoreType.DMA((2,))],
        compiler_params=_cparams(("arbitrary",)),
        name="moe_combine_residual",
    )(pos, pos, X, wts, modl, ys)


def _route_positions(eidx, tg):
    rows = eidx.shape[0]
    e_flat = eidx.reshape(-1)
    onehot = (e_flat[:, None] == jnp.arange(N_EXPERTS, dtype=jnp.int32)[None, :]).astype(jnp.int32)
    csum = jnp.cumsum(onehot, axis=0)
    counts = csum[-1]
    rank = jnp.sum((csum - onehot) * onehot, axis=1)
    padded = ((counts + tg - 1) // tg) * tg
    ends = jnp.cumsum(padded)
    offsets = ends - padded
    pos = jnp.sum(onehot * offsets[None, :], axis=1) + rank
    n_tiles = (2 * rows + N_EXPERTS * tg) // tg
    tile_start = jnp.arange(n_tiles, dtype=jnp.int32) * tg
    tile_expert = jnp.sum((tile_start[:, None] >= ends[None, :]).astype(jnp.int32), axis=1)
    n_used = (ends[-1] // tg).astype(jnp.int32)
    used = jnp.arange(n_tiles) < n_used
    te_1h = (tile_expert[:, None] == jnp.arange(N_EXPERTS, dtype=jnp.int32)[None, :]).astype(jnp.int32)
    real_end = jnp.sum(te_1h * (offsets + counts)[None, :], axis=1)
    fill = jnp.logical_not(used & (tile_start + tg <= real_end)).astype(jnp.int32)
    last_e = jnp.sum((((n_used - 1) * tg) >= ends).astype(jnp.int32))
    tile_expert = jnp.where(used, tile_expert, last_e).astype(jnp.int32)
    return pos.astype(jnp.int32), tile_expert, n_used.reshape(1), fill


def _rope_tables(n_lat, tm):
    t = jnp.arange(n_lat)
    rows, cols = t // GRID_W, t % GRID_W

    def tab(d2, pos):
        freqs = ROPE_THETA ** (-jnp.arange(d2, dtype=F32) / d2)
        ang = pos.astype(F32)[:, None] * freqs[None, :]
        return jnp.cos(ang), jnp.sin(ang)

    cr, sr = tab(32, rows)
    cc, sc = tab(32, cols)
    cos_a = jnp.concatenate([cr, cr, cc, cc], axis=-1)
    sin_a = jnp.concatenate([-sr, sr, -sc, sc], axis=-1)
    cr, sr = tab(16, rows)
    cc, sc = tab(16, cols)
    one, zero = jnp.ones((n_lat, 64), F32), jnp.zeros((n_lat, 64), F32)
    cos_m = jnp.concatenate([cr, cr, cc, cc, one], axis=-1)
    sin_m = jnp.concatenate([-sr, sr, -sc, sc, zero], axis=-1)
    ident_c, ident_s = jnp.ones((tm, LANES), F32), jnp.zeros((tm, LANES), F32)
    return tuple(jnp.concatenate([a, b], axis=0) for a, b in
                 ((cos_a, ident_c), (sin_a, ident_s), (cos_m, ident_c), (sin_m, ident_s)))


def _pad_mla_heads(w, n_heads):
    lead = w.shape[:-1]
    w = w.reshape(lead + (n_heads, MLA_QK))
    w = jnp.pad(w, [(0, 0)] * len(lead) + [(0, 0), (0, MLA_PAD - MLA_QK)])
    return w.reshape(lead + (n_heads * MLA_PAD,))


def kernel(x, c, ctx, c_ctx, w_mod, b_mod, norm_mix, norm_ffn, w_in, mla_qa_norm, mla_w_uq, mla_kva_norm,
           mla_w_ukv, qn_att, kn_att, qn_mla, kn_mla, qn_win, kn_win, qn_na, kn_na, win_sink, na_rpb,
           w_branch, w_gate, b_gate, w_out, w_router, b_router, moe_w1, moe_w3, moe_w2):
    B, N, D = x.shape
    n_ctx = ctx.shape[1]
    L = w_mod.shape[0]
    R = B * N + B * n_ctx
    tm = 512
    tm_moe = 256
    tg = 512
    tn_merge = min(256, D // 2)
    tq = min(512, N)
    tk = min(512, N // 2)

    X = jnp.concatenate([x.reshape(B * N, D), ctx.reshape(B * n_ctx, D)], axis=0)
    cvec = jnp.zeros((8, D), F32).at[:B].set(c).at[B].set(c_ctx)
    mod = _mod_call(cvec, w_mod, b_mod, min(1024, D)).reshape(L, 8, 6, 1, D)
    tabs = _rope_tables(N, tm)

    wr_hi = w_router.astype(BF)
    wr_lo = (w_router - wr_hi.astype(F32)).astype(BF)
    wr_p = jnp.pad(jnp.stack([wr_hi, wr_lo]), ((0, 0), (0, 0), (0, LANES - N_EXPERTS)))
    br_p = jnp.pad(b_router.astype(F32), (0, LANES - N_EXPERTS)).reshape(1, LANES)

    for l in range(L):
        last = l == L - 1
        rows = B * N if last else R
        modl = mod[l]
        w_in_l = w_in[l]
        w_in_p = jnp.concatenate(
            [w_in_l[:, :KR_END], jnp.zeros((D, 64), F32), w_in_l[:, KR_END:]], axis=1).astype(BF)
        wuq_p = _pad_mla_heads(mla_w_uq[l], 4).astype(BF)
        wukv = mla_w_ukv[l].astype(BF)
        gains = [g.reshape(1, -1) for g in (
            qn_att[l], kn_att[l], qn_win[l], kn_win[l], qn_na[l], kn_na[l],
            _pad_mla_heads(qn_mla[l], 1), _pad_mla_heads(kn_mla[l], 1), mla_qa_norm[l], mla_kva_norm[l])]

        h, proj = _proj_call(X, norm_mix[l], modl, w_in_p, B, N, tm, P_TOTAL // 2)
        (qa, ka, va, qm, km, vm, qw, kw, vw, qn, kn, vn) = _heads_call(proj, tabs, gains, wuq_p, wukv, B, N, tm)

        sink = win_sink[l].astype(F32)
        o_att = _flash_call(qa, ka, va, B, N, n_ctx, 2, HEAD_DIM, tq, tk, "dense_gqa")
        o_mla = _flash_call(qm, km, vm, B, N, n_ctx, 1, MLA_PAD, 2 * tq, tk, "latent_attn")
        o_win = _window_call(sink, qw, kw, vw, B, N, n_ctx, 2, tq)
        bias = _na_bias_tables(na_rpb[l], N // GRID_W)
        o_na = _na_call(qn, kn, vn, bias, B, N, n_ctx)
        outs = [o_att, o_mla, o_win, o_na]
        if not last:
            outs_c = [
                _ctx_attn_call(sink, qa, ka, va, B, N, n_ctx, 2, HEAD_DIM, HEAD_DIM, False, "ctx_dense_gqa"),
                _ctx_attn_call(sink, qm, km, vm, B, N, n_ctx, 1, MLA_PAD, HEAD_DIM, False, "ctx_latent_attn"),
                _ctx_attn_call(sink, qw, kw, vw, B, N, n_ctx, 2, HEAD_DIM, HEAD_DIM, True, "ctx_window_attn"),
                _ctx_attn_call(sink, qn, kn, vn, B, N, n_ctx, 1, HEAD_DIM, HEAD_DIM, False, "ctx_neighbourhood"),
            ]
            outs = [jnp.concatenate([a, b], axis=0) for a, b in zip(outs, outs_c)]

        X, h2, eidx, wts = _merge_call(
            X, h, outs, w_gate[l].astype(BF), b_gate[l].reshape(4, 1, D), w_branch[l].astype(BF),
            w_out[l].astype(BF), modl, norm_ffn[l], wr_p, br_p, rows, B, N, tm, tn_merge)

        pos, tile_expert, n_used, fill = _route_positions(eidx[:, :2], tg)
        n_slots = 2 * rows + N_EXPERTS * tg
        xs = _scatter_call(h2, pos, fill, n_slots, tm_moe, tg)
        ys = _experts_call(xs, tile_expert, n_used, moe_w1, moe_w3, moe_w2, l, tg)
        X = _combine_call(X, ys, pos, wts, modl, B, N, tm_moe)

    return X.reshape(B, N, D)
```

```python
import functools

import numpy as np
import jax
import jax.numpy as jnp
from jax import lax
from jax.experimental import pallas as pl
from jax.experimental.pallas import tpu as pltpu

BF = jnp.bfloat16
F32 = jnp.float32

GRID_W = 64
HEAD_DIM = 128
ROPE_THETA = 10000.0
EPS = 1e-6
NEG = -1e30
LOG2E = 1.4426950408889634
WINDOW = 128
NA_WIN_H = 8
NA_WIN_W = 16
N_EXPERTS = 16
N_GROUPS = 4
MLA_QK = 192
MLA_PAD = 256
LANES = 128
VMEM_LIMIT = 56 * 1024 * 1024

P_AQ, P_AK, P_AV = 0, 512, 768
P_MCQ, P_MCKV, P_MKR = 1024, 1408, 1664
P_WQ, P_WK, P_WV = 1792, 2304, 2560
P_NQ, P_NK, P_NV = 2816, 3328, 3840
P_TOTAL = 4352
KR_END = 1728


def _cparams(sem):
    return pltpu.CompilerParams(dimension_semantics=sem, vmem_limit_bytes=VMEM_LIMIT)


def _sigmoid(z):
    return 1.0 / (1.0 + jnp.exp(-z))


def _dot(a, b):
    return jnp.dot(a, b, preferred_element_type=F32)


def _dot_nt(a, b):
    return lax.dot_general(a, b, (((1,), (1,)), ((), ())), preferred_element_type=F32)


def _mod_kernel(c_ref, w_ref, b_ref, o_ref):
    c = c_ref[...]
    s = c * _sigmoid(c)
    o_ref[...] = _dot(s.astype(BF), w_ref[...].astype(BF)) + b_ref[...]


def _mod_call(cvec, w_mod, b_mod, tn):
    L, D, D6 = w_mod.shape
    return pl.pallas_call(
        _mod_kernel,
        grid=(L, D6 // tn),
        in_specs=[
            pl.BlockSpec((8, D), lambda l, j: (0, 0)),
            pl.BlockSpec((None, D, tn), lambda l, j: (l, 0, j)),
            pl.BlockSpec((None, 1, tn), lambda l, j: (l, 0, j)),
        ],
        out_specs=pl.BlockSpec((None, 8, tn), lambda l, j: (l, 0, j)),
        out_shape=jax.ShapeDtypeStruct((L, 8, D6), F32),
        compiler_params=_cparams(("parallel", "parallel")),
        name="adaln_mod",
    )(cvec, w_mod, b_mod.reshape(L, 1, D6))


def _mod_spec(comp, tiles_per_batch, n_batch, D):
    return pl.BlockSpec(
        (None, None, 1, D),
        lambda i, *_: (jnp.minimum(i // tiles_per_batch, n_batch), comp, 0, 0))


def _proj_kernel(x_ref, g_ref, sh_ref, sc_ref, w_ref, h_ref, o_ref, h_scr):
    @pl.when(pl.program_id(1) == 0)
    def _():
        x = x_ref[...]
        ms = jnp.mean(x * x, axis=-1, keepdims=True)
        xn = x * lax.rsqrt(ms + EPS) * g_ref[...]
        h = (xn * (1.0 + sc_ref[...]) + sh_ref[...]).astype(BF)
        h_scr[...] = h
        h_ref[...] = h

    o_ref[...] = _dot(h_scr[...], w_ref[...])


def _proj_call(X, norm_g, modl, w_in_p, n_batch, n_lat, tm, tn):
    R, D = X.shape
    P = w_in_p.shape[1]
    tpb = n_lat // tm
    if tn == P:
        w_spec = pl.BlockSpec((D, tn), lambda i, j: (0, j), pipeline_mode=pl.Buffered(1))
    else:
        w_spec = pl.BlockSpec((D, tn), lambda i, j: (0, j))
    return pl.pallas_call(
        _proj_kernel,
        grid=(R // tm, P // tn),
        in_specs=[
            pl.BlockSpec((tm, D), lambda i, j: (i, 0)),
            pl.BlockSpec((1, D), lambda i, j: (0, 0)),
            _mod_spec(0, tpb, n_batch, D),
            _mod_spec(1, tpb, n_batch, D),
            w_spec,
        ],
        out_specs=[
            pl.BlockSpec((tm, D), lambda i, j: (i, 0)),
            pl.BlockSpec((tm, tn), lambda i, j: (i, j)),
        ],
        out_shape=[jax.ShapeDtypeStruct((R, D), BF), jax.ShapeDtypeStruct((R, P), F32)],
        scratch_shapes=[pltpu.VMEM((tm, D), BF)],
        compiler_params=_cparams(("parallel", "arbitrary")),
        name="norm_mod_proj",
    )(X, norm_g.reshape(1, D), modl, modl, w_in_p)


def _sumsq(x):
    xx = None
    for c in range(x.shape[1] // LANES):
        b = x[:, c * LANES:(c + 1) * LANES]
        xx = b * b if xx is None else xx + b * b
    hi = xx.astype(BF)
    lo = (xx - hi.astype(F32)).astype(BF)
    ones = jnp.ones((LANES, LANES), BF)
    return _dot(hi, ones) + _dot(lo, ones)


def _rms(x, g, n):
    r = lax.rsqrt(_sumsq(x) / n + EPS)
    blocks = [x[:, c * LANES:(c + 1) * LANES] * r * g[:, c * LANES:(c + 1) * LANES]
              for c in range(x.shape[1] // LANES)]
    return blocks[0] if len(blocks) == 1 else jnp.concatenate(blocks, axis=1)


def _rope(x, cos, sin_signed, half):
    lane = lax.broadcasted_iota(jnp.int32, x.shape, 1)
    first = (lane & (2 * half - 1)) < half
    xr = jnp.where(first, pltpu.roll(x, LANES - half, 1), pltpu.roll(x, half, 1))
    return x * cos + xr * sin_signed


def _heads_kernel(p_ref, ca_ref, sa_ref, cm_ref, sm_ref,
                  qa_g, ka_g, qw_g, kw_g, qn_g, kn_g, qm_g, km_g, cqn_g, ckvn_g, wuq_ref, wukv_ref,
                  qa_ref, ka_ref, va_ref, qm_ref, km_ref, vm_ref,
                  qw_ref, kw_ref, vw_ref, qn_ref, kn_ref, vn_ref):
    hd = HEAD_DIM
    sc128 = HEAD_DIM ** -0.5 * LOG2E
    sc192 = MLA_QK ** -0.5 * LOG2E
    ca, sa, cm, sm = ca_ref[...], sa_ref[...], cm_ref[...], sm_ref[...]

    def sl(off, h, w=hd):
        return p_ref[:, off + h * w: off + (h + 1) * w]

    ones = jnp.ones((p_ref.shape[0], hd), BF)

    def put_values(v_out, v_off, n_heads):
        for h in range(n_heads):
            v_out[:, 2 * h * hd:(2 * h + 1) * hd] = sl(v_off, h).astype(BF)
            v_out[:, (2 * h + 1) * hd:(2 * h + 2) * hd] = ones

    for (q_off, k_off, v_off, qg, kg, q_out, k_out, v_out) in (
            (P_AQ, P_AK, P_AV, qa_g, ka_g, qa_ref, ka_ref, va_ref),
            (P_WQ, P_WK, P_WV, qw_g, kw_g, qw_ref, kw_ref, vw_ref)):
        for h in range(4):
            q = _rope(_rms(sl(q_off, h), qg[...], hd), ca, sa, 32)
            q_out[:, h * hd:(h + 1) * hd] = (q * sc128).astype(BF)
        for h in range(2):
            k = _rope(_rms(sl(k_off, h), kg[...], hd), ca, sa, 32)
            k_out[:, h * hd:(h + 1) * hd] = k.astype(BF)
        put_values(v_out, v_off, 2)

    for h in range(4):
        qn_ref[:, h * hd:(h + 1) * hd] = (_rms(sl(P_NQ, h), qn_g[...], hd) * sc128).astype(BF)
        kn_ref[:, h * hd:(h + 1) * hd] = _rms(sl(P_NK, h), kn_g[...], hd).astype(BF)
    put_values(vn_ref, P_NV, 4)

    cq = _rms(p_ref[:, P_MCQ:P_MCQ + 384], cqn_g[...], 384)
    qf = _dot(cq.astype(BF), wuq_ref[...])
    ckv = _rms(p_ref[:, P_MCKV:P_MCKV + 256], ckvn_g[...], 256)
    kvf = _dot(ckv.astype(BF), wukv_ref[...])
    kr = p_ref[:, P_MKR:P_MKR + hd]
    kr_ss = _sumsq(kr)
    qg0, qg1 = qm_g[:, :hd], qm_g[:, hd:]
    kg0, kg1 = km_g[:, :hd], km_g[:, hd:]
    for h in range(4):
        q0 = qf[:, h * MLA_PAD: h * MLA_PAD + hd]
        q1 = qf[:, h * MLA_PAD + hd: (h + 1) * MLA_PAD]
        r = lax.rsqrt(_sumsq(qf[:, h * MLA_PAD:(h + 1) * MLA_PAD]) / MLA_QK + EPS)
        qm_ref[:, h * MLA_PAD: h * MLA_PAD + hd] = (q0 * r * qg0 * sc192).astype(BF)
        qm_ref[:, h * MLA_PAD + hd: (h + 1) * MLA_PAD] = (
            _rope(q1 * r * qg1, cm, sm, 16) * sc192).astype(BF)
        k0 = kvf[:, h * 2 * hd: h * 2 * hd + hd]
        r = lax.rsqrt((_sumsq(k0) + kr_ss) / MLA_QK + EPS)
        km_ref[:, h * MLA_PAD: h * MLA_PAD + hd] = (k0 * r * kg0).astype(BF)
        km_ref[:, h * MLA_PAD + hd: (h + 1) * MLA_PAD] = _rope(kr * r * kg1, cm, sm, 16).astype(BF)
        vm_ref[:, 2 * h * hd:(2 * h + 1) * hd] = kvf[:, h * 2 * hd + hd: (h + 1) * 2 * hd].astype(BF)
        vm_ref[:, (2 * h + 1) * hd:(2 * h + 2) * hd] = ones


def _heads_call(proj, tabs, gains, wuq_p, wukv, n_batch, n_lat, tm):
    R, P = proj.shape
    tpb = n_lat // tm
    tab_spec = pl.BlockSpec((tm, LANES), lambda i: (jnp.where(i < n_batch * tpb, i % tpb, tpb), 0))

    def full(a):
        return pl.BlockSpec(a.shape, lambda i: (0,) * a.ndim)

    widths = (512, 256, 512, 4 * MLA_PAD, 4 * MLA_PAD, 1024, 512, 256, 512, 512, 512, 1024)
    return pl.pallas_call(
        _heads_kernel,
        grid=(R // tm,),
        in_specs=[pl.BlockSpec((tm, P), lambda i: (i, 0))] + [tab_spec] * 4
        + [full(g) for g in gains] + [full(wuq_p), full(wukv)],
        out_specs=[pl.BlockSpec((tm, w), lambda i: (i, 0)) for w in widths],
        out_shape=[jax.ShapeDtypeStruct((R, w), BF) for w in widths],
        compiler_params=_cparams(("parallel",)),
        name="head_prep",
    )(proj, *tabs, *gains, wuq_p, wukv)


def _flash_kernel(q_ref, k_ref, v_ref, kc_ref, vc_ref, o_ref, q_scr, s_scr, m_scr, acc_scr,
                  *, G, d, dv, w, unroll):
    tq = q_ref.shape[0]
    n_blk = k_ref.shape[0] // w
    for g in range(G):
        q_scr[g * tq:(g + 1) * tq, :] = q_ref[:, g * d:(g + 1) * d]
    m_scr[...] = jnp.full_like(m_scr, NEG)
    acc_scr[...] = jnp.zeros_like(acc_scr)

    def scores(k):
        return _dot_nt(q_scr[...], k)

    def absorb(s, v):
        m_prev = m_scr[...]
        m_next = jnp.maximum(m_prev, s.max(axis=1, keepdims=True))
        alpha = jnp.exp2(m_prev - m_next)
        p = jnp.concatenate(
            [jnp.exp2(s[:, c * LANES:(c + 1) * LANES] - m_next) for c in range(s.shape[1] // LANES)],
            axis=1)
        pv = _dot(p.astype(BF), v)
        acc_scr[...] = jnp.concatenate([alpha, alpha], axis=1) * acc_scr[...] + pv
        m_scr[...] = m_next

    absorb(scores(kc_ref[...]), vc_ref[...])
    s_scr[0] = scores(k_ref[0:w, :])

    def group(jj, carry):
        base = jj * (unroll * w)
        for u in range(unroll):
            cur = pl.multiple_of(base + u * w, w)
            nxt = pl.multiple_of(jnp.minimum(base + (u + 1) * w, (n_blk - 1) * w), w)
            s_scr[(u + 1) % 2] = scores(k_ref[pl.ds(nxt, w), :])
            absorb(s_scr[u % 2], v_ref[pl.ds(cur, w), :])
        return carry

    lax.fori_loop(0, n_blk // unroll, group, 0)

    acc = acc_scr[...]
    o = acc[:, :dv] / acc[:, dv:]
    for g in range(G):
        o_ref[:, g * dv:(g + 1) * dv] = o[g * tq:(g + 1) * tq].astype(BF)


def _flash_call(q, k, v, n_batch, n_lat, n_ctx, G, d, tq, w, name):
    dv = LANES
    hkv = k.shape[1] // d
    nq = n_lat // tq
    cb = (n_batch * n_lat) // n_ctx
    unroll = 4 if (n_lat // w) % 4 == 0 else 2
    assert (n_lat // w) % unroll == 0
    return pl.pallas_call(
        functools.partial(_flash_kernel, G=G, d=d, dv=dv, w=w, unroll=unroll),
        grid=(n_batch, hkv, nq),
        in_specs=[
            pl.BlockSpec((tq, G * d), lambda b, h, i: (b * nq + i, h)),
            pl.BlockSpec((n_lat, d), lambda b, h, i: (b, h)),
            pl.BlockSpec((n_lat, 2 * dv), lambda b, h, i: (b, h)),
            pl.BlockSpec((n_ctx, d), lambda b, h, i: (cb + b, h)),
            pl.BlockSpec((n_ctx, 2 * dv), lambda b, h, i: (cb + b, h)),
        ],
        out_specs=pl.BlockSpec((tq, G * dv), lambda b, h, i: (b * nq + i, h)),
        out_shape=jax.ShapeDtypeStruct((n_batch * n_lat, hkv * G * dv), BF),
        scratch_shapes=[pltpu.VMEM((G * tq, d), BF), pltpu.VMEM((2, G * tq, w), F32),
                        pltpu.VMEM((G * tq, LANES), F32), pltpu.VMEM((G * tq, 2 * dv), F32)],
        compiler_params=_cparams(("parallel", "parallel", "parallel")),
        name=name,
    )(q, k, v, k, v)


def _window_kernel(sink_ref, q_ref, kp_ref, km_ref, kn_ref, vp_ref, vm_ref, vn_ref, kc_ref, vc_ref,
                   o_ref, s_scr, sc_scr, *, G):
    hd = HEAD_DIM
    i = pl.program_id(1)
    ni = pl.num_programs(1)
    tq = q_ref.shape[0]
    n_heads = q_ref.shape[1] // hd
    r = lax.broadcasted_iota(jnp.int32, (tq, tq + 2 * WINDOW), 0)
    c = lax.broadcasted_iota(jnp.int32, (tq, tq + 2 * WINDOW), 1)
    rel = c - WINDOW - r
    lo = jnp.where(i > 0, 0, WINDOW)
    hi = jnp.where(i < ni - 1, tq + 2 * WINDOW, tq + WINDOW)
    valid = (jnp.abs(rel) <= WINDOW) & (c >= lo) & (c < hi)

    def kv_cat(p_ref, m_ref, n_ref, kvh, w):
        cols = slice(kvh * w, (kvh + 1) * w)
        return jnp.concatenate([p_ref[:, cols], m_ref[:, cols], n_ref[:, cols]], axis=0)

    for qh in range(n_heads):
        kvh = qh // G
        q = q_ref[:, qh * hd:(qh + 1) * hd]
        s_scr[qh] = jnp.where(valid, _dot_nt(q, kv_cat(kp_ref, km_ref, kn_ref, kvh, hd)), NEG)
        sc_scr[qh] = _dot_nt(q, kc_ref[:, kvh * hd:(kvh + 1) * hd])
    for qh in range(n_heads):
        kvh = qh // G
        s = s_scr[qh]
        sc = sc_scr[qh]
        sk = sink_ref[qh] * LOG2E
        m = jnp.maximum(jnp.max(s, axis=-1, keepdims=True), jnp.max(sc, axis=-1, keepdims=True))
        m = jnp.maximum(m, sk)
        p = jnp.exp2(s - m)
        pc = jnp.exp2(sc - m)
        o = (_dot(p.astype(BF), kv_cat(vp_ref, vm_ref, vn_ref, kvh, 2 * hd))
             + _dot(pc.astype(BF), vc_ref[:, kvh * 2 * hd:(kvh + 1) * 2 * hd]))
        l = o[:, hd:] + jnp.exp2(sk - m)
        o_ref[:, qh * hd:(qh + 1) * hd] = (o[:, :hd] / l).astype(BF)


def _window_call(sink, q, k, v, n_batch, n_lat, n_ctx, G, tq):
    hd = HEAD_DIM
    hkv = k.shape[1] // hd
    nq = n_lat // tq
    bpt = tq // WINDOW
    nblk = k.shape[0] // WINDOW
    cb = (n_batch * n_lat) // n_ctx

    def prev(b, i):
        return (jnp.maximum((b * nq + i) * bpt - 1, 0), 0)

    def nxt(b, i):
        return (jnp.minimum((b * nq + i + 1) * bpt, nblk - 1), 0)

    def main(b, i):
        return (b * nq + i, 0)

    def ctxb(b, i):
        return (cb + b, 0)

    kw, vw, qw = hkv * hd, hkv * 2 * hd, hkv * G * hd
    return pl.pallas_call(
        functools.partial(_window_kernel, G=G),
        grid=(n_batch, nq),
        in_specs=[
            pl.BlockSpec(memory_space=pltpu.SMEM),
            pl.BlockSpec((tq, qw), main),
            pl.BlockSpec((WINDOW, kw), prev), pl.BlockSpec((tq, kw), main), pl.BlockSpec((WINDOW, kw), nxt),
            pl.BlockSpec((WINDOW, vw), prev), pl.BlockSpec((tq, vw), main), pl.BlockSpec((WINDOW, vw), nxt),
            pl.BlockSpec((n_ctx, kw), ctxb), pl.BlockSpec((n_ctx, vw), ctxb),
        ],
        out_specs=pl.BlockSpec((tq, qw), main),
        out_shape=jax.ShapeDtypeStruct((n_batch * n_lat, qw), BF),
        scratch_shapes=[pltpu.VMEM((hkv * G, tq, tq + 2 * WINDOW), F32), pltpu.VMEM((hkv * G, tq, n_ctx), F32)],
        compiler_params=_cparams(("parallel", "parallel")),
        name="window_attn",
    )(sink, q, k, k, k, v, v, v, k, v)


NA_TQ = 8 * GRID_W
NA_KB = 4 * GRID_W
NA_NKB = 4


def _na_kernel(q_ref, k0, k1, k2, k3, v0, v1, v2, v3, kc_ref, vc_ref, bias_ref, o_ref, s_scr, sc_scr):
    hd = HEAD_DIM
    n_heads = q_ref.shape[1] // hd

    def cat(refs, h, w):
        return jnp.concatenate([r[:, h * w:(h + 1) * w] for r in refs], axis=0)

    for h in range(n_heads):
        q = q_ref[:, h * hd:(h + 1) * hd]
        s_scr[h] = _dot_nt(q, cat((k0, k1, k2, k3), h, hd)) + bias_ref[h]
        sc_scr[h] = _dot_nt(q, kc_ref[:, h * hd:(h + 1) * hd])
    for h in range(n_heads):
        s = s_scr[h]
        sc = sc_scr[h]
        m = jnp.maximum(jnp.max(s, axis=-1, keepdims=True), jnp.max(sc, axis=-1, keepdims=True))
        p = jnp.exp2(s - m)
        pc = jnp.exp2(sc - m)
        o = (_dot(p.astype(BF), cat((v0, v1, v2, v3), h, 2 * hd))
             + _dot(pc.astype(BF), vc_ref[:, h * 2 * hd:(h + 1) * 2 * hd]))
        o_ref[:, h * hd:(h + 1) * hd] = (o[:, :hd] / o[:, hd:]).astype(BF)


def _na_bias_tables(rpb, n_rows):
    H = rpb.shape[0]
    J = n_rows // 8
    nb = n_rows // 4
    n_dy, n_dx = 2 * NA_WIN_H - 1, 2 * NA_WIN_W - 1
    cq = np.arange(GRID_W)
    kc = np.arange(GRID_W)
    cs = np.clip(cq - NA_WIN_W // 2, 0, GRID_W - NA_WIN_W)
    dx = kc[None, :] - cq[:, None] + NA_WIN_W - 1
    col_ok = (kc[None, :] >= cs[:, None]) & (kc[None, :] < cs[:, None] + NA_WIN_W)
    dx1h = np.zeros((n_dx, GRID_W * GRID_W), np.float32)
    dx1h[np.clip(dx, 0, n_dx - 1).reshape(-1), np.arange(GRID_W * GRID_W)] = 1.0
    blocks = jnp.einsum("hyx,xb->hyb", rpb.astype(F32) * LOG2E, jnp.asarray(dx1h),
                        precision=lax.Precision.HIGHEST)
    blocks = jnp.where(jnp.asarray(col_ok.reshape(-1))[None, None, :], blocks, NEG)
    blocks = jnp.concatenate([blocks, jnp.full((H, 1, GRID_W * GRID_W), NEG, F32)], axis=1)
    blocks = blocks.reshape(H, n_dy + 1, GRID_W, GRID_W)
    idx = np.zeros((3, 8, 4 * NA_NKB), np.int32)
    for v, jv in enumerate((0, min(1, J - 1), J - 1)):
        r = 8 * jv + np.arange(8)
        kb_un = 2 * jv - 1 + np.arange(NA_NKB)
        kb = np.clip(kb_un, 0, nb - 1)
        krow = (4 * kb[:, None] + np.arange(4)[None, :]).reshape(-1)
        krow_dup = np.repeat(kb != kb_un, 4)
        rs = np.clip(r - NA_WIN_H // 2, 0, n_rows - NA_WIN_H)
        row_ok = (krow[None, :] >= rs[:, None]) & (krow[None, :] < rs[:, None] + NA_WIN_H) & ~krow_dup[None, :]
        dy = krow[None, :] - r[:, None] + NA_WIN_H - 1
        idx[v] = np.where(row_ok, np.clip(dy, 0, n_dy - 1), n_dy)
    t = jnp.take(blocks, jnp.asarray(idx.reshape(-1)), axis=1)
    t = t.reshape(H, 3, 8, 4 * NA_NKB, GRID_W, GRID_W).transpose(0, 1, 2, 4, 3, 5)
    return t.reshape(H, 3, NA_TQ, NA_NKB * NA_KB)


def _na_call(q, k, v, bias, n_batch, n_lat, n_ctx):
    hd = HEAD_DIM
    H = q.shape[1] // hd
    J = n_lat // NA_TQ
    nb = n_lat // NA_KB
    cb = (n_batch * n_lat) // n_ctx

    def kblk(t):
        return lambda b, j: (b * nb + jnp.clip(2 * j - 1 + t, 0, nb - 1), 0)

    def qmap(b, j):
        return (b * J + j, 0)

    def ctxb(b, j):
        return (cb + b, 0)

    def bmap(b, j):
        return (0, jnp.where(j == 0, 0, jnp.where(j == J - 1, 2, 1)), 0, 0)

    n_keys = NA_NKB * NA_KB
    return pl.pallas_call(
        _na_kernel,
        grid=(n_batch, J),
        in_specs=[pl.BlockSpec((NA_TQ, H * hd), qmap)]
        + [pl.BlockSpec((NA_KB, H * hd), kblk(t)) for t in range(NA_NKB)]
        + [pl.BlockSpec((NA_KB, H * 2 * hd), kblk(t)) for t in range(NA_NKB)]
        + [pl.BlockSpec((n_ctx, H * hd), ctxb), pl.BlockSpec((n_ctx, H * 2 * hd), ctxb),
           pl.BlockSpec((H, None, NA_TQ, n_keys), bmap)],
        out_specs=pl.BlockSpec((NA_TQ, H * hd), qmap),
        out_shape=jax.ShapeDtypeStruct((n_batch * n_lat, H * hd), BF),
        scratch_shapes=[pltpu.VMEM((H, NA_TQ, n_keys), F32), pltpu.VMEM((H, NA_TQ, n_ctx), F32)],
        compiler_params=_cparams(("parallel", "parallel")),
        name="neighbourhood_attn",
    )(q, k, k, k, k, v, v, v, v, k, v, bias)


def _ctx_attn_kernel(sink_ref, q_ref, k_ref, v_ref, o_ref, *, use_sink):
    s = _dot_nt(q_ref[...], k_ref[...])
    m = jnp.max(s, axis=-1, keepdims=True)
    if use_sink:
        sk = sink_ref[pl.program_id(1)] * LOG2E
        m = jnp.maximum(m, sk)
    p = jnp.exp2(s - m)
    l = jnp.sum(p, axis=-1, keepdims=True)
    if use_sink:
        l = l + jnp.exp2(sk - m)
    o_ref[...] = (_dot(p.astype(BF), v_ref[...]) / l).astype(BF)


def _ctx_attn_call(sink, q, k, v, n_batch, n_lat, n_ctx, G, d, dv, use_sink, name):
    v_stride = 2
    H = q.shape[1] // d
    cb = (n_batch * n_lat) // n_ctx
    return pl.pallas_call(
        functools.partial(_ctx_attn_kernel, use_sink=use_sink),
        grid=(n_batch, H),
        in_specs=[
            pl.BlockSpec(memory_space=pltpu.SMEM),
            pl.BlockSpec((n_ctx, d), lambda b, h: (cb + b, h)),
            pl.BlockSpec((n_ctx, d), lambda b, h: (cb + b, h // G)),
            pl.BlockSpec((n_ctx, dv), lambda b, h: (cb + b, (h // G) * v_stride)),
        ],
        out_specs=pl.BlockSpec((n_ctx, dv), lambda b, h: (b, h)),
        out_shape=jax.ShapeDtypeStruct((n_batch * n_ctx, H * dv), BF),
        compiler_params=_cparams(("parallel", "parallel")),
        name=name,
    )(sink, q, k, v)


def _merge_kernel(x_ref, h_ref, oa_ref, om_ref, ow_ref, on_ref, wg_ref, bg_ref, wb_ref, wo_ref,
                  g1_ref, nf_ref, sh2_ref, sc2_ref, wr_ref, br_ref,
                  xo_ref, h2_ref, ei_ref, wt_ref, y_scr):
    j = pl.program_id(1)

    @pl.when(j == 0)
    def _():
        y_scr[...] = jnp.zeros_like(y_scr)

    h = h_ref[...]
    acc = None
    for i, o_ref in enumerate((oa_ref, om_ref, ow_ref, on_ref)):
        gate = _sigmoid(_dot(h, wg_ref[i]) + bg_ref[i])
        y = gate * _dot(o_ref[...], wb_ref[i])
        acc = y if acc is None else acc + y
    y_scr[...] += _dot(acc.astype(BF), wo_ref[...])

    @pl.when(j == pl.num_programs(1) - 1)
    def _():
        x = x_ref[...] + g1_ref[...] * y_scr[...]
        xo_ref[...] = x
        ms = jnp.mean(x * x, axis=-1, keepdims=True)
        h2 = x * lax.rsqrt(ms + EPS) * nf_ref[...]
        h2 = h2 * (1.0 + sc2_ref[...]) + sh2_ref[...]
        h2_ref[...] = h2
        hi = h2.astype(BF)
        lo = (h2 - hi.astype(F32)).astype(BF)
        logits = _dot(hi, wr_ref[0]) + (_dot(lo, wr_ref[0]) + _dot(hi, wr_ref[1]))
        scores = _sigmoid(logits)
        biased = scores + br_ref[...]
        lane = lax.broadcasted_iota(jnp.int32, logits.shape, 1)
        lane_f = lane.astype(F32)
        ninf = -jnp.inf

        def top2(vals):
            t1 = jnp.max(vals, axis=-1, keepdims=True)
            i1 = jnp.min(jnp.where(vals == t1, lane_f, float(LANES)), axis=-1, keepdims=True)
            vals2 = jnp.where(lane_f == i1, ninf, vals)
            t2 = jnp.max(vals2, axis=-1, keepdims=True)
            i2 = jnp.min(jnp.where(vals2 == t2, lane_f, float(LANES)), axis=-1, keepdims=True)
            return t1, i1, t2, i2

        per = N_EXPERTS // N_GROUPS
        best, gi = None, None
        for g in range(N_GROUPS):
            ing = (lane >= g * per) & (lane < (g + 1) * per)
            t1, _, t2, _ = top2(jnp.where(ing, biased, ninf))
            gs = t1 + t2
            if best is None:
                best, gi = gs, jnp.zeros_like(gs)
            else:
                better = gs > best
                best = jnp.where(better, gs, best)
                gi = jnp.where(better, float(g), gi)
        lane_grp = (lane >> 2).astype(F32)
        vals = jnp.where(lane < N_EXPERTS, jnp.where(lane_grp == gi, biased, NEG), ninf)
        _, i1, _, i2 = top2(vals)
        w1 = jnp.sum(jnp.where(lane_f == i1, scores, 0.0), axis=-1, keepdims=True)
        w2 = jnp.sum(jnp.where(lane_f == i2, scores, 0.0), axis=-1, keepdims=True)
        den = w1 + w2
        ei_ref[...] = jnp.where(lane == 0, i1, jnp.where(lane == 1, i2, 0.0)).astype(jnp.int32)
        wt_ref[...] = jnp.where(lane == 0, w1 / den, jnp.where(lane == 1, w2 / den, 0.0))


def _merge_call(X, h, outs, wg, bg, wb, wo, modl, norm_f, wr_p, br_p, rows, n_batch, n_lat, tm, tn):
    D = X.shape[1]
    bw = wb.shape[1]
    tpb = n_lat // tm

    def rowmap(i, j):
        return (i, 0)

    return pl.pallas_call(
        _merge_kernel,
        grid=(rows // tm, D // tn),
        in_specs=[
            pl.BlockSpec((tm, D), rowmap),
            pl.BlockSpec((tm, D), rowmap),
        ] + [pl.BlockSpec((tm, bw), rowmap)] * 4 + [
            pl.BlockSpec((4, D, tn), lambda i, j: (0, 0, j)),
            pl.BlockSpec((4, 1, tn), lambda i, j: (0, 0, j)),
            pl.BlockSpec((4, bw, tn), lambda i, j: (0, 0, j)),
            pl.BlockSpec((tn, D), lambda i, j: (j, 0)),
            _mod_spec(2, tpb, n_batch, D),
            pl.BlockSpec((1, D), lambda i, j: (0, 0)),
            _mod_spec(3, tpb, n_batch, D),
            _mod_spec(4, tpb, n_batch, D),
            pl.BlockSpec((2, D, LANES), lambda i, j: (0, 0, 0)),
            pl.BlockSpec((1, LANES), lambda i, j: (0, 0)),
        ],
        out_specs=[
            pl.BlockSpec((tm, D), rowmap),
            pl.BlockSpec((tm, D), rowmap),
            pl.BlockSpec((tm, LANES), rowmap),
            pl.BlockSpec((tm, LANES), rowmap),
        ],
        out_shape=[
            jax.ShapeDtypeStruct((rows, D), F32),
            jax.ShapeDtypeStruct((rows, D), F32),
            jax.ShapeDtypeStruct((rows, LANES), jnp.int32),
            jax.ShapeDtypeStruct((rows, LANES), F32),
        ],
        scratch_shapes=[pltpu.VMEM((tm, D), F32)],
        compiler_params=_cparams(("parallel", "arbitrary")),
        name="merge_residual_router",
    )(X, h, *outs, wg, bg, wb, wo, modl, norm_f.reshape(1, D), modl, modl, wr_p, br_p)


def _row_dma_issue(n_rows, make_copy):
    def issue(r, carry):
        for k in range(2):
            make_copy(r, k).start()
        return carry

    lax.fori_loop(0, n_rows, issue, 0, unroll=8)


def _row_dma_drain(n_rows, make_copy):
    def drain(r, carry):
        for k in range(2):
            make_copy(r, k).wait()
        return carry

    lax.fori_loop(0, n_rows, drain, 0, unroll=8)


def _row_dma_loop(n_rows, make_copy):
    _row_dma_issue(n_rows, make_copy)
    _row_dma_drain(n_rows, make_copy)


def _scatter_kernel(fill_ref, pos_ref, h_ref, xs_ref, zbuf, sem, zsem):
    tg = zbuf.shape[0]

    @pl.when(pl.program_id(0) == 0)
    def _():
        zbuf[...] = jnp.zeros_like(zbuf)

        def fill(t, carry):
            @pl.when(fill_ref[t] != 0)
            def _():
                cp = pltpu.make_async_copy(zbuf, xs_ref.at[pl.ds(pl.multiple_of(t * tg, tg), tg), :], zsem)
                cp.start()
                cp.wait()
            return carry

        lax.fori_loop(0, fill_ref.shape[0], fill, 0)

    def make_copy(r, k):
        return pltpu.make_async_copy(h_ref.at[pl.ds(r, 1), :],
                                     xs_ref.at[pl.ds(pos_ref[2 * r + k], 1), :], sem)

    _row_dma_loop(h_ref.shape[0], make_copy)


def _scatter_call(h2, pos, fill, n_slots, tm, tg):
    rows, D = h2.shape
    grid_spec = pltpu.PrefetchScalarGridSpec(
        num_scalar_prefetch=1,
        grid=(rows // tm,),
        in_specs=[
            pl.BlockSpec((2 * tm,), lambda i, f: (i,), memory_space=pltpu.SMEM),
            pl.BlockSpec((tm, D), lambda i, f: (i, 0)),
        ],
        out_specs=pl.BlockSpec(memory_space=pl.ANY),
        scratch_shapes=[pltpu.VMEM((tg, D), F32), pltpu.SemaphoreType.DMA(()), pltpu.SemaphoreType.DMA(())],
    )
    return pl.pallas_call(
        _scatter_kernel,
        grid_spec=grid_spec,
        out_shape=jax.ShapeDtypeStruct((n_slots, D), F32),
        compiler_params=_cparams(("arbitrary",)),
        name="moe_scatter_rows",
    )(fill, pos, h2)


def _experts_kernel(te_ref, nu_ref, x_ref, w1_ref, w3_ref, w2_ref, y_ref, w1_scr, w3_scr, w2_scr):
    i = pl.program_id(0)

    @pl.when((i == 0) | (te_ref[i] != te_ref[jnp.maximum(i - 1, 0)]))
    def _():
        w1_scr[...] = w1_ref[...].astype(BF)
        w3_scr[...] = w3_ref[...].astype(BF)
        w2_scr[...] = w2_ref[...].astype(BF)

    @pl.when(i < nu_ref[0])
    def _():
        x = x_ref[...].astype(BF)
        a = _dot(x, w1_scr[...])
        b = _dot(x, w3_scr[...])
        hid = (a * _sigmoid(a)) * b
        y_ref[...] = _dot(hid.astype(BF), w2_scr[...])

    @pl.when(i >= nu_ref[0])
    def _():
        y_ref[...] = jnp.zeros_like(y_ref)


def _experts_call(xs, tile_expert, n_used, w1, w3, w2, layer, tg):
    S, D = xs.shape
    de = w1.shape[3]
    grid_spec = pltpu.PrefetchScalarGridSpec(
        num_scalar_prefetch=2,
        grid=(S // tg,),
        in_specs=[
            pl.BlockSpec((tg, D), lambda i, te, nu: (i, 0)),
            pl.BlockSpec((None, None, D, de), lambda i, te, nu: (layer, te[i], 0, 0)),
            pl.BlockSpec((None, None, D, de), lambda i, te, nu: (layer, te[i], 0, 0)),
            pl.BlockSpec((None, None, de, D), lambda i, te, nu: (layer, te[i], 0, 0)),
        ],
        out_specs=pl.BlockSpec((tg, D), lambda i, te, nu: (i, 0)),
        scratch_shapes=[pltpu.VMEM((D, de), BF), pltpu.VMEM((D, de), BF), pltpu.VMEM((de, D), BF)],
    )
    return pl.pallas_call(
        _experts_kernel,
        grid_spec=grid_spec,
        out_shape=jax.ShapeDtypeStruct((S, D), F32),
        compiler_params=_cparams(("arbitrary",)),
        name="moe_experts",
    )(tile_expert, n_used, xs, w1, w3, w2)


def _combine_kernel(pos_ref, pos_next_ref, x_ref, wt_ref, g2_ref, ys_ref, o_ref, buf, sem):
    i = pl.program_id(0)
    n = pl.num_programs(0)
    tm = x_ref.shape[0]
    slot = i % 2

    def gather(p_ref, s):
        def make_copy(r, k):
            return pltpu.make_async_copy(ys_ref.at[pl.ds(p_ref[2 * r + k], 1), :],
                                         buf.at[s, k, pl.ds(r, 1), :], sem.at[s])
        return make_copy

    @pl.when(i == 0)
    def _():
        _row_dma_issue(tm, gather(pos_ref, 0))

    @pl.when(i + 1 < n)
    def _():
        _row_dma_issue(tm, gather(pos_next_ref, 1 - slot))

    _row_dma_drain(tm, gather(pos_ref, slot))
    wt = wt_ref[...]
    moe = wt[:, 0:1] * buf[slot, 0] + wt[:, 1:2] * buf[slot, 1]
    o_ref[...] = x_ref[...] + g2_ref[...] * moe


def _combine_call(X, ys, pos, wts, modl, n_batch, n_lat, tm):
    rows, D = X.shape
    tpb = n_lat // tm
    n_steps = rows // tm
    return pl.pallas_call(
        _combine_kernel,
        grid=(n_steps,),
        in_specs=[
            pl.BlockSpec((2 * tm,), lambda i: (i,), memory_space=pltpu.SMEM),
            pl.BlockSpec((2 * tm,), lambda i: (jnp.minimum(i + 1, n_steps - 1),), memory_space=pltpu.SMEM),
            pl.BlockSpec((tm, D), lambda i: (i, 0)),
            pl.BlockSpec((tm, LANES), lambda i: (i, 0)),
            _mod_spec(5, tpb, n_batch, D),
            pl.BlockSpec(memory_space=pl.ANY),
        ],
        out_specs=pl.BlockSpec((tm, D), lambda i: (i, 0)),
        out_shape=jax.ShapeDtypeStruct((rows, D), F32),
        scratch_shapes=[pltpu.VMEM((2, 2, tm, D), F32), pltpu.SemaphoreType.DMA((2,))],
        compiler_params=_cparams(("arbitrary",)),
        name="moe_combine_residual",
    )(pos, pos, X, wts, modl, ys)


def _route_positions(eidx, tg):
    rows = eidx.shape[0]
    e_flat = eidx.reshape(-1)
    onehot = (e_flat[:, None] == jnp.arange(N_EXPERTS, dtype=jnp.int32)[None, :]).astype(jnp.int32)
    csum = jnp.cumsum(onehot, axis=0)
    counts = csum[-1]
    rank = jnp.sum((csum - onehot) * onehot, axis=1)
    padded = ((counts + tg - 1) // tg) * tg
    ends = jnp.cumsum(padded)
    offsets = ends - padded
    pos = jnp.sum(onehot * offsets[None, :], axis=1) + rank
    n_tiles = (2 * rows + N_EXPERTS * tg) // tg
    tile_start = jnp.arange(n_tiles, dtype=jnp.int32) * tg
    tile_expert = jnp.sum((tile_start[:, None] >= ends[None, :]).astype(jnp.int32), axis=1)
    n_used = (ends[-1] // tg).astype(jnp.int32)
    used = jnp.arange(n_tiles) < n_used
    te_1h = (tile_expert[:, None] == jnp.arange(N_EXPERTS, dtype=jnp.int32)[None, :]).astype(jnp.int32)
    real_end = jnp.sum(te_1h * (offsets + counts)[None, :], axis=1)
    fill = jnp.logical_not(used & (tile_start + tg <= real_end)).astype(jnp.int32)
    last_e = jnp.sum((((n_used - 1) * tg) >= ends).astype(jnp.int32))
    tile_expert = jnp.where(used, tile_expert, last_e).astype(jnp.int32)
    return pos.astype(jnp.int32), tile_expert, n_used.reshape(1), fill


def _rope_tables(n_lat, tm):
    t = jnp.arange(n_lat)
    rows, cols = t // GRID_W, t % GRID_W

    def tab(d2, pos):
        freqs = ROPE_THETA ** (-jnp.arange(d2, dtype=F32) / d2)
        ang = pos.astype(F32)[:, None] * freqs[None, :]
        return jnp.cos(ang), jnp.sin(ang)

    cr, sr = tab(32, rows)
    cc, sc = tab(32, cols)
    cos_a = jnp.concatenate([cr, cr, cc, cc], axis=-1)
    sin_a = jnp.concatenate([-sr, sr, -sc, sc], axis=-1)
    cr, sr = tab(16, rows)
    cc, sc = tab(16, cols)
    one, zero = jnp.ones((n_lat, 64), F32), jnp.zeros((n_lat, 64), F32)
    cos_m = jnp.concatenate([cr, cr, cc, cc, one], axis=-1)
    sin_m = jnp.concatenate([-sr, sr, -sc, sc, zero], axis=-1)
    ident_c, ident_s = jnp.ones((tm, LANES), F32), jnp.zeros((tm, LANES), F32)
    return tuple(jnp.concatenate([a, b], axis=0) for a, b in
                 ((cos_a, ident_c), (sin_a, ident_s), (cos_m, ident_c), (sin_m, ident_s)))


def _pad_mla_heads(w, n_heads):
    lead = w.shape[:-1]
    w = w.reshape(lead + (n_heads, MLA_QK))
    w = jnp.pad(w, [(0, 0)] * len(lead) + [(0, 0), (0, MLA_PAD - MLA_QK)])
    return w.reshape(lead + (n_heads * MLA_PAD,))


def kernel(x, c, ctx, c_ctx, w_mod, b_mod, norm_mix, norm_ffn, w_in, mla_qa_norm, mla_w_uq, mla_kva_norm,
           mla_w_ukv, qn_att, kn_att, qn_mla, kn_mla, qn_win, kn_win, qn_na, kn_na, win_sink, na_rpb,
           w_branch, w_gate, b_gate, w_out, w_router, b_router, moe_w1, moe_w3, moe_w2):
    B, N, D = x.shape
    n_ctx = ctx.shape[1]
    L = w_mod.shape[0]
    R = B * N + B * n_ctx
    tm = 512
    tm_moe = 256
    tg = 512
    tn_merge = min(256, D // 2)
    tq = min(512, N)
    tk = min(512, N // 2)

    X = jnp.concatenate([x.reshape(B * N, D), ctx.reshape(B * n_ctx, D)], axis=0)
    cvec = jnp.zeros((8, D), F32).at[:B].set(c).at[B].set(c_ctx)
    mod = _mod_call(cvec, w_mod, b_mod, min(1024, D)).reshape(L, 8, 6, 1, D)
    tabs = _rope_tables(N, tm)

    wr_hi = w_router.astype(BF)
    wr_lo = (w_router - wr_hi.astype(F32)).astype(BF)
    wr_p = jnp.pad(jnp.stack([wr_hi, wr_lo]), ((0, 0), (0, 0), (0, LANES - N_EXPERTS)))
    br_p = jnp.pad(b_router.astype(F32), (0, LANES - N_EXPERTS)).reshape(1, LANES)

    for l in range(L):
        last = l == L - 1
        rows = B * N if last else R
        modl = mod[l]
        w_in_l = w_in[l]
        w_in_p = jnp.concatenate(
            [w_in_l[:, :KR_END], jnp.zeros((D, 64), F32), w_in_l[:, KR_END:]], axis=1).astype(BF)
        wuq_p = _pad_mla_heads(mla_w_uq[l], 4).astype(BF)
        wukv = mla_w_ukv[l].astype(BF)
        gains = [g.reshape(1, -1) for g in (
            qn_att[l], kn_att[l], qn_win[l], kn_win[l], qn_na[l], kn_na[l],
            _pad_mla_heads(qn_mla[l], 1), _pad_mla_heads(kn_mla[l], 1), mla_qa_norm[l], mla_kva_norm[l])]

        h, proj = _proj_call(X, norm_mix[l], modl, w_in_p, B, N, tm, P_TOTAL)
        (qa, ka, va, qm, km, vm, qw, kw, vw, qn, kn, vn) = _heads_call(proj, tabs, gains, wuq_p, wukv, B, N, tm)

        sink = win_sink[l].astype(F32)
        o_att = _flash_call(qa, ka, va, B, N, n_ctx, 2, HEAD_DIM, tq, tk, "dense_gqa")
        o_mla = _flash_call(qm, km, vm, B, N, n_ctx, 1, MLA_PAD, 2 * tq, tk, "latent_attn")
        o_win = _window_call(sink, qw, kw, vw, B, N, n_ctx, 2, tq)
        bias = _na_bias_tables(na_rpb[l], N // GRID_W)
        o_na = _na_call(qn, kn, vn, bias, B, N, n_ctx)
        outs = [o_att, o_mla, o_win, o_na]
        if not last:
            outs_c = [
                _ctx_attn_call(sink, qa, ka, va, B, N, n_ctx, 2, HEAD_DIM, HEAD_DIM, False, "ctx_dense_gqa"),
                _ctx_attn_call(sink, qm, km, vm, B, N, n_ctx, 1, MLA_PAD, HEAD_DIM, False, "ctx_latent_attn"),
                _ctx_attn_call(sink, qw, kw, vw, B, N, n_ctx, 2, HEAD_DIM, HEAD_DIM, True, "ctx_window_attn"),
                _ctx_attn_call(sink, qn, kn, vn, B, N, n_ctx, 1, HEAD_DIM, HEAD_DIM, False, "ctx_neighbourhood"),
            ]
            outs = [jnp.concatenate([a, b], axis=0) for a, b in zip(outs, outs_c)]

        X, h2, eidx, wts = _merge_call(
            X, h, outs, w_gate[l].astype(BF), b_gate[l].reshape(4, 1, D), w_branch[l].astype(BF),
            w_out[l].astype(BF), modl, norm_ffn[l], wr_p, br_p, rows, B, N, tm, tn_merge)

        pos, tile_expert, n_used, fill = _route_positions(eidx[:, :2], tg)
        n_slots = 2 * rows + N_EXPERTS * tg
        xs = _scatter_call(h2, pos, fill, n_slots, tm_moe, tg)
        ys = _experts_call(xs, tile_expert, n_used, moe_w1, moe_w3, moe_w2, l, tg)
        X = _combine_call(X, ys, pos, wts, modl, B, N, tm_moe)

    return X.reshape(B, N, D)
```

```python
import functools

import numpy as np
import jax
import jax.numpy as jnp
from jax import lax
from jax.experimental import pallas as pl
from jax.experimental.pallas import tpu as pltpu

BF = jnp.bfloat16
F32 = jnp.float32

GRID_W = 64
HEAD_DIM = 128
ROPE_THETA = 10000.0
EPS = 1e-6
NEG = -1e30
LOG2E = 1.4426950408889634
WINDOW = 128
NA_WIN_H = 8
NA_WIN_W = 16
N_EXPERTS = 16
N_GROUPS = 4
MLA_QK = 192
MLA_PAD = 256
LANES = 128
VMEM_LIMIT = 56 * 1024 * 1024
VMEM_LIMIT_MERGE = 62 * 1024 * 1024

P_AQ, P_AK, P_AV = 0, 512, 768
P_MCQ, P_MCKV, P_MKR = 1024, 1408, 1664
P_WQ, P_WK, P_WV = 1792, 2304, 2560
P_NQ, P_NK, P_NV = 2816, 3328, 3840
P_TOTAL = 4352
KR_END = 1728


def _cparams(sem, vmem_limit=VMEM_LIMIT):
    return pltpu.CompilerParams(dimension_semantics=sem, vmem_limit_bytes=vmem_limit)


def _sigmoid(z):
    return 1.0 / (1.0 + jnp.exp(-z))


def _dot(a, b):
    return jnp.dot(a, b, preferred_element_type=F32)


def _dot_nt(a, b):
    return lax.dot_general(a, b, (((1,), (1,)), ((), ())), preferred_element_type=F32)


def _mod_kernel(c_ref, w_ref, b_ref, o_ref):
    c = c_ref[...]
    s = c * _sigmoid(c)
    o_ref[...] = _dot(s.astype(BF), w_ref[...].astype(BF)) + b_ref[...]


def _mod_call(cvec, w_mod, b_mod, tn):
    L, D, D6 = w_mod.shape
    return pl.pallas_call(
        _mod_kernel,
        grid=(L, D6 // tn),
        in_specs=[
            pl.BlockSpec((8, D), lambda l, j: (0, 0)),
            pl.BlockSpec((None, D, tn), lambda l, j: (l, 0, j)),
            pl.BlockSpec((None, 1, tn), lambda l, j: (l, 0, j)),
        ],
        out_specs=pl.BlockSpec((None, 8, tn), lambda l, j: (l, 0, j)),
        out_shape=jax.ShapeDtypeStruct((L, 8, D6), F32),
        compiler_params=_cparams(("parallel", "parallel")),
        name="adaln_mod",
    )(cvec, w_mod, b_mod.reshape(L, 1, D6))


def _mod_spec(comp, tiles_per_batch, n_batch, D):
    return pl.BlockSpec(
        (None, None, 1, D),
        lambda i, *_: (jnp.minimum(i // tiles_per_batch, n_batch), comp, 0, 0))


def _proj_kernel(*refs, n_lat_tiles):
    if n_lat_tiles is None:
        x_ref, g_ref, sh_ref, sc_ref, w_ref, h_ref, o_ref, h_scr = refs
    else:
        xl_ref, xc_ref, g_ref, sh_ref, sc_ref, w_ref, h_ref, o_ref, h_scr = refs

    @pl.when(pl.program_id(1) == 0)
    def _():
        if n_lat_tiles is None:
            x = x_ref[...]
        else:
            x = jnp.where(pl.program_id(0) < n_lat_tiles, xl_ref[...], xc_ref[...])
        ms = jnp.mean(x * x, axis=-1, keepdims=True)
        xn = x * lax.rsqrt(ms + EPS) * g_ref[...]
        h = (xn * (1.0 + sc_ref[...]) + sh_ref[...]).astype(BF)
        h_scr[...] = h
        h_ref[...] = h

    o_ref[...] = _dot(h_scr[...], w_ref[...])


def _split_row_specs(arrays, tm, width, single_buffer=False):
    mode = dict(pipeline_mode=pl.Buffered(1)) if single_buffer else {}
    if len(arrays) == 1:
        return [pl.BlockSpec((tm, width), lambda i, j: (i, 0), **mode)], None
    n_lat_tiles = arrays[0].shape[0] // tm
    return [pl.BlockSpec((tm, width), lambda i, j: (jnp.minimum(i, n_lat_tiles - 1), 0), **mode),
            pl.BlockSpec((tm, width), lambda i, j: (jnp.maximum(i - n_lat_tiles, 0), 0),
                         pipeline_mode=pl.Buffered(1))], n_lat_tiles


def _proj_call(xs, norm_g, modl, w_in_p, n_batch, n_lat, tm, tn):
    D = xs[0].shape[1]
    R = sum(a.shape[0] for a in xs)
    P = w_in_p.shape[1]
    tpb = n_lat // tm
    if tn == P:
        w_spec = pl.BlockSpec((D, tn), lambda i, j: (0, j), pipeline_mode=pl.Buffered(1))
    else:
        w_spec = pl.BlockSpec((D, tn), lambda i, j: (0, j))
    row_spec = pl.BlockSpec((tm, D), lambda i, j: (i, 0))
    out_specs = [row_spec, pl.BlockSpec((tm, tn), lambda i, j: (i, j))]
    out_shape = [jax.ShapeDtypeStruct((R, D), BF), jax.ShapeDtypeStruct((R, P), F32)]
    x_specs, n_lat_tiles = _split_row_specs(xs, tm, D)
    return pl.pallas_call(
        functools.partial(_proj_kernel, n_lat_tiles=n_lat_tiles),
        grid=(R // tm, P // tn),
        in_specs=x_specs + [
            pl.BlockSpec((1, D), lambda i, j: (0, 0)),
            _mod_spec(0, tpb, n_batch, D),
            _mod_spec(1, tpb, n_batch, D),
            w_spec,
        ],
        out_specs=out_specs,
        out_shape=out_shape,
        scratch_shapes=[pltpu.VMEM((tm, D), BF)],
        compiler_params=_cparams(("parallel", "arbitrary")),
        name="norm_mod_proj",
    )(*xs, norm_g.reshape(1, D), modl, modl, w_in_p)


def _sumsq(x):
    xx = None
    for c in range(x.shape[1] // LANES):
        b = x[:, c * LANES:(c + 1) * LANES]
        xx = b * b if xx is None else xx + b * b
    hi = xx.astype(BF)
    lo = (xx - hi.astype(F32)).astype(BF)
    ones = jnp.ones((LANES, LANES), BF)
    return _dot(hi, ones) + _dot(lo, ones)


def _rms(x, g, n):
    r = lax.rsqrt(_sumsq(x) / n + EPS)
    blocks = [x[:, c * LANES:(c + 1) * LANES] * r * g[:, c * LANES:(c + 1) * LANES]
              for c in range(x.shape[1] // LANES)]
    return blocks[0] if len(blocks) == 1 else jnp.concatenate(blocks, axis=1)


def _rope(x, cos, sin_signed, half):
    lane = lax.broadcasted_iota(jnp.int32, x.shape, 1)
    first = (lane & (2 * half - 1)) < half
    xr = jnp.where(first, pltpu.roll(x, LANES - half, 1), pltpu.roll(x, half, 1))
    return x * cos + xr * sin_signed


def _heads_kernel(p_ref, ca_ref, sa_ref, cm_ref, sm_ref,
                  qa_g, ka_g, qw_g, kw_g, qn_g, kn_g, qm_g, km_g, cqn_g, ckvn_g, wuq_ref, wukv_ref,
                  qa_ref, ka_ref, va_ref, qm_ref, km_ref, vm_ref,
                  qw_ref, kw_ref, vw_ref, qn_ref, kn_ref, vn_ref):
    hd = HEAD_DIM
    sc128 = HEAD_DIM ** -0.5 * LOG2E
    sc192 = MLA_QK ** -0.5 * LOG2E
    ca, sa, cm, sm = ca_ref[...], sa_ref[...], cm_ref[...], sm_ref[...]

    def sl(off, h, w=hd):
        return p_ref[:, off + h * w: off + (h + 1) * w]

    ones = jnp.ones((p_ref.shape[0], hd), BF)

    def put_values(v_out, v_off, n_heads):
        for h in range(n_heads):
            v_out[:, 2 * h * hd:(2 * h + 1) * hd] = sl(v_off, h).astype(BF)
            v_out[:, (2 * h + 1) * hd:(2 * h + 2) * hd] = ones

    for (q_off, k_off, v_off, qg, kg, q_out, k_out, v_out) in (
            (P_AQ, P_AK, P_AV, qa_g, ka_g, qa_ref, ka_ref, va_ref),
            (P_WQ, P_WK, P_WV, qw_g, kw_g, qw_ref, kw_ref, vw_ref)):
        for h in range(4):
            q = _rope(_rms(sl(q_off, h), qg[...], hd), ca, sa, 32)
            q_out[:, h * hd:(h + 1) * hd] = (q * sc128).astype(BF)
        for h in range(2):
            k = _rope(_rms(sl(k_off, h), kg[...], hd), ca, sa, 32)
            k_out[:, h * hd:(h + 1) * hd] = k.astype(BF)
        put_values(v_out, v_off, 2)

    for h in range(4):
        qn_ref[:, h * hd:(h + 1) * hd] = (_rms(sl(P_NQ, h), qn_g[...], hd) * sc128).astype(BF)
        kn_ref[:, h * hd:(h + 1) * hd] = _rms(sl(P_NK, h), kn_g[...], hd).astype(BF)
    put_values(vn_ref, P_NV, 4)

    cq = _rms(p_ref[:, P_MCQ:P_MCQ + 384], cqn_g[...], 384)
    qf = _dot(cq.astype(BF), wuq_ref[...])
    ckv = _rms(p_ref[:, P_MCKV:P_MCKV + 256], ckvn_g[...], 256)
    kvf = _dot(ckv.astype(BF), wukv_ref[...])
    kr = p_ref[:, P_MKR:P_MKR + hd]
    kr_ss = _sumsq(kr)
    qg0, qg1 = qm_g[:, :hd], qm_g[:, hd:]
    kg0, kg1 = km_g[:, :hd], km_g[:, hd:]
    for h in range(4):
        q0 = qf[:, h * MLA_PAD: h * MLA_PAD + hd]
        q1 = qf[:, h * MLA_PAD + hd: (h + 1) * MLA_PAD]
        r = lax.rsqrt(_sumsq(qf[:, h * MLA_PAD:(h + 1) * MLA_PAD]) / MLA_QK + EPS)
        qm_ref[:, h * MLA_PAD: h * MLA_PAD + hd] = (q0 * r * qg0 * sc192).astype(BF)
        qm_ref[:, h * MLA_PAD + hd: (h + 1) * MLA_PAD] = (
            _rope(q1 * r * qg1, cm, sm, 16) * sc192).astype(BF)
        k0 = kvf[:, h * 2 * hd: h * 2 * hd + hd]
        r = lax.rsqrt((_sumsq(k0) + kr_ss) / MLA_QK + EPS)
        km_ref[:, h * MLA_PAD: h * MLA_PAD + hd] = (k0 * r * kg0).astype(BF)
        km_ref[:, h * MLA_PAD + hd: (h + 1) * MLA_PAD] = _rope(kr * r * kg1, cm, sm, 16).astype(BF)
        vm_ref[:, 2 * h * hd:(2 * h + 1) * hd] = kvf[:, h * 2 * hd + hd: (h + 1) * 2 * hd].astype(BF)
        vm_ref[:, (2 * h + 1) * hd:(2 * h + 2) * hd] = ones


def _heads_call(proj, tabs, gains, wuq_p, wukv, n_batch, n_lat, tm):
    R, P = proj.shape
    tpb = n_lat // tm
    tab_spec = pl.BlockSpec((tm, LANES), lambda i: (jnp.where(i < n_batch * tpb, i % tpb, tpb), 0))

    def full(a):
        return pl.BlockSpec(a.shape, lambda i: (0,) * a.ndim)

    widths = (512, 256, 512, 4 * MLA_PAD, 4 * MLA_PAD, 1024, 512, 256, 512, 512, 512, 1024)
    return pl.pallas_call(
        _heads_kernel,
        grid=(R // tm,),
        in_specs=[pl.BlockSpec((tm, P), lambda i: (i, 0))] + [tab_spec] * 4
        + [full(g) for g in gains] + [full(wuq_p), full(wukv)],
        out_specs=[pl.BlockSpec((tm, w), lambda i: (i, 0)) for w in widths],
        out_shape=[jax.ShapeDtypeStruct((R, w), BF) for w in widths],
        compiler_params=_cparams(("parallel",)),
        name="head_prep",
    )(proj, *tabs, *gains, wuq_p, wukv)


def _flash_kernel(q_ref, k_ref, v_ref, kc_ref, vc_ref, o_ref, q_scr, s_scr, m_scr, acc_scr,
                  *, G, d, dv, w, unroll):
    tq = q_ref.shape[0]
    n_blk = k_ref.shape[0] // w
    for g in range(G):
        q_scr[g * tq:(g + 1) * tq, :] = q_ref[:, g * d:(g + 1) * d]
    m_scr[...] = jnp.full_like(m_scr, NEG)
    acc_scr[...] = jnp.zeros_like(acc_scr)

    def scores(k):
        return _dot_nt(q_scr[...], k)

    def absorb(s, v):
        m_prev = m_scr[...]
        m_next = jnp.maximum(m_prev, s.max(axis=1, keepdims=True))
        alpha = jnp.exp2(m_prev - m_next)
        p = jnp.concatenate(
            [jnp.exp2(s[:, c * LANES:(c + 1) * LANES] - m_next) for c in range(s.shape[1] // LANES)],
            axis=1)
        pv = _dot(p.astype(BF), v)
        acc_scr[...] = jnp.concatenate([alpha, alpha], axis=1) * acc_scr[...] + pv
        m_scr[...] = m_next

    n_c = kc_ref.shape[0]
    s_scr[1, :, 0:n_c] = scores(kc_ref[...])
    s_scr[0] = scores(k_ref[0:w, :])
    absorb(s_scr[1, :, 0:n_c], vc_ref[...])

    def group(jj, carry):
        base = jj * (unroll * w)
        for u in range(unroll):
            cur = pl.multiple_of(base + u * w, w)
            nxt = pl.multiple_of(jnp.minimum(base + (u + 1) * w, (n_blk - 1) * w), w)
            s_scr[(u + 1) % 2] = scores(k_ref[pl.ds(nxt, w), :])
            absorb(s_scr[u % 2], v_ref[pl.ds(cur, w), :])
        return carry

    lax.fori_loop(0, n_blk // unroll, group, 0)

    acc = acc_scr[...]
    o = acc[:, :dv] / acc[:, dv:]
    for g in range(G):
        o_ref[:, g * dv:(g + 1) * dv] = o[g * tq:(g + 1) * tq].astype(BF)


def _flash_call(q, k, v, n_batch, n_lat, n_ctx, G, d, tq, w, name):
    dv = LANES
    hkv = k.shape[1] // d
    nq = n_lat // tq
    cb = (n_batch * n_lat) // n_ctx
    n_blk = n_lat // w
    unroll = next(u for u in (8, 4, 2) if n_blk % u == 0)
    assert n_ctx <= w
    return pl.pallas_call(
        functools.partial(_flash_kernel, G=G, d=d, dv=dv, w=w, unroll=unroll),
        grid=(n_batch, hkv, nq),
        in_specs=[
            pl.BlockSpec((tq, G * d), lambda b, h, i: (b * nq + i, h)),
            pl.BlockSpec((n_lat, d), lambda b, h, i: (b, h)),
            pl.BlockSpec((n_lat, 2 * dv), lambda b, h, i: (b, h)),
            pl.BlockSpec((n_ctx, d), lambda b, h, i: (cb + b, h)),
            pl.BlockSpec((n_ctx, 2 * dv), lambda b, h, i: (cb + b, h)),
        ],
        out_specs=pl.BlockSpec((tq, G * dv), lambda b, h, i: (b * nq + i, h)),
        out_shape=jax.ShapeDtypeStruct((n_batch * n_lat, hkv * G * dv), BF),
        scratch_shapes=[pltpu.VMEM((G * tq, d), BF), pltpu.VMEM((2, G * tq, w), F32),
                        pltpu.VMEM((G * tq, LANES), F32), pltpu.VMEM((G * tq, 2 * dv), F32)],
        compiler_params=_cparams(("parallel", "parallel", "parallel")),
        name=name,
    )(q, k, v, k, v)


def _window_kernel(sink_ref, q_ref, kp_ref, km_ref, kn_ref, vp_ref, vm_ref, vn_ref, kc_ref, vc_ref,
                   o_ref, s_scr, sc_scr, *, G):
    hd = HEAD_DIM
    i = pl.program_id(1)
    ni = pl.num_programs(1)
    tq = q_ref.shape[0]
    n_heads = q_ref.shape[1] // hd
    r = lax.broadcasted_iota(jnp.int32, (tq, tq + 2 * WINDOW), 0)
    c = lax.broadcasted_iota(jnp.int32, (tq, tq + 2 * WINDOW), 1)
    rel = c - WINDOW - r
    lo = jnp.where(i > 0, 0, WINDOW)
    hi = jnp.where(i < ni - 1, tq + 2 * WINDOW, tq + WINDOW)
    valid = (jnp.abs(rel) <= WINDOW) & (c >= lo) & (c < hi)

    def kv_cat(p_ref, m_ref, n_ref, kvh, w):
        cols = slice(kvh * w, (kvh + 1) * w)
        return jnp.concatenate([p_ref[:, cols], m_ref[:, cols], n_ref[:, cols]], axis=0)

    for qh in range(n_heads):
        kvh = qh // G
        q = q_ref[:, qh * hd:(qh + 1) * hd]
        s_scr[qh] = jnp.where(valid, _dot_nt(q, kv_cat(kp_ref, km_ref, kn_ref, kvh, hd)), NEG)
        sc_scr[qh] = _dot_nt(q, kc_ref[:, kvh * hd:(kvh + 1) * hd])
    for qh in range(n_heads):
        kvh = qh // G
        s = s_scr[qh]
        sc = sc_scr[qh]
        sk = sink_ref[qh] * LOG2E
        m = jnp.maximum(jnp.max(s, axis=-1, keepdims=True), jnp.max(sc, axis=-1, keepdims=True))
        m = jnp.maximum(m, sk)
        p = jnp.exp2(s - m)
        pc = jnp.exp2(sc - m)
        o = (_dot(p.astype(BF), kv_cat(vp_ref, vm_ref, vn_ref, kvh, 2 * hd))
             + _dot(pc.astype(BF), vc_ref[:, kvh * 2 * hd:(kvh + 1) * 2 * hd]))
        l = o[:, hd:] + jnp.exp2(sk - m)
        o_ref[:, qh * hd:(qh + 1) * hd] = (o[:, :hd] / l).astype(BF)


def _window_call(sink, q, k, v, n_batch, n_lat, n_ctx, G, tq):
    hd = HEAD_DIM
    hkv = k.shape[1] // hd
    nq = n_lat // tq
    bpt = tq // WINDOW
    nblk = k.shape[0] // WINDOW
    cb = (n_batch * n_lat) // n_ctx

    def prev(b, i):
        return (jnp.maximum((b * nq + i) * bpt - 1, 0), 0)

    def nxt(b, i):
        return (jnp.minimum((b * nq + i + 1) * bpt, nblk - 1), 0)

    def main(b, i):
        return (b * nq + i, 0)

    def ctxb(b, i):
        return (cb + b, 0)

    kw, vw, qw = hkv * hd, hkv * 2 * hd, hkv * G * hd
    return pl.pallas_call(
        functools.partial(_window_kernel, G=G),
        grid=(n_batch, nq),
        in_specs=[
            pl.BlockSpec(memory_space=pltpu.SMEM),
            pl.BlockSpec((tq, qw), main),
            pl.BlockSpec((WINDOW, kw), prev), pl.BlockSpec((tq, kw), main), pl.BlockSpec((WINDOW, kw), nxt),
            pl.BlockSpec((WINDOW, vw), prev), pl.BlockSpec((tq, vw), main), pl.BlockSpec((WINDOW, vw), nxt),
            pl.BlockSpec((n_ctx, kw), ctxb), pl.BlockSpec((n_ctx, vw), ctxb),
        ],
        out_specs=pl.BlockSpec((tq, qw), main),
        out_shape=jax.ShapeDtypeStruct((n_batch * n_lat, qw), BF),
        scratch_shapes=[pltpu.VMEM((hkv * G, tq, tq + 2 * WINDOW), F32), pltpu.VMEM((hkv * G, tq, n_ctx), F32)],
        compiler_params=_cparams(("parallel", "parallel")),
        name="window_attn",
    )(sink, q, k, k, k, v, v, v, k, v)


NA_TQ = 8 * GRID_W
NA_KB = 4 * GRID_W
NA_NKB = 4


def _na_kernel(q_ref, k0, k1, k2, k3, v0, v1, v2, v3, kc_ref, vc_ref, bias_ref, o_ref, s_scr, sc_scr):
    hd = HEAD_DIM
    n_heads = q_ref.shape[1] // hd

    def cat(refs, h, w):
        return jnp.concatenate([r[:, h * w:(h + 1) * w] for r in refs], axis=0)

    for h in range(n_heads):
        q = q_ref[:, h * hd:(h + 1) * hd]
        s_scr[h] = _dot_nt(q, cat((k0, k1, k2, k3), h, hd)) + bias_ref[h]
        sc_scr[h] = _dot_nt(q, kc_ref[:, h * hd:(h + 1) * hd])
    for h in range(n_heads):
        s = s_scr[h]
        sc = sc_scr[h]
        m = jnp.maximum(jnp.max(s, axis=-1, keepdims=True), jnp.max(sc, axis=-1, keepdims=True))
        p = jnp.exp2(s - m)
        pc = jnp.exp2(sc - m)
        o = (_dot(p.astype(BF), cat((v0, v1, v2, v3), h, 2 * hd))
             + _dot(pc.astype(BF), vc_ref[:, h * 2 * hd:(h + 1) * 2 * hd]))
        o_ref[:, h * hd:(h + 1) * hd] = (o[:, :hd] / o[:, hd:]).astype(BF)


def _na_bias_tables(rpb, n_rows):
    H = rpb.shape[0]
    J = n_rows // 8
    nb = n_rows // 4
    n_dy, n_dx = 2 * NA_WIN_H - 1, 2 * NA_WIN_W - 1
    cq = np.arange(GRID_W)
    kc = np.arange(GRID_W)
    cs = np.clip(cq - NA_WIN_W // 2, 0, GRID_W - NA_WIN_W)
    dx = kc[None, :] - cq[:, None] + NA_WIN_W - 1
    col_ok = (kc[None, :] >= cs[:, None]) & (kc[None, :] < cs[:, None] + NA_WIN_W)
    dx1h = np.zeros((n_dx, GRID_W * GRID_W), np.float32)
    dx1h[np.clip(dx, 0, n_dx - 1).reshape(-1), np.arange(GRID_W * GRID_W)] = 1.0
    blocks = jnp.einsum("hyx,xb->hyb", rpb.astype(F32) * LOG2E, jnp.asarray(dx1h),
                        precision=lax.Precision.HIGHEST)
    blocks = jnp.where(jnp.asarray(col_ok.reshape(-1))[None, None, :], blocks, NEG)
    blocks = jnp.concatenate([blocks, jnp.full((H, 1, GRID_W * GRID_W), NEG, F32)], axis=1)
    blocks = blocks.reshape(H, n_dy + 1, GRID_W, GRID_W)
    idx = np.zeros((3, 8, 4 * NA_NKB), np.int32)
    for v, jv in enumerate((0, min(1, J - 1), J - 1)):
        r = 8 * jv + np.arange(8)
        kb_un = 2 * jv - 1 + np.arange(NA_NKB)
        kb = np.clip(kb_un, 0, nb - 1)
        krow = (4 * kb[:, None] + np.arange(4)[None, :]).reshape(-1)
        krow_dup = np.repeat(kb != kb_un, 4)
        rs = np.clip(r - NA_WIN_H // 2, 0, n_rows - NA_WIN_H)
        row_ok = (krow[None, :] >= rs[:, None]) & (krow[None, :] < rs[:, None] + NA_WIN_H) & ~krow_dup[None, :]
        dy = krow[None, :] - r[:, None] + NA_WIN_H - 1
        idx[v] = np.where(row_ok, np.clip(dy, 0, n_dy - 1), n_dy)
    t = jnp.take(blocks, jnp.asarray(idx.reshape(-1)), axis=1)
    t = t.reshape(H, 3, 8, 4 * NA_NKB, GRID_W, GRID_W).transpose(0, 1, 2, 4, 3, 5)
    return t.reshape(H, 3, NA_TQ, NA_NKB * NA_KB)


def _na_call(q, k, v, bias, n_batch, n_lat, n_ctx):
    hd = HEAD_DIM
    H = q.shape[1] // hd
    J = n_lat // NA_TQ
    nb = n_lat // NA_KB
    cb = (n_batch * n_lat) // n_ctx

    def kblk(t):
        return lambda b, j: (b * nb + jnp.clip(2 * j - 1 + t, 0, nb - 1), 0)

    def qmap(b, j):
        return (b * J + j, 0)

    def ctxb(b, j):
        return (cb + b, 0)

    def bmap(b, j):
        return (0, jnp.where(j == 0, 0, jnp.where(j == J - 1, 2, 1)), 0, 0)

    n_keys = NA_NKB * NA_KB
    return pl.pallas_call(
        _na_kernel,
        grid=(n_batch, J),
        in_specs=[pl.BlockSpec((NA_TQ, H * hd), qmap)]
        + [pl.BlockSpec((NA_KB, H * hd), kblk(t)) for t in range(NA_NKB)]
        + [pl.BlockSpec((NA_KB, H * 2 * hd), kblk(t)) for t in range(NA_NKB)]
        + [pl.BlockSpec((n_ctx, H * hd), ctxb), pl.BlockSpec((n_ctx, H * 2 * hd), ctxb),
           pl.BlockSpec((H, None, NA_TQ, n_keys), bmap)],
        out_specs=pl.BlockSpec((NA_TQ, H * hd), qmap),
        out_shape=jax.ShapeDtypeStruct((n_batch * n_lat, H * hd), BF),
        scratch_shapes=[pltpu.VMEM((H, NA_TQ, n_keys), F32), pltpu.VMEM((H, NA_TQ, n_ctx), F32)],
        compiler_params=_cparams(("parallel", "parallel")),
        name="neighbourhood_attn",
    )(q, k, k, k, k, v, v, v, v, k, v, bias)


def _ctx_attn_kernel(sink_ref, q_ref, k_ref, v_ref, o_ref, *, use_sink):
    s = _dot_nt(q_ref[...], k_ref[...])
    m = jnp.max(s, axis=-1, keepdims=True)
    if use_sink:
        sk = sink_ref[pl.program_id(1)] * LOG2E
        m = jnp.maximum(m, sk)
    p = jnp.exp2(s - m)
    l = jnp.sum(p, axis=-1, keepdims=True)
    if use_sink:
        l = l + jnp.exp2(sk - m)
    o_ref[...] = (_dot(p.astype(BF), v_ref[...]) / l).astype(BF)


def _ctx_attn_call(sink, q, k, v, n_batch, n_lat, n_ctx, G, d, dv, use_sink, name):
    v_stride = 2
    H = q.shape[1] // d
    cb = (n_batch * n_lat) // n_ctx
    return pl.pallas_call(
        functools.partial(_ctx_attn_kernel, use_sink=use_sink),
        grid=(n_batch, H),
        in_specs=[
            pl.BlockSpec(memory_space=pltpu.SMEM),
            pl.BlockSpec((n_ctx, d), lambda b, h: (cb + b, h)),
            pl.BlockSpec((n_ctx, d), lambda b, h: (cb + b, h // G)),
            pl.BlockSpec((n_ctx, dv), lambda b, h: (cb + b, (h // G) * v_stride)),
        ],
        out_specs=pl.BlockSpec((n_ctx, dv), lambda b, h: (b, h)),
        out_shape=jax.ShapeDtypeStruct((n_batch * n_ctx, H * dv), BF),
        compiler_params=_cparams(("parallel", "parallel")),
        name=name,
    )(sink, q, k, v)


def _merge_kernel(*refs, n_lat_tiles):
    n_row_in = 5 if n_lat_tiles is None else 10
    row_refs = refs[:1 + n_row_in]
    (wg_ref, bg_ref, wb_ref, wo_ref, g1_ref, nf_ref, sh2_ref, sc2_ref, wr_ref, br_ref,
     xo_ref, h2_ref, ei_ref, wt_ref, y_scr) = refs[1 + n_row_in:]
    if n_lat_tiles is None:
        x_ref, h_ref = row_refs[0], row_refs[1]
        mixer_rows = [r.__getitem__ for r in row_refs[2:]]
        x_rows = x_ref.__getitem__
    else:
        h_ref = row_refs[2]
        is_lat = pl.program_id(0) < n_lat_tiles

        def pick(l_ref, c_ref):
            return lambda idx: jnp.where(is_lat, l_ref[idx], c_ref[idx])

        x_rows = pick(row_refs[0], row_refs[1])
        mixer_rows = [pick(row_refs[3 + 2 * i], row_refs[4 + 2 * i]) for i in range(4)]
    j = pl.program_id(1)

    @pl.when(j == 0)
    def _():
        y_scr[...] = jnp.zeros_like(y_scr)

    h = h_ref[...]
    acc = None
    for i, rows_of in enumerate(mixer_rows):
        gate = _sigmoid(_dot(h, wg_ref[i]) + bg_ref[i])
        y = gate * _dot(rows_of(...), wb_ref[i])
        acc = y if acc is None else acc + y
    y_scr[...] += _dot(acc.astype(BF), wo_ref[...])

    @pl.when(j == pl.num_programs(1) - 1)
    def _():
        x = x_rows(...) + g1_ref[...] * y_scr[...]
        xo_ref[...] = x
        ms = jnp.mean(x * x, axis=-1, keepdims=True)
        h2 = x * lax.rsqrt(ms + EPS) * nf_ref[...]
        h2 = h2 * (1.0 + sc2_ref[...]) + sh2_ref[...]
        h2_ref[...] = h2
        hi = h2.astype(BF)
        lo = (h2 - hi.astype(F32)).astype(BF)
        logits = _dot(hi, wr_ref[0]) + (_dot(lo, wr_ref[0]) + _dot(hi, wr_ref[1]))
        scores = _sigmoid(logits)
        biased = scores + br_ref[...]
        lane = lax.broadcasted_iota(jnp.int32, logits.shape, 1)
        lane_f = lane.astype(F32)
        ninf = -jnp.inf

        def top2(vals):
            t1 = jnp.max(vals, axis=-1, keepdims=True)
            i1 = jnp.min(jnp.where(vals == t1, lane_f, float(LANES)), axis=-1, keepdims=True)
            vals2 = jnp.where(lane_f == i1, ninf, vals)
            t2 = jnp.max(vals2, axis=-1, keepdims=True)
            i2 = jnp.min(jnp.where(vals2 == t2, lane_f, float(LANES)), axis=-1, keepdims=True)
            return t1, i1, t2, i2

        per = N_EXPERTS // N_GROUPS
        best, gi = None, None
        for g in range(N_GROUPS):
            ing = (lane >= g * per) & (lane < (g + 1) * per)
            t1, _, t2, _ = top2(jnp.where(ing, biased, ninf))
            gs = t1 + t2
            if best is None:
                best, gi = gs, jnp.zeros_like(gs)
            else:
                better = gs > best
                best = jnp.where(better, gs, best)
                gi = jnp.where(better, float(g), gi)
        lane_grp = (lane >> 2).astype(F32)
        vals = jnp.where(lane < N_EXPERTS, jnp.where(lane_grp == gi, biased, NEG), ninf)
        _, i1, _, i2 = top2(vals)
        w1 = jnp.sum(jnp.where(lane_f == i1, scores, 0.0), axis=-1, keepdims=True)
        w2 = jnp.sum(jnp.where(lane_f == i2, scores, 0.0), axis=-1, keepdims=True)
        den = w1 + w2
        ei_ref[...] = jnp.where(lane == 0, i1, jnp.where(lane == 1, i2, 0.0)).astype(jnp.int32)
        wt_ref[...] = jnp.where(lane == 0, w1 / den, jnp.where(lane == 1, w2 / den, 0.0))


def _merge_call(xs, h, outs, wg, bg, wb, wo, modl, norm_f, wr_p, br_p, rows, n_batch, n_lat, tm, tn):
    D = xs[0].shape[1]
    bw = wb.shape[1]
    tpb = n_lat // tm

    def rowmap(i, j):
        return (i, 0)

    x_specs, n_lat_tiles = _split_row_specs(xs, tm, D, single_buffer=True)
    o_specs, o_args = [], []
    for o in outs:
        assert len(o) == len(xs)
        o_specs += _split_row_specs(o, tm, bw)[0]
        o_args += list(o)
    return pl.pallas_call(
        functools.partial(_merge_kernel, n_lat_tiles=n_lat_tiles),
        grid=(rows // tm, D // tn),
        in_specs=x_specs + [pl.BlockSpec((tm, D), rowmap)] + o_specs + [
            pl.BlockSpec((4, D, tn), lambda i, j: (0, 0, j)),
            pl.BlockSpec((4, 1, tn), lambda i, j: (0, 0, j)),
            pl.BlockSpec((4, bw, tn), lambda i, j: (0, 0, j)),
            pl.BlockSpec((tn, D), lambda i, j: (j, 0)),
            _mod_spec(2, tpb, n_batch, D),
            pl.BlockSpec((1, D), lambda i, j: (0, 0)),
            _mod_spec(3, tpb, n_batch, D),
            _mod_spec(4, tpb, n_batch, D),
            pl.BlockSpec((2, D, LANES), lambda i, j: (0, 0, 0)),
            pl.BlockSpec((1, LANES), lambda i, j: (0, 0)),
        ],
        out_specs=[
            pl.BlockSpec((tm, D), rowmap),
            pl.BlockSpec((tm, D), rowmap),
            pl.BlockSpec((tm, LANES), rowmap),
            pl.BlockSpec((tm, LANES), rowmap),
        ],
        out_shape=[
            jax.ShapeDtypeStruct((rows, D), F32),
            jax.ShapeDtypeStruct((rows, D), F32),
            jax.ShapeDtypeStruct((rows, LANES), jnp.int32),
            jax.ShapeDtypeStruct((rows, LANES), F32),
        ],
        scratch_shapes=[pltpu.VMEM((tm, D), F32)],
        compiler_params=_cparams(("parallel", "arbitrary"), VMEM_LIMIT_MERGE),
        name="merge_residual_router",
    )(*xs, h, *o_args, wg, bg, wb, wo, modl, norm_f.reshape(1, D), modl, modl, wr_p, br_p)


def _row_dma_issue(n_rows, make_copy):
    def issue(r, carry):
        for k in range(2):
            make_copy(r, k).start()
        return carry

    lax.fori_loop(0, n_rows, issue, 0, unroll=8)


def _row_dma_drain(n_rows, make_copy):
    def drain(r, carry):
        for k in range(2):
            make_copy(r, k).wait()
        return carry

    lax.fori_loop(0, n_rows, drain, 0, unroll=8)


def _row_dma_loop(n_rows, make_copy):
    _row_dma_issue(n_rows, make_copy)
    _row_dma_drain(n_rows, make_copy)


def _scatter_kernel(fill_ref, pos_ref, h_ref, xs_ref, zbuf, sem, zsem):
    tg = zbuf.shape[0]

    @pl.when(pl.program_id(0) == 0)
    def _():
        zbuf[...] = jnp.zeros_like(zbuf)

        def fill(t, carry):
            @pl.when(fill_ref[t] != 0)
            def _():
                cp = pltpu.make_async_copy(zbuf, xs_ref.at[pl.ds(pl.multiple_of(t * tg, tg), tg), :], zsem)
                cp.start()
                cp.wait()
            return carry

        lax.fori_loop(0, fill_ref.shape[0], fill, 0)

    def make_copy(r, k):
        return pltpu.make_async_copy(h_ref.at[pl.ds(r, 1), :],
                                     xs_ref.at[pl.ds(pos_ref[2 * r + k], 1), :], sem)

    _row_dma_loop(h_ref.shape[0], make_copy)


def _scatter_call(h2, pos, fill, n_slots, tm, tg):
    rows, D = h2.shape
    grid_spec = pltpu.PrefetchScalarGridSpec(
        num_scalar_prefetch=1,
        grid=(rows // tm,),
        in_specs=[
            pl.BlockSpec((2 * tm,), lambda i, f: (i,), memory_space=pltpu.SMEM),
            pl.BlockSpec((tm, D), lambda i, f: (i, 0)),
        ],
        out_specs=pl.BlockSpec(memory_space=pl.ANY),
        scratch_shapes=[pltpu.VMEM((tg, D), F32), pltpu.SemaphoreType.DMA(()), pltpu.SemaphoreType.DMA(())],
    )
    return pl.pallas_call(
        _scatter_kernel,
        grid_spec=grid_spec,
        out_shape=jax.ShapeDtypeStruct((n_slots, D), F32),
        compiler_params=_cparams(("arbitrary",)),
        name="moe_scatter_rows",
    )(fill, pos, h2)


def _experts_kernel(te_ref, nu_ref, x_ref, w1_ref, w3_ref, w2_ref, y_ref, w1_scr, w3_scr, w2_scr):
    i = pl.program_id(0)

    @pl.when((i == 0) | (te_ref[i] != te_ref[jnp.maximum(i - 1, 0)]))
    def _():
        w1_scr[...] = w1_ref[...].astype(BF)
        w3_scr[...] = w3_ref[...].astype(BF)
        w2_scr[...] = w2_ref[...].astype(BF)

    @pl.when(i < nu_ref[0])
    def _():
        x = x_ref[...].astype(BF)
        a = _dot(x, w1_scr[...])
        b = _dot(x, w3_scr[...])
        hid = (a * _sigmoid(a)) * b
        y_ref[...] = _dot(hid.astype(BF), w2_scr[...])

    @pl.when(i >= nu_ref[0])
    def _():
        y_ref[...] = jnp.zeros_like(y_ref)


def _experts_call(xs, tile_expert, n_used, w1, w3, w2, layer, tg):
    S, D = xs.shape
    de = w1.shape[3]
    grid_spec = pltpu.PrefetchScalarGridSpec(
        num_scalar_prefetch=2,
        grid=(S // tg,),
        in_specs=[
            pl.BlockSpec((tg, D), lambda i, te, nu: (i, 0)),
            pl.BlockSpec((None, None, D, de), lambda i, te, nu: (layer, te[i], 0, 0)),
            pl.BlockSpec((None, None, D, de), lambda i, te, nu: (layer, te[i], 0, 0)),
            pl.BlockSpec((None, None, de, D), lambda i, te, nu: (layer, te[i], 0, 0)),
        ],
        out_specs=pl.BlockSpec((tg, D), lambda i, te, nu: (i, 0)),
        scratch_shapes=[pltpu.VMEM((D, de), BF), pltpu.VMEM((D, de), BF), pltpu.VMEM((de, D), BF)],
    )
    return pl.pallas_call(
        _experts_kernel,
        grid_spec=grid_spec,
        out_shape=jax.ShapeDtypeStruct((S, D), F32),
        compiler_params=_cparams(("arbitrary",)),
        name="moe_experts",
    )(tile_expert, n_used, xs, w1, w3, w2)


def _combine_kernel(pos_ref, pos_next_ref, x_ref, wt_ref, g2_ref, ys_ref, o_ref, buf, sem):
    i = pl.program_id(0)
    n = pl.num_programs(0)
    tm = x_ref.shape[0]
    slot = i % 2

    def gather(p_ref, s):
        def make_copy(r, k):
            return pltpu.make_async_copy(ys_ref.at[pl.ds(p_ref[2 * r + k], 1), :],
                                         buf.at[s, k, pl.ds(r, 1), :], sem.at[s])
        return make_copy

    @pl.when(i == 0)
    def _():
        _row_dma_issue(tm, gather(pos_ref, 0))

    @pl.when(i + 1 < n)
    def _():
        _row_dma_issue(tm, gather(pos_next_ref, 1 - slot))

    _row_dma_drain(tm, gather(pos_ref, slot))
    wt = wt_ref[...]
    moe = wt[:, 0:1] * buf[slot, 0] + wt[:, 1:2] * buf[slot, 1]
    o_ref[...] = x_ref[...] + g2_ref[...] * moe


def _combine_call(X, ys, pos, wts, modl, n_batch, n_lat, tm):
    rows, D = X.shape
    tpb = n_lat // tm
    n_steps = rows // tm
    return pl.pallas_call(
        _combine_kernel,
        grid=(n_steps,),
        in_specs=[
            pl.BlockSpec((2 * tm,), lambda i: (i,), memory_space=pltpu.SMEM),
            pl.BlockSpec((2 * tm,), lambda i: (jnp.minimum(i + 1, n_steps - 1),), memory_space=pltpu.SMEM),
            pl.BlockSpec((tm, D), lambda i: (i, 0)),
            pl.BlockSpec((tm, LANES), lambda i: (i, 0)),
            _mod_spec(5, tpb, n_batch, D),
            pl.BlockSpec(memory_space=pl.ANY),
        ],
        out_specs=pl.BlockSpec((tm, D), lambda i: (i, 0)),
        out_shape=jax.ShapeDtypeStruct((rows, D), F32),
        scratch_shapes=[pltpu.VMEM((2, 2, tm, D), F32), pltpu.SemaphoreType.DMA((2,))],
        compiler_params=_cparams(("arbitrary",)),
        name="moe_combine_residual",
    )(pos, pos, X, wts, modl, ys)


def _route_positions(eidx, tg):
    rows = eidx.shape[0]
    e_flat = eidx.reshape(-1)
    onehot = (e_flat[:, None] == jnp.arange(N_EXPERTS, dtype=jnp.int32)[None, :]).astype(jnp.int32)
    csum = jnp.cumsum(onehot, axis=0)
    counts = csum[-1]
    rank = jnp.sum((csum - onehot) * onehot, axis=1)
    padded = ((counts + tg - 1) // tg) * tg
    ends = jnp.cumsum(padded)
    offsets = ends - padded
    pos = jnp.sum(onehot * offsets[None, :], axis=1) + rank
    n_tiles = (2 * rows + N_EXPERTS * tg) // tg
    tile_start = jnp.arange(n_tiles, dtype=jnp.int32) * tg
    tile_expert = jnp.sum((tile_start[:, None] >= ends[None, :]).astype(jnp.int32), axis=1)
    n_used = (ends[-1] // tg).astype(jnp.int32)
    used = jnp.arange(n_tiles) < n_used
    te_1h = (tile_expert[:, None] == jnp.arange(N_EXPERTS, dtype=jnp.int32)[None, :]).astype(jnp.int32)
    real_end = jnp.sum(te_1h * (offsets + counts)[None, :], axis=1)
    fill = jnp.logical_not(used & (tile_start + tg <= real_end)).astype(jnp.int32)
    last_e = jnp.sum((((n_used - 1) * tg) >= ends).astype(jnp.int32))
    tile_expert = jnp.where(used, tile_expert, last_e).astype(jnp.int32)
    return pos.astype(jnp.int32), tile_expert, n_used.reshape(1), fill


def _rope_tables(n_lat, tm):
    t = jnp.arange(n_lat)
    rows, cols = t // GRID_W, t % GRID_W

    def tab(d2, pos):
        freqs = ROPE_THETA ** (-jnp.arange(d2, dtype=F32) / d2)
        ang = pos.astype(F32)[:, None] * freqs[None, :]
        return jnp.cos(ang), jnp.sin(ang)

    cr, sr = tab(32, rows)
    cc, sc = tab(32, cols)
    cos_a = jnp.concatenate([cr, cr, cc, cc], axis=-1)
    sin_a = jnp.concatenate([-sr, sr, -sc, sc], axis=-1)
    cr, sr = tab(16, rows)
    cc, sc = tab(16, cols)
    one, zero = jnp.ones((n_lat, 64), F32), jnp.zeros((n_lat, 64), F32)
    cos_m = jnp.concatenate([cr, cr, cc, cc, one], axis=-1)
    sin_m = jnp.concatenate([-sr, sr, -sc, sc, zero], axis=-1)
    ident_c, ident_s = jnp.ones((tm, LANES), F32), jnp.zeros((tm, LANES), F32)
    return tuple(jnp.concatenate([a, b], axis=0) for a, b in
                 ((cos_a, ident_c), (sin_a, ident_s), (cos_m, ident_c), (sin_m, ident_s)))


def _pad_mla_heads(w, n_heads):
    lead = w.shape[:-1]
    w = w.reshape(lead + (n_heads, MLA_QK))
    w = jnp.pad(w, [(0, 0)] * len(lead) + [(0, 0), (0, MLA_PAD - MLA_QK)])
    return w.reshape(lead + (n_heads * MLA_PAD,))


def kernel(x, c, ctx, c_ctx, w_mod, b_mod, norm_mix, norm_ffn, w_in, mla_qa_norm, mla_w_uq, mla_kva_norm,
           mla_w_ukv, qn_att, kn_att, qn_mla, kn_mla, qn_win, kn_win, qn_na, kn_na, win_sink, na_rpb,
           w_branch, w_gate, b_gate, w_out, w_router, b_router, moe_w1, moe_w3, moe_w2):
    B, N, D = x.shape
    n_ctx = ctx.shape[1]
    L = w_mod.shape[0]
    R = B * N + B * n_ctx
    tm = 512
    tm_moe = 256
    tg = 512
    tn_merge = min(256, D // 2)
    tq = min(512, N)
    tk = min(512, N // 2)

    x_lat, x_ctx = x.reshape(B * N, D), ctx.reshape(B * n_ctx, D)
    X = None
    cvec = jnp.zeros((8, D), F32).at[:B].set(c).at[B].set(c_ctx)
    mod = _mod_call(cvec, w_mod, b_mod, min(1024, D)).reshape(L, 8, 6, 1, D)
    tabs = _rope_tables(N, tm)

    wr_hi = w_router.astype(BF)
    wr_lo = (w_router - wr_hi.astype(F32)).astype(BF)
    wr_p = jnp.pad(jnp.stack([wr_hi, wr_lo]), ((0, 0), (0, 0), (0, LANES - N_EXPERTS)))
    br_p = jnp.pad(b_router.astype(F32), (0, LANES - N_EXPERTS)).reshape(1, LANES)

    for l in range(L):
        last = l == L - 1
        rows = B * N if last else R
        modl = mod[l]
        w_in_l = w_in[l]
        w_in_p = jnp.concatenate(
            [w_in_l[:, :KR_END], jnp.zeros((D, 64), F32), w_in_l[:, KR_END:]], axis=1).astype(BF)
        wuq_p = _pad_mla_heads(mla_w_uq[l], 4).astype(BF)
        wukv = mla_w_ukv[l].astype(BF)
        gains = [g.reshape(1, -1) for g in (
            qn_att[l], kn_att[l], qn_win[l], kn_win[l], qn_na[l], kn_na[l],
            _pad_mla_heads(qn_mla[l], 1), _pad_mla_heads(kn_mla[l], 1), mla_qa_norm[l], mla_kva_norm[l])]

        split = l == 0
        h, proj = _proj_call((x_lat, x_ctx) if split else (X,), norm_mix[l], modl, w_in_p, B, N, tm, P_TOTAL)
        (qa, ka, va, qm, km, vm, qw, kw, vw, qn, kn, vn) = _heads_call(proj, tabs, gains, wuq_p, wukv, B, N, tm)

        sink = win_sink[l].astype(F32)
        o_att = _flash_call(qa, ka, va, B, N, n_ctx, 2, HEAD_DIM, tq, tk, "dense_gqa")
        o_mla = _flash_call(qm, km, vm, B, N, n_ctx, 1, MLA_PAD, 2 * tq, tk, "latent_attn")
        o_win = _window_call(sink, qw, kw, vw, B, N, n_ctx, 2, tq)
        bias = _na_bias_tables(na_rpb[l], N // GRID_W)
        o_na = _na_call(qn, kn, vn, bias, B, N, n_ctx)
        outs = [(o,) for o in (o_att, o_mla, o_win, o_na)]
        if not last:
            outs_c = [
                _ctx_attn_call(sink, qa, ka, va, B, N, n_ctx, 2, HEAD_DIM, HEAD_DIM, False, "ctx_dense_gqa"),
                _ctx_attn_call(sink, qm, km, vm, B, N, n_ctx, 1, MLA_PAD, HEAD_DIM, False, "ctx_latent_attn"),
                _ctx_attn_call(sink, qw, kw, vw, B, N, n_ctx, 2, HEAD_DIM, HEAD_DIM, True, "ctx_window_attn"),
                _ctx_attn_call(sink, qn, kn, vn, B, N, n_ctx, 1, HEAD_DIM, HEAD_DIM, False, "ctx_neighbourhood"),
            ]
            if split:
                outs = [(a[0], b) for a, b in zip(outs, outs_c)]
            else:
                outs = [(jnp.concatenate([a[0], b], axis=0),) for a, b in zip(outs, outs_c)]
        if split:
            xs_res = (x_lat,) if last else (x_lat, x_ctx)
        else:
            xs_res = (X,)

        X, h2, eidx, wts = _merge_call(
            xs_res, h, outs, w_gate[l].astype(BF), b_gate[l].reshape(4, 1, D), w_branch[l].astype(BF),
            w_out[l].astype(BF), modl, norm_ffn[l], wr_p, br_p, rows, B, N, tm, tn_merge)

        pos, tile_expert, n_used, fill = _route_positions(eidx[:, :2], tg)
        n_slots = 2 * rows + N_EXPERTS * tg
        xs = _scatter_call(h2, pos, fill, n_slots, tm_moe, tg)
        ys = _experts_call(xs, tile_expert, n_used, moe_w1, moe_w3, moe_w2, l, tg)
        X = _combine_call(X, ys, pos, wts, modl, B, N, tm_moe)

    return X.reshape(B, N, D)
```

```python
import functools

import numpy as np
import jax
import jax.numpy as jnp
from jax import lax
from jax.experimental import pallas as pl
from jax.experimental.pallas import tpu as pltpu

BF = jnp.bfloat16
F32 = jnp.float32

GRID_W = 64
HEAD_DIM = 128
ROPE_THETA = 10000.0
EPS = 1e-6
NEG = -1e30
LOG2E = 1.4426950408889634
WINDOW = 128
NA_WIN_H = 8
NA_WIN_W = 16
N_EXPERTS = 16
N_GROUPS = 4
MLA_QK = 192
MLA_PAD = 256
LANES = 128
VMEM_LIMIT = 56 * 1024 * 1024
VMEM_LIMIT_MERGE = 62 * 1024 * 1024

P_AQ, P_AK, P_AV = 0, 512, 768
P_MCQ, P_MCKV, P_MKR = 1024, 1408, 1664
P_WQ, P_WK, P_WV = 1792, 2304, 2560
P_NQ, P_NK, P_NV = 2816, 3328, 3840
P_TOTAL = 4352
KR_END = 1728


def _cparams(sem, vmem_limit=VMEM_LIMIT):
    return pltpu.CompilerParams(dimension_semantics=sem, vmem_limit_bytes=vmem_limit)


def _sigmoid(z):
    return 1.0 / (1.0 + jnp.exp(-z))


def _dot(a, b):
    return jnp.dot(a, b, preferred_element_type=F32)


def _dot_nt(a, b):
    return lax.dot_general(a, b, (((1,), (1,)), ((), ())), preferred_element_type=F32)


def _mod_kernel(c_ref, w_ref, b_ref, o_ref):
    c = c_ref[...]
    s = c * _sigmoid(c)
    o_ref[...] = _dot(s.astype(BF), w_ref[...].astype(BF)) + b_ref[...]


def _mod_call(cvec, w_mod, b_mod, tn):
    L, D, D6 = w_mod.shape
    return pl.pallas_call(
        _mod_kernel,
        grid=(L, D6 // tn),
        in_specs=[
            pl.BlockSpec((8, D), lambda l, j: (0, 0)),
            pl.BlockSpec((None, D, tn), lambda l, j: (l, 0, j)),
            pl.BlockSpec((None, 1, tn), lambda l, j: (l, 0, j)),
        ],
        out_specs=pl.BlockSpec((None, 8, tn), lambda l, j: (l, 0, j)),
        out_shape=jax.ShapeDtypeStruct((L, 8, D6), F32),
        compiler_params=_cparams(("parallel", "parallel")),
        name="adaln_mod",
    )(cvec, w_mod, b_mod.reshape(L, 1, D6))


def _mod_spec(comp, tiles_per_batch, n_batch, D):
    return pl.BlockSpec(
        (None, None, 1, D),
        lambda i, *_: (jnp.minimum(i // tiles_per_batch, n_batch), comp, 0, 0))


def _proj_kernel(*refs, n_lat_tiles):
    if n_lat_tiles is None:
        x_ref, g_ref, sh_ref, sc_ref, w_ref, h_ref, o_ref, h_scr = refs
    else:
        xl_ref, xc_ref, g_ref, sh_ref, sc_ref, w_ref, h_ref, o_ref, h_scr = refs

    @pl.when(pl.program_id(1) == 0)
    def _():
        if n_lat_tiles is None:
            x = x_ref[...]
        else:
            x = jnp.where(pl.program_id(0) < n_lat_tiles, xl_ref[...], xc_ref[...])
        ms = jnp.mean(x * x, axis=-1, keepdims=True)
        xn = x * lax.rsqrt(ms + EPS) * g_ref[...]
        h = (xn * (1.0 + sc_ref[...]) + sh_ref[...]).astype(BF)
        h_scr[...] = h
        h_ref[...] = h

    o_ref[...] = _dot(h_scr[...], w_ref[...])


def _split_row_specs(arrays, tm, width, single_buffer=False):
    mode = dict(pipeline_mode=pl.Buffered(1)) if single_buffer else {}
    if len(arrays) == 1:
        return [pl.BlockSpec((tm, width), lambda i, j: (i, 0), **mode)], None
    n_lat_tiles = arrays[0].shape[0] // tm
    return [pl.BlockSpec((tm, width), lambda i, j: (jnp.minimum(i, n_lat_tiles - 1), 0), **mode),
            pl.BlockSpec((tm, width), lambda i, j: (jnp.maximum(i - n_lat_tiles, 0), 0),
                         pipeline_mode=pl.Buffered(1))], n_lat_tiles


def _proj_call(xs, norm_g, modl, w_in_p, n_batch, n_lat, tm, tn):
    D = xs[0].shape[1]
    R = sum(a.shape[0] for a in xs)
    P = w_in_p.shape[1]
    tpb = n_lat // tm
    if tn == P:
        w_spec = pl.BlockSpec((D, tn), lambda i, j: (0, j), pipeline_mode=pl.Buffered(1))
    else:
        w_spec = pl.BlockSpec((D, tn), lambda i, j: (0, j))
    row_spec = pl.BlockSpec((tm, D), lambda i, j: (i, 0))
    out_specs = [row_spec, pl.BlockSpec((tm, tn), lambda i, j: (i, j))]
    out_shape = [jax.ShapeDtypeStruct((R, D), BF), jax.ShapeDtypeStruct((R, P), F32)]
    x_specs, n_lat_tiles = _split_row_specs(xs, tm, D)
    return pl.pallas_call(
        functools.partial(_proj_kernel, n_lat_tiles=n_lat_tiles),
        grid=(R // tm, P // tn),
        in_specs=x_specs + [
            pl.BlockSpec((1, D), lambda i, j: (0, 0)),
            _mod_spec(0, tpb, n_batch, D),
            _mod_spec(1, tpb, n_batch, D),
            w_spec,
        ],
        out_specs=out_specs,
        out_shape=out_shape,
        scratch_shapes=[pltpu.VMEM((tm, D), BF)],
        compiler_params=_cparams(("parallel", "arbitrary")),
        name="norm_mod_proj",
    )(*xs, norm_g.reshape(1, D), modl, modl, w_in_p)


def _sumsq(x):
    xx = None
    for c in range(x.shape[1] // LANES):
        b = x[:, c * LANES:(c + 1) * LANES]
        xx = b * b if xx is None else xx + b * b
    hi = xx.astype(BF)
    lo = (xx - hi.astype(F32)).astype(BF)
    ones = jnp.ones((LANES, LANES), BF)
    return _dot(hi, ones) + _dot(lo, ones)


def _rms(x, g, n):
    r = lax.rsqrt(_sumsq(x) / n + EPS)
    blocks = [x[:, c * LANES:(c + 1) * LANES] * r * g[:, c * LANES:(c + 1) * LANES]
              for c in range(x.shape[1] // LANES)]
    return blocks[0] if len(blocks) == 1 else jnp.concatenate(blocks, axis=1)


def _rope(x, cos, sin_signed, half):
    lane = lax.broadcasted_iota(jnp.int32, x.shape, 1)
    first = (lane & (2 * half - 1)) < half
    xr = jnp.where(first, pltpu.roll(x, LANES - half, 1), pltpu.roll(x, half, 1))
    return x * cos + xr * sin_signed


def _heads_kernel(p_ref, ca_ref, sa_ref, cm_ref, sm_ref,
                  qa_g, ka_g, qw_g, kw_g, qn_g, kn_g, qm_g, km_g, cqn_g, ckvn_g, wuq_ref, wukv_ref,
                  qa_ref, ka_ref, va_ref, qm_ref, km_ref, vm_ref,
                  qw_ref, kw_ref, vw_ref, qn_ref, kn_ref, vn_ref):
    hd = HEAD_DIM
    sc128 = HEAD_DIM ** -0.5 * LOG2E
    sc192 = MLA_QK ** -0.5 * LOG2E
    ca, sa, cm, sm = ca_ref[...], sa_ref[...], cm_ref[...], sm_ref[...]

    def sl(off, h, w=hd):
        return p_ref[:, off + h * w: off + (h + 1) * w]

    ones = jnp.ones((p_ref.shape[0], hd), BF)

    def put_values(v_out, v_off, n_heads):
        for h in range(n_heads):
            v_out[:, 2 * h * hd:(2 * h + 1) * hd] = sl(v_off, h).astype(BF)
            v_out[:, (2 * h + 1) * hd:(2 * h + 2) * hd] = ones

    for (q_off, k_off, v_off, qg, kg, q_out, k_out, v_out) in (
            (P_AQ, P_AK, P_AV, qa_g, ka_g, qa_ref, ka_ref, va_ref),
            (P_WQ, P_WK, P_WV, qw_g, kw_g, qw_ref, kw_ref, vw_ref)):
        for h in range(4):
            q = _rope(_rms(sl(q_off, h), qg[...], hd), ca, sa, 32)
            q_out[:, h * hd:(h + 1) * hd] = (q * sc128).astype(BF)
        for h in range(2):
            k = _rope(_rms(sl(k_off, h), kg[...], hd), ca, sa, 32)
            k_out[:, h * hd:(h + 1) * hd] = k.astype(BF)
        put_values(v_out, v_off, 2)

    for h in range(4):
        qn_ref[:, h * hd:(h + 1) * hd] = (_rms(sl(P_NQ, h), qn_g[...], hd) * sc128).astype(BF)
        kn_ref[:, h * hd:(h + 1) * hd] = _rms(sl(P_NK, h), kn_g[...], hd).astype(BF)
    put_values(vn_ref, P_NV, 4)

    cq = _rms(p_ref[:, P_MCQ:P_MCQ + 384], cqn_g[...], 384)
    qf = _dot(cq.astype(BF), wuq_ref[...])
    ckv = _rms(p_ref[:, P_MCKV:P_MCKV + 256], ckvn_g[...], 256)
    kvf = _dot(ckv.astype(BF), wukv_ref[...])
    kr = p_ref[:, P_MKR:P_MKR + hd]
    kr_ss = _sumsq(kr)
    qg0, qg1 = qm_g[:, :hd], qm_g[:, hd:]
    kg0, kg1 = km_g[:, :hd], km_g[:, hd:]
    for h in range(4):
        q0 = qf[:, h * MLA_PAD: h * MLA_PAD + hd]
        q1 = qf[:, h * MLA_PAD + hd: (h + 1) * MLA_PAD]
        r = lax.rsqrt(_sumsq(qf[:, h * MLA_PAD:(h + 1) * MLA_PAD]) / MLA_QK + EPS)
        qm_ref[:, h * MLA_PAD: h * MLA_PAD + hd] = (q0 * r * qg0 * sc192).astype(BF)
        qm_ref[:, h * MLA_PAD + hd: (h + 1) * MLA_PAD] = (
            _rope(q1 * r * qg1, cm, sm, 16) * sc192).astype(BF)
        k0 = kvf[:, h * 2 * hd: h * 2 * hd + hd]
        r = lax.rsqrt((_sumsq(k0) + kr_ss) / MLA_QK + EPS)
        km_ref[:, h * MLA_PAD: h * MLA_PAD + hd] = (k0 * r * kg0).astype(BF)
        km_ref[:, h * MLA_PAD + hd: (h + 1) * MLA_PAD] = _rope(kr * r * kg1, cm, sm, 16).astype(BF)
        vm_ref[:, 2 * h * hd:(2 * h + 1) * hd] = kvf[:, h * 2 * hd + hd: (h + 1) * 2 * hd].astype(BF)
        vm_ref[:, (2 * h + 1) * hd:(2 * h + 2) * hd] = ones


def _heads_call(proj, tabs, gains, wuq_p, wukv, n_batch, n_lat, tm):
    R, P = proj.shape
    tpb = n_lat // tm
    tab_spec = pl.BlockSpec((tm, LANES), lambda i: (jnp.where(i < n_batch * tpb, i % tpb, tpb), 0))

    def full(a):
        return pl.BlockSpec(a.shape, lambda i: (0,) * a.ndim)

    widths = (512, 256, 512, 4 * MLA_PAD, 4 * MLA_PAD, 1024, 512, 256, 512, 512, 512, 1024)
    return pl.pallas_call(
        _heads_kernel,
        grid=(R // tm,),
        in_specs=[pl.BlockSpec((tm, P), lambda i: (i, 0))] + [tab_spec] * 4
        + [full(g) for g in gains] + [full(wuq_p), full(wukv)],
        out_specs=[pl.BlockSpec((tm, w), lambda i: (i, 0)) for w in widths],
        out_shape=[jax.ShapeDtypeStruct((R, w), BF) for w in widths],
        compiler_params=_cparams(("parallel",)),
        name="head_prep",
    )(proj, *tabs, *gains, wuq_p, wukv)


def _flash_kernel(q_ref, k_ref, v_ref, kc_ref, vc_ref, o_ref, q_scr, s_scr, m_scr, acc_scr,
                  *, G, d, dv, w, unroll):
    tq = q_ref.shape[0]
    n_blk = k_ref.shape[0] // w
    for g in range(G):
        q_scr[g * tq:(g + 1) * tq, :] = q_ref[:, g * d:(g + 1) * d]
    m_scr[...] = jnp.full_like(m_scr, NEG)
    acc_scr[...] = jnp.zeros_like(acc_scr)

    def scores(k):
        return _dot_nt(q_scr[...], k)

    def absorb(s, v):
        m_prev = m_scr[...]
        m_next = jnp.maximum(m_prev, s.max(axis=1, keepdims=True))
        alpha = jnp.exp2(m_prev - m_next)
        p = jnp.concatenate(
            [jnp.exp2(s[:, c * LANES:(c + 1) * LANES] - m_next) for c in range(s.shape[1] // LANES)],
            axis=1)
        pv = _dot(p.astype(BF), v)
        acc_scr[...] = jnp.concatenate([alpha, alpha], axis=1) * acc_scr[...] + pv
        m_scr[...] = m_next

    n_c = kc_ref.shape[0]
    s_scr[1, :, 0:n_c] = scores(kc_ref[...])
    s_scr[0] = scores(k_ref[0:w, :])
    absorb(s_scr[1, :, 0:n_c], vc_ref[...])

    def group(jj, carry):
        base = jj * (unroll * w)
        for u in range(unroll):
            cur = pl.multiple_of(base + u * w, w)
            nxt = pl.multiple_of(jnp.minimum(base + (u + 1) * w, (n_blk - 1) * w), w)
            s_scr[(u + 1) % 2] = scores(k_ref[pl.ds(nxt, w), :])
            absorb(s_scr[u % 2], v_ref[pl.ds(cur, w), :])
        return carry

    lax.fori_loop(0, n_blk // unroll, group, 0)

    acc = acc_scr[...]
    o = acc[:, :dv] / acc[:, dv:]
    for g in range(G):
        o_ref[:, g * dv:(g + 1) * dv] = o[g * tq:(g + 1) * tq].astype(BF)


def _flash_call(q, k, v, n_batch, n_lat, n_ctx, G, d, tq, w, name):
    dv = LANES
    hkv = k.shape[1] // d
    nq = n_lat // tq
    cb = (n_batch * n_lat) // n_ctx
    n_blk = n_lat // w
    unroll = next(u for u in (8, 4, 2) if n_blk % u == 0)
    assert n_ctx <= w
    return pl.pallas_call(
        functools.partial(_flash_kernel, G=G, d=d, dv=dv, w=w, unroll=unroll),
        grid=(n_batch, hkv, nq),
        in_specs=[
            pl.BlockSpec((tq, G * d), lambda b, h, i: (b * nq + i, h)),
            pl.BlockSpec((n_lat, d), lambda b, h, i: (b, h)),
            pl.BlockSpec((n_lat, 2 * dv), lambda b, h, i: (b, h)),
            pl.BlockSpec((n_ctx, d), lambda b, h, i: (cb + b, h)),
            pl.BlockSpec((n_ctx, 2 * dv), lambda b, h, i: (cb + b, h)),
        ],
        out_specs=pl.BlockSpec((tq, G * dv), lambda b, h, i: (b * nq + i, h)),
        out_shape=jax.ShapeDtypeStruct((n_batch * n_lat, hkv * G * dv), BF),
        scratch_shapes=[pltpu.VMEM((G * tq, d), BF), pltpu.VMEM((2, G * tq, w), F32),
                        pltpu.VMEM((G * tq, LANES), F32), pltpu.VMEM((G * tq, 2 * dv), F32)],
        compiler_params=_cparams(("parallel", "parallel", "parallel")),
        name=name,
    )(q, k, v, k, v)


def _window_kernel(sink_ref, q_ref, kp_ref, km_ref, kn_ref, vp_ref, vm_ref, vn_ref, kc_ref, vc_ref,
                   o_ref, s_scr, sc_scr, *, G):
    hd = HEAD_DIM
    i = pl.program_id(1)
    ni = pl.num_programs(1)
    tq = q_ref.shape[0]
    n_heads = q_ref.shape[1] // hd
    r = lax.broadcasted_iota(jnp.int32, (tq, tq + 2 * WINDOW), 0)
    c = lax.broadcasted_iota(jnp.int32, (tq, tq + 2 * WINDOW), 1)
    rel = c - WINDOW - r
    lo = jnp.where(i > 0, 0, WINDOW)
    hi = jnp.where(i < ni - 1, tq + 2 * WINDOW, tq + WINDOW)
    valid = (jnp.abs(rel) <= WINDOW) & (c >= lo) & (c < hi)

    def kv_cat(p_ref, m_ref, n_ref, kvh, w):
        cols = slice(kvh * w, (kvh + 1) * w)
        return jnp.concatenate([p_ref[:, cols], m_ref[:, cols], n_ref[:, cols]], axis=0)

    for qh in range(n_heads):
        kvh = qh // G
        q = q_ref[:, qh * hd:(qh + 1) * hd]
        s_scr[qh] = jnp.where(valid, _dot_nt(q, kv_cat(kp_ref, km_ref, kn_ref, kvh, hd)), NEG)
        sc_scr[qh] = _dot_nt(q, kc_ref[:, kvh * hd:(kvh + 1) * hd])
    for qh in range(n_heads):
        kvh = qh // G
        s = s_scr[qh]
        sc = sc_scr[qh]
        sk = sink_ref[qh] * LOG2E
        m = jnp.maximum(jnp.max(s, axis=-1, keepdims=True), jnp.max(sc, axis=-1, keepdims=True))
        m = jnp.maximum(m, sk)
        p = jnp.exp2(s - m)
        pc = jnp.exp2(sc - m)
        o = (_dot(p.astype(BF), kv_cat(vp_ref, vm_ref, vn_ref, kvh, 2 * hd))
             + _dot(pc.astype(BF), vc_ref[:, kvh * 2 * hd:(kvh + 1) * 2 * hd]))
        l = o[:, hd:] + jnp.exp2(sk - m)
        o_ref[:, qh * hd:(qh + 1) * hd] = (o[:, :hd] / l).astype(BF)


def _window_call(sink, q, k, v, n_batch, n_lat, n_ctx, G, tq):
    hd = HEAD_DIM
    hkv = k.shape[1] // hd
    nq = n_lat // tq
    bpt = tq // WINDOW
    nblk = k.shape[0] // WINDOW
    cb = (n_batch * n_lat) // n_ctx

    def prev(b, i):
        return (jnp.maximum((b * nq + i) * bpt - 1, 0), 0)

    def nxt(b, i):
        return (jnp.minimum((b * nq + i + 1) * bpt, nblk - 1), 0)

    def main(b, i):
        return (b * nq + i, 0)

    def ctxb(b, i):
        return (cb + b, 0)

    kw, vw, qw = hkv * hd, hkv * 2 * hd, hkv * G * hd
    return pl.pallas_call(
        functools.partial(_window_kernel, G=G),
        grid=(n_batch, nq),
        in_specs=[
            pl.BlockSpec(memory_space=pltpu.SMEM),
            pl.BlockSpec((tq, qw), main),
            pl.BlockSpec((WINDOW, kw), prev), pl.BlockSpec((tq, kw), main), pl.BlockSpec((WINDOW, kw), nxt),
            pl.BlockSpec((WINDOW, vw), prev), pl.BlockSpec((tq, vw), main), pl.BlockSpec((WINDOW, vw), nxt),
            pl.BlockSpec((n_ctx, kw), ctxb), pl.BlockSpec((n_ctx, vw), ctxb),
        ],
        out_specs=pl.BlockSpec((tq, qw), main),
        out_shape=jax.ShapeDtypeStruct((n_batch * n_lat, qw), BF),
        scratch_shapes=[pltpu.VMEM((hkv * G, tq, tq + 2 * WINDOW), F32), pltpu.VMEM((hkv * G, tq, n_ctx), F32)],
        compiler_params=_cparams(("parallel", "parallel")),
        name="window_attn",
    )(sink, q, k, k, k, v, v, v, k, v)


NA_TQ = 8 * GRID_W
NA_KB = 4 * GRID_W
NA_NKB = 4


def _na_kernel(q_ref, k0, k1, k2, k3, v0, v1, v2, v3, kc_ref, vc_ref, bias_ref, o_ref, s_scr, sc_scr):
    hd = HEAD_DIM
    n_heads = q_ref.shape[1] // hd

    def cat(refs, h, w):
        return jnp.concatenate([r[:, h * w:(h + 1) * w] for r in refs], axis=0)

    for h in range(n_heads):
        q = q_ref[:, h * hd:(h + 1) * hd]
        s_scr[h] = _dot_nt(q, cat((k0, k1, k2, k3), h, hd)) + bias_ref[h]
        sc_scr[h] = _dot_nt(q, kc_ref[:, h * hd:(h + 1) * hd])
    for h in range(n_heads):
        s = s_scr[h]
        sc = sc_scr[h]
        m = jnp.maximum(jnp.max(s, axis=-1, keepdims=True), jnp.max(sc, axis=-1, keepdims=True))
        p = jnp.exp2(s - m)
        pc = jnp.exp2(sc - m)
        o = (_dot(p.astype(BF), cat((v0, v1, v2, v3), h, 2 * hd))
             + _dot(pc.astype(BF), vc_ref[:, h * 2 * hd:(h + 1) * 2 * hd]))
        o_ref[:, h * hd:(h + 1) * hd] = (o[:, :hd] / o[:, hd:]).astype(BF)


def _na_bias_tables(rpb, n_rows):
    H = rpb.shape[0]
    J = n_rows // 8
    nb = n_rows // 4
    n_dy, n_dx = 2 * NA_WIN_H - 1, 2 * NA_WIN_W - 1
    cq = np.arange(GRID_W)
    kc = np.arange(GRID_W)
    cs = np.clip(cq - NA_WIN_W // 2, 0, GRID_W - NA_WIN_W)
    dx = kc[None, :] - cq[:, None] + NA_WIN_W - 1
    col_ok = (kc[None, :] >= cs[:, None]) & (kc[None, :] < cs[:, None] + NA_WIN_W)
    dx1h = np.zeros((n_dx, GRID_W * GRID_W), np.float32)
    dx1h[np.clip(dx, 0, n_dx - 1).reshape(-1), np.arange(GRID_W * GRID_W)] = 1.0
    blocks = jnp.einsum("hyx,xb->hyb", rpb.astype(F32) * LOG2E, jnp.asarray(dx1h),
                        precision=lax.Precision.HIGHEST)
    blocks = jnp.where(jnp.asarray(col_ok.reshape(-1))[None, None, :], blocks, NEG)
    blocks = jnp.concatenate([blocks, jnp.full((H, 1, GRID_W * GRID_W), NEG, F32)], axis=1)
    blocks = blocks.reshape(H, n_dy + 1, GRID_W, GRID_W)
    idx = np.zeros((3, 8, 4 * NA_NKB), np.int32)
    for v, jv in enumerate((0, min(1, J - 1), J - 1)):
        r = 8 * jv + np.arange(8)
        kb_un = 2 * jv - 1 + np.arange(NA_NKB)
        kb = np.clip(kb_un, 0, nb - 1)
        krow = (4 * kb[:, None] + np.arange(4)[None, :]).reshape(-1)
        krow_dup = np.repeat(kb != kb_un, 4)
        rs = np.clip(r - NA_WIN_H // 2, 0, n_rows - NA_WIN_H)
        row_ok = (krow[None, :] >= rs[:, None]) & (krow[None, :] < rs[:, None] + NA_WIN_H) & ~krow_dup[None, :]
        dy = krow[None, :] - r[:, None] + NA_WIN_H - 1
        idx[v] = np.where(row_ok, np.clip(dy, 0, n_dy - 1), n_dy)
    t = jnp.take(blocks, jnp.asarray(idx.reshape(-1)), axis=1)
    t = t.reshape(H, 3, 8, 4 * NA_NKB, GRID_W, GRID_W).transpose(0, 1, 2, 4, 3, 5)
    return t.reshape(H, 3, NA_TQ, NA_NKB * NA_KB)


def _na_call(q, k, v, bias, n_batch, n_lat, n_ctx):
    hd = HEAD_DIM
    H = q.shape[1] // hd
    J = n_lat // NA_TQ
    nb = n_lat // NA_KB
    cb = (n_batch * n_lat) // n_ctx

    def kblk(t):
        return lambda b, j: (b * nb + jnp.clip(2 * j - 1 + t, 0, nb - 1), 0)

    def qmap(b, j):
        return (b * J + j, 0)

    def ctxb(b, j):
        return (cb + b, 0)

    def bmap(b, j):
        return (0, jnp.where(j == 0, 0, jnp.where(j == J - 1, 2, 1)), 0, 0)

    n_keys = NA_NKB * NA_KB
    return pl.pallas_call(
        _na_kernel,
        grid=(n_batch, J),
        in_specs=[pl.BlockSpec((NA_TQ, H * hd), qmap)]
        + [pl.BlockSpec((NA_KB, H * hd), kblk(t)) for t in range(NA_NKB)]
        + [pl.BlockSpec((NA_KB, H * 2 * hd), kblk(t)) for t in range(NA_NKB)]
        + [pl.BlockSpec((n_ctx, H * hd), ctxb), pl.BlockSpec((n_ctx, H * 2 * hd), ctxb),
           pl.BlockSpec((H, None, NA_TQ, n_keys), bmap)],
        out_specs=pl.BlockSpec((NA_TQ, H * hd), qmap),
        out_shape=jax.ShapeDtypeStruct((n_batch * n_lat, H * hd), BF),
        scratch_shapes=[pltpu.VMEM((H, NA_TQ, n_keys), F32), pltpu.VMEM((H, NA_TQ, n_ctx), F32)],
        compiler_params=_cparams(("parallel", "parallel")),
        name="neighbourhood_attn",
    )(q, k, k, k, k, v, v, v, v, k, v, bias)


def _ctx_attn_kernel(sink_ref, q_ref, k_ref, v_ref, o_ref, *, use_sink):
    s = _dot_nt(q_ref[...], k_ref[...])
    m = jnp.max(s, axis=-1, keepdims=True)
    if use_sink:
        sk = sink_ref[pl.program_id(1)] * LOG2E
        m = jnp.maximum(m, sk)
    p = jnp.exp2(s - m)
    l = jnp.sum(p, axis=-1, keepdims=True)
    if use_sink:
        l = l + jnp.exp2(sk - m)
    o_ref[...] = (_dot(p.astype(BF), v_ref[...]) / l).astype(BF)


def _ctx_attn_call(sink, q, k, v, n_batch, n_lat, n_ctx, G, d, dv, use_sink, name):
    v_stride = 2
    H = q.shape[1] // d
    cb = (n_batch * n_lat) // n_ctx
    return pl.pallas_call(
        functools.partial(_ctx_attn_kernel, use_sink=use_sink),
        grid=(n_batch, H),
        in_specs=[
            pl.BlockSpec(memory_space=pltpu.SMEM),
            pl.BlockSpec((n_ctx, d), lambda b, h: (cb + b, h)),
            pl.BlockSpec((n_ctx, d), lambda b, h: (cb + b, h // G)),
            pl.BlockSpec((n_ctx, dv), lambda b, h: (cb + b, (h // G) * v_stride)),
        ],
        out_specs=pl.BlockSpec((n_ctx, dv), lambda b, h: (b, h)),
        out_shape=jax.ShapeDtypeStruct((n_batch * n_ctx, H * dv), BF),
        compiler_params=_cparams(("parallel", "parallel")),
        name=name,
    )(sink, q, k, v)


def _merge_kernel(*refs, n_lat_tiles):
    n_row_in = 5 if n_lat_tiles is None else 10
    row_refs = refs[:1 + n_row_in]
    (wg_ref, bg_ref, wb_ref, wo_ref, g1_ref, nf_ref, sh2_ref, sc2_ref, wr_ref, br_ref,
     xo_ref, h2_ref, ei_ref, wt_ref, y_scr) = refs[1 + n_row_in:]
    if n_lat_tiles is None:
        x_ref, h_ref = row_refs[0], row_refs[1]
        mixer_rows = [r.__getitem__ for r in row_refs[2:]]
        x_rows = x_ref.__getitem__
    else:
        h_ref = row_refs[2]
        is_lat = pl.program_id(0) < n_lat_tiles

        def pick(l_ref, c_ref):
            return lambda idx: jnp.where(is_lat, l_ref[idx], c_ref[idx])

        x_rows = pick(row_refs[0], row_refs[1])
        mixer_rows = [pick(row_refs[3 + 2 * i], row_refs[4 + 2 * i]) for i in range(4)]
    j = pl.program_id(1)

    @pl.when(j == 0)
    def _():
        y_scr[...] = jnp.zeros_like(y_scr)

    h = h_ref[...]
    acc = None
    for i, rows_of in enumerate(mixer_rows):
        gate = _sigmoid(_dot(h, wg_ref[i]) + bg_ref[i])
        y = gate * _dot(rows_of(...), wb_ref[i])
        acc = y if acc is None else acc + y
    y_scr[...] += _dot(acc.astype(BF), wo_ref[...])

    @pl.when(j == pl.num_programs(1) - 1)
    def _():
        x = x_rows(...) + g1_ref[...] * y_scr[...]
        xo_ref[...] = x
        ms = jnp.mean(x * x, axis=-1, keepdims=True)
        h2 = x * lax.rsqrt(ms + EPS) * nf_ref[...]
        h2 = h2 * (1.0 + sc2_ref[...]) + sh2_ref[...]
        h2_ref[...] = h2
        hi = h2.astype(BF)
        lo = (h2 - hi.astype(F32)).astype(BF)
        logits = _dot(hi, wr_ref[0]) + (_dot(lo, wr_ref[0]) + _dot(hi, wr_ref[1]))
        scores = _sigmoid(logits)
        biased = scores + br_ref[...]
        lane = lax.broadcasted_iota(jnp.int32, logits.shape, 1)
        lane_f = lane.astype(F32)
        ninf = -jnp.inf

        def top2(vals):
            t1 = jnp.max(vals, axis=-1, keepdims=True)
            i1 = jnp.min(jnp.where(vals == t1, lane_f, float(LANES)), axis=-1, keepdims=True)
            vals2 = jnp.where(lane_f == i1, ninf, vals)
            t2 = jnp.max(vals2, axis=-1, keepdims=True)
            i2 = jnp.min(jnp.where(vals2 == t2, lane_f, float(LANES)), axis=-1, keepdims=True)
            return t1, i1, t2, i2

        per = N_EXPERTS // N_GROUPS
        best, gi = None, None
        for g in range(N_GROUPS):
            ing = (lane >= g * per) & (lane < (g + 1) * per)
            t1, _, t2, _ = top2(jnp.where(ing, biased, ninf))
            gs = t1 + t2
            if best is None:
                best, gi = gs, jnp.zeros_like(gs)
            else:
                better = gs > best
                best = jnp.where(better, gs, best)
                gi = jnp.where(better, float(g), gi)
        lane_grp = (lane >> 2).astype(F32)
        vals = jnp.where(lane < N_EXPERTS, jnp.where(lane_grp == gi, biased, NEG), ninf)
        _, i1, _, i2 = top2(vals)
        w1 = jnp.sum(jnp.where(lane_f == i1, scores, 0.0), axis=-1, keepdims=True)
        w2 = jnp.sum(jnp.where(lane_f == i2, scores, 0.0), axis=-1, keepdims=True)
        den = w1 + w2
        ei_ref[...] = jnp.where(lane == 0, i1, jnp.where(lane == 1, i2, 0.0)).astype(jnp.int32)
        wt_ref[...] = jnp.where(lane == 0, w1 / den, jnp.where(lane == 1, w2 / den, 0.0))


def _merge_call(xs, h, outs, wg, bg, wb, wo, modl, norm_f, wr_p, br_p, rows, n_batch, n_lat, tm, tn):
    D = xs[0].shape[1]
    bw = wb.shape[1]
    tpb = n_lat // tm

    def rowmap(i, j):
        return (i, 0)

    x_specs, n_lat_tiles = _split_row_specs(xs, tm, D)
    o_specs, o_args = [], []
    for o in outs:
        assert len(o) == len(xs)
        o_specs += _split_row_specs(o, tm, bw)[0]
        o_args += list(o)
    return pl.pallas_call(
        functools.partial(_merge_kernel, n_lat_tiles=n_lat_tiles),
        grid=(rows // tm, D // tn),
        in_specs=x_specs + [pl.BlockSpec((tm, D), rowmap)] + o_specs + [
            pl.BlockSpec((4, D, tn), lambda i, j: (0, 0, j)),
            pl.BlockSpec((4, 1, tn), lambda i, j: (0, 0, j)),
            pl.BlockSpec((4, bw, tn), lambda i, j: (0, 0, j)),
            pl.BlockSpec((tn, D), lambda i, j: (j, 0)),
            _mod_spec(2, tpb, n_batch, D),
            pl.BlockSpec((1, D), lambda i, j: (0, 0)),
            _mod_spec(3, tpb, n_batch, D),
            _mod_spec(4, tpb, n_batch, D),
            pl.BlockSpec((2, D, LANES), lambda i, j: (0, 0, 0)),
            pl.BlockSpec((1, LANES), lambda i, j: (0, 0)),
        ],
        out_specs=[
            pl.BlockSpec((tm, D), rowmap),
            pl.BlockSpec((tm, D), rowmap),
            pl.BlockSpec((tm, LANES), rowmap),
            pl.BlockSpec((tm, LANES), rowmap),
        ],
        out_shape=[
            jax.ShapeDtypeStruct((rows, D), F32),
            jax.ShapeDtypeStruct((rows, D), F32),
            jax.ShapeDtypeStruct((rows, LANES), jnp.int32),
            jax.ShapeDtypeStruct((rows, LANES), F32),
        ],
        scratch_shapes=[pltpu.VMEM((tm, D), F32)],
        compiler_params=_cparams(("parallel", "arbitrary"), VMEM_LIMIT_MERGE),
        name="merge_residual_router",
    )(*xs, h, *o_args, wg, bg, wb, wo, modl, norm_f.reshape(1, D), modl, modl, wr_p, br_p)


def _row_dma_issue(n_rows, make_copy):
    def issue(r, carry):
        for k in range(2):
            make_copy(r, k).start()
        return carry

    lax.fori_loop(0, n_rows, issue, 0, unroll=8)


def _row_dma_drain(n_rows, make_copy):
    def drain(r, carry):
        for k in range(2):
            make_copy(r, k).wait()
        return carry

    lax.fori_loop(0, n_rows, drain, 0, unroll=8)


def _row_dma_loop(n_rows, make_copy):
    _row_dma_issue(n_rows, make_copy)
    _row_dma_drain(n_rows, make_copy)


def _scatter_kernel(fill_ref, pos_ref, h_ref, xs_ref, zbuf, sem, zsem):
    tg = zbuf.shape[0]

    @pl.when(pl.program_id(0) == 0)
    def _():
        zbuf[...] = jnp.zeros_like(zbuf)

        def fill(t, carry):
            @pl.when(fill_ref[t] != 0)
            def _():
                cp = pltpu.make_async_copy(zbuf, xs_ref.at[pl.ds(pl.multiple_of(t * tg, tg), tg), :], zsem)
                cp.start()
                cp.wait()
            return carry

        lax.fori_loop(0, fill_ref.shape[0], fill, 0)

    def make_copy(r, k):
        return pltpu.make_async_copy(h_ref.at[pl.ds(r, 1), :],
                                     xs_ref.at[pl.ds(pos_ref[2 * r + k], 1), :], sem)

    _row_dma_loop(h_ref.shape[0], make_copy)


def _scatter_call(h2, pos, fill, n_slots, tm, tg):
    rows, D = h2.shape
    grid_spec = pltpu.PrefetchScalarGridSpec(
        num_scalar_prefetch=1,
        grid=(rows // tm,),
        in_specs=[
            pl.BlockSpec((2 * tm,), lambda i, f: (i,), memory_space=pltpu.SMEM),
            pl.BlockSpec((tm, D), lambda i, f: (i, 0)),
        ],
        out_specs=pl.BlockSpec(memory_space=pl.ANY),
        scratch_shapes=[pltpu.VMEM((tg, D), F32), pltpu.SemaphoreType.DMA(()), pltpu.SemaphoreType.DMA(())],
    )
    return pl.pallas_call(
        _scatter_kernel,
        grid_spec=grid_spec,
        out_shape=jax.ShapeDtypeStruct((n_slots, D), F32),
        compiler_params=_cparams(("arbitrary",)),
        name="moe_scatter_rows",
    )(fill, pos, h2)


def _experts_kernel(te_ref, nu_ref, x_ref, w1_ref, w3_ref, w2_ref, y_ref, w1_scr, w3_scr, w2_scr):
    i = pl.program_id(0)

    @pl.when((i == 0) | (te_ref[i] != te_ref[jnp.maximum(i - 1, 0)]))
    def _():
        w1_scr[...] = w1_ref[...].astype(BF)
        w3_scr[...] = w3_ref[...].astype(BF)
        w2_scr[...] = w2_ref[...].astype(BF)

    @pl.when(i < nu_ref[0])
    def _():
        x = x_ref[...].astype(BF)
        a = _dot(x, w1_scr[...])
        b = _dot(x, w3_scr[...])
        hid = (a * _sigmoid(a)) * b
        y_ref[...] = _dot(hid.astype(BF), w2_scr[...])

    @pl.when(i >= nu_ref[0])
    def _():
        y_ref[...] = jnp.zeros_like(y_ref)


def _experts_call(xs, tile_expert, n_used, w1, w3, w2, layer, tg):
    S, D = xs.shape
    de = w1.shape[3]
    grid_spec = pltpu.PrefetchScalarGridSpec(
        num_scalar_prefetch=2,
        grid=(S // tg,),
        in_specs=[
            pl.BlockSpec((tg, D), lambda i, te, nu: (i, 0)),
            pl.BlockSpec((None, None, D, de), lambda i, te, nu: (layer, te[i], 0, 0)),
            pl.BlockSpec((None, None, D, de), lambda i, te, nu: (layer, te[i], 0, 0)),
            pl.BlockSpec((None, None, de, D), lambda i, te, nu: (layer, te[i], 0, 0)),
        ],
        out_specs=pl.BlockSpec((tg, D), lambda i, te, nu: (i, 0)),
        scratch_shapes=[pltpu.VMEM((D, de), BF), pltpu.VMEM((D, de), BF), pltpu.VMEM((de, D), BF)],
    )
    return pl.pallas_call(
        _experts_kernel,
        grid_spec=grid_spec,
        out_shape=jax.ShapeDtypeStruct((S, D), F32),
        compiler_params=_cparams(("arbitrary",)),
        name="moe_experts",
    )(tile_expert, n_used, xs, w1, w3, w2)


def _combine_kernel(pos_ref, pos_next_ref, x_ref, wt_ref, g2_ref, ys_ref, o_ref, buf, sem):
    i = pl.program_id(0)
    n = pl.num_programs(0)
    tm = x_ref.shape[0]
    slot = i % 2

    def gather(p_ref, s):
        def make_copy(r, k):
            return pltpu.make_async_copy(ys_ref.at[pl.ds(p_ref[2 * r + k], 1), :],
                                         buf.at[s, k, pl.ds(r, 1), :], sem.at[s])
        return make_copy

    @pl.when(i == 0)
    def _():
        _row_dma_issue(tm, gather(pos_ref, 0))

    @pl.when(i + 1 < n)
    def _():
        _row_dma_issue(tm, gather(pos_next_ref, 1 - slot))

    _row_dma_drain(tm, gather(pos_ref, slot))
    wt = wt_ref[...]
    moe = wt[:, 0:1] * buf[slot, 0] + wt[:, 1:2] * buf[slot, 1]
    o_ref[...] = x_ref[...] + g2_ref[...] * moe


def _combine_call(X, ys, pos, wts, modl, n_batch, n_lat, tm):
    rows, D = X.shape
    tpb = n_lat // tm
    n_steps = rows // tm
    return pl.pallas_call(
        _combine_kernel,
        grid=(n_steps,),
        in_specs=[
            pl.BlockSpec((2 * tm,), lambda i: (i,), memory_space=pltpu.SMEM),
            pl.BlockSpec((2 * tm,), lambda i: (jnp.minimum(i + 1, n_steps - 1),), memory_space=pltpu.SMEM),
            pl.BlockSpec((tm, D), lambda i: (i, 0)),
            pl.BlockSpec((tm, LANES), lambda i: (i, 0)),
            _mod_spec(5, tpb, n_batch, D),
            pl.BlockSpec(memory_space=pl.ANY),
        ],
        out_specs=pl.BlockSpec((tm, D), lambda i: (i, 0)),
        out_shape=jax.ShapeDtypeStruct((rows, D), F32),
        scratch_shapes=[pltpu.VMEM((2, 2, tm, D), F32), pltpu.SemaphoreType.DMA((2,))],
        compiler_params=_cparams(("arbitrary",)),
        name="moe_combine_residual",
    )(pos, pos, X, wts, modl, ys)


def _route_positions(eidx, tg):
    rows = eidx.shape[0]
    e_flat = eidx.reshape(-1)
    onehot = (e_flat[:, None] == jnp.arange(N_EXPERTS, dtype=jnp.int32)[None, :]).astype(jnp.int32)
    csum = jnp.cumsum(onehot, axis=0)
    counts = csum[-1]
    rank = jnp.sum((csum - onehot) * onehot, axis=1)
    padded = ((counts + tg - 1) // tg) * tg
    ends = jnp.cumsum(padded)
    offsets = ends - padded
    pos = jnp.sum(onehot * offsets[None, :], axis=1) + rank
    n_tiles = (2 * rows + N_EXPERTS * tg) // tg
    tile_start = jnp.arange(n_tiles, dtype=jnp.int32) * tg
    tile_expert = jnp.sum((tile_start[:, None] >= ends[None, :]).astype(jnp.int32), axis=1)
    n_used = (ends[-1] // tg).astype(jnp.int32)
    used = jnp.arange(n_tiles) < n_used
    te_1h = (tile_expert[:, None] == jnp.arange(N_EXPERTS, dtype=jnp.int32)[None, :]).astype(jnp.int32)
    real_end = jnp.sum(te_1h * (offsets + counts)[None, :], axis=1)
    fill = jnp.logical_not(used & (tile_start + tg <= real_end)).astype(jnp.int32)
    last_e = jnp.sum((((n_used - 1) * tg) >= ends).astype(jnp.int32))
    tile_expert = jnp.where(used, tile_expert, last_e).astype(jnp.int32)
    return pos.astype(jnp.int32), tile_expert, n_used.reshape(1), fill


def _rope_tables(n_lat, tm):
    n_rows = n_lat // GRID_W

    def tab(d2, axis):
        freqs = np.float32(ROPE_THETA) ** (-np.arange(d2, dtype=np.float32) / np.float32(d2))
        pos = np.arange(n_rows if axis == 0 else GRID_W, dtype=np.float32)
        ang = pos[:, None] * freqs[None, :]
        shape = (n_rows, GRID_W, d2)
        small = [np.cos(ang).astype(np.float32), np.sin(ang).astype(np.float32)]
        full = [jnp.broadcast_to(jnp.asarray(a)[:, None, :] if axis == 0 else jnp.asarray(a)[None, :, :], shape)
                for a in small]
        return tuple(a.reshape(n_lat, d2) for a in full)

    cr, sr = tab(32, 0)
    cc, sc = tab(32, 1)
    cos_a = jnp.concatenate([cr, cr, cc, cc], axis=-1)
    sin_a = jnp.concatenate([-sr, sr, -sc, sc], axis=-1)
    cr, sr = tab(16, 0)
    cc, sc = tab(16, 1)
    one, zero = jnp.ones((n_lat, 64), F32), jnp.zeros((n_lat, 64), F32)
    cos_m = jnp.concatenate([cr, cr, cc, cc, one], axis=-1)
    sin_m = jnp.concatenate([-sr, sr, -sc, sc, zero], axis=-1)
    ident_c, ident_s = jnp.ones((tm, LANES), F32), jnp.zeros((tm, LANES), F32)
    return tuple(jnp.concatenate([a, b], axis=0) for a, b in
                 ((cos_a, ident_c), (sin_a, ident_s), (cos_m, ident_c), (sin_m, ident_s)))


def _pad_mla_heads(w, n_heads):
    lead = w.shape[:-1]
    w = w.reshape(lead + (n_heads, MLA_QK))
    w = jnp.pad(w, [(0, 0)] * len(lead) + [(0, 0), (0, MLA_PAD - MLA_QK)])
    return w.reshape(lead + (n_heads * MLA_PAD,))


def kernel(x, c, ctx, c_ctx, w_mod, b_mod, norm_mix, norm_ffn, w_in, mla_qa_norm, mla_w_uq, mla_kva_norm,
           mla_w_ukv, qn_att, kn_att, qn_mla, kn_mla, qn_win, kn_win, qn_na, kn_na, win_sink, na_rpb,
           w_branch, w_gate, b_gate, w_out, w_router, b_router, moe_w1, moe_w3, moe_w2):
    B, N, D = x.shape
    n_ctx = ctx.shape[1]
    L = w_mod.shape[0]
    R = B * N + B * n_ctx
    tm = 512
    tm_moe = 256
    tg = 512
    tn_merge = min(256, D // 2)
    tq = min(512, N)
    tk = min(512, N // 2)

    x_lat, x_ctx = x.reshape(B * N, D), ctx.reshape(B * n_ctx, D)
    X = None
    cvec = jnp.zeros((8, D), F32).at[:B].set(c).at[B].set(c_ctx)
    mod = _mod_call(cvec, w_mod, b_mod, min(1024, D)).reshape(L, 8, 6, 1, D)
    tabs = _rope_tables(N, tm)

    wr_hi = w_router.astype(BF)
    wr_lo = (w_router - wr_hi.astype(F32)).astype(BF)
    wr_p = jnp.pad(jnp.stack([wr_hi, wr_lo]), ((0, 0), (0, 0), (0, LANES - N_EXPERTS)))
    br_p = jnp.pad(b_router.astype(F32), (0, LANES - N_EXPERTS)).reshape(1, LANES)

    for l in range(L):
        last = l == L - 1
        rows = B * N if last else R
        modl = mod[l]
        w_in_l = w_in[l]
        w_in_p = jnp.concatenate(
            [w_in_l[:, :KR_END], jnp.zeros((D, 64), F32), w_in_l[:, KR_END:]], axis=1).astype(BF)
        wuq_p = _pad_mla_heads(mla_w_uq[l], 4).astype(BF)
        wukv = mla_w_ukv[l].astype(BF)
        gains = [g.reshape(1, -1) for g in (
            qn_att[l], kn_att[l], qn_win[l], kn_win[l], qn_na[l], kn_na[l],
            _pad_mla_heads(qn_mla[l], 1), _pad_mla_heads(kn_mla[l], 1), mla_qa_norm[l], mla_kva_norm[l])]

        split = l == 0
        h, proj = _proj_call((x_lat, x_ctx) if split else (X,), norm_mix[l], modl, w_in_p, B, N, tm, P_TOTAL)
        (qa, ka, va, qm, km, vm, qw, kw, vw, qn, kn, vn) = _heads_call(proj, tabs, gains, wuq_p, wukv, B, N, tm)

        sink = win_sink[l].astype(F32)
        o_att = _flash_call(qa, ka, va, B, N, n_ctx, 2, HEAD_DIM, tq, tk, "dense_gqa")
        o_mla = _flash_call(qm, km, vm, B, N, n_ctx, 1, MLA_PAD, 2 * tq, tk, "latent_attn")
        o_win = _window_call(sink, qw, kw, vw, B, N, n_ctx, 2, tq)
        bias = _na_bias_tables(na_rpb[l], N // GRID_W)
        o_na = _na_call(qn, kn, vn, bias, B, N, n_ctx)
        outs = [(o,) for o in (o_att, o_mla, o_win, o_na)]
        if not last:
            outs_c = [
                _ctx_attn_call(sink, qa, ka, va, B, N, n_ctx, 2, HEAD_DIM, HEAD_DIM, False, "ctx_dense_gqa"),
                _ctx_attn_call(sink, qm, km, vm, B, N, n_ctx, 1, MLA_PAD, HEAD_DIM, False, "ctx_latent_attn"),
                _ctx_attn_call(sink, qw, kw, vw, B, N, n_ctx, 2, HEAD_DIM, HEAD_DIM, True, "ctx_window_attn"),
                _ctx_attn_call(sink, qn, kn, vn, B, N, n_ctx, 1, HEAD_DIM, HEAD_DIM, False, "ctx_neighbourhood"),
            ]
            if split:
                outs = [(a[0], b) for a, b in zip(outs, outs_c)]
            else:
                outs = [(jnp.concatenate([a[0], b], axis=0),) for a, b in zip(outs, outs_c)]
        if split:
            xs_res = (x_lat,) if last else (x_lat, x_ctx)
        else:
            xs_res = (X,)

        X, h2, eidx, wts = _merge_call(
            xs_res, h, outs, w_gate[l].astype(BF), b_gate[l].reshape(4, 1, D), w_branch[l].astype(BF),
            w_out[l].astype(BF), modl, norm_ffn[l], wr_p, br_p, rows, B, N, tm, tn_merge)

        pos, tile_expert, n_used, fill = _route_positions(eidx[:, :2], tg)
        n_slots = 2 * rows + N_EXPERTS * tg
        xs = _scatter_call(h2, pos, fill, n_slots, tm_moe, tg)
        ys = _experts_call(xs, tile_expert, n_used, moe_w1, moe_w3, moe_w2, l, tg)
        X = _combine_call(X, ys, pos, wts, modl, B, N, tm_moe)

    return X.reshape(B, N, D)
```

```python
import functools

import numpy as np
import jax
import jax.numpy as jnp
from jax import lax
from jax.experimental import pallas as pl
from jax.experimental.pallas import tpu as pltpu

BF = jnp.bfloat16
F32 = jnp.float32

GRID_W = 64
HEAD_DIM = 128
ROPE_THETA = 10000.0
EPS = 1e-6
NEG = -1e30
LOG2E = 1.4426950408889634
WINDOW = 128
NA_WIN_H = 8
NA_WIN_W = 16
N_EXPERTS = 16
N_GROUPS = 4
MLA_QK = 192
MLA_PAD = 256
LANES = 128
VMEM_LIMIT = 56 * 1024 * 1024
VMEM_LIMIT_MERGE = 62 * 1024 * 1024

P_AQ, P_AK, P_AV = 0, 512, 768
P_MCQ, P_MCKV, P_MKR = 1024, 1408, 1664
P_WQ, P_WK, P_WV = 1792, 2304, 2560
P_NQ, P_NK, P_NV = 2816, 3328, 3840
P_TOTAL = 4352
KR_END = 1728


def _cparams(sem, vmem_limit=VMEM_LIMIT):
    return pltpu.CompilerParams(dimension_semantics=sem, vmem_limit_bytes=vmem_limit)


def _sigmoid(z):
    return 1.0 / (1.0 + jnp.exp(-z))


def _dot(a, b):
    return jnp.dot(a, b, preferred_element_type=F32)


def _dot_nt(a, b):
    return lax.dot_general(a, b, (((1,), (1,)), ((), ())), preferred_element_type=F32)


def _mod_kernel(c_ref, w_ref, b_ref, o_ref):
    c = c_ref[...]
    s = c * _sigmoid(c)
    o_ref[...] = _dot(s.astype(BF), w_ref[...].astype(BF)) + b_ref[...]


def _mod_call(cvec, w_mod, b_mod, tn):
    L, D, D6 = w_mod.shape
    return pl.pallas_call(
        _mod_kernel,
        grid=(L, D6 // tn),
        in_specs=[
            pl.BlockSpec((8, D), lambda l, j: (0, 0)),
            pl.BlockSpec((None, D, tn), lambda l, j: (l, 0, j)),
            pl.BlockSpec((None, 1, tn), lambda l, j: (l, 0, j)),
        ],
        out_specs=pl.BlockSpec((None, 8, tn), lambda l, j: (l, 0, j)),
        out_shape=jax.ShapeDtypeStruct((L, 8, D6), F32),
        compiler_params=_cparams(("parallel", "parallel")),
        name="adaln_mod",
    )(cvec, w_mod, b_mod.reshape(L, 1, D6))


def _mod_spec(comp, tiles_per_batch, n_batch, D):
    return pl.BlockSpec(
        (None, None, 1, D),
        lambda i, *_: (jnp.minimum(i // tiles_per_batch, n_batch), comp, 0, 0))


def _proj_kernel(*refs, n_lat_tiles):
    if n_lat_tiles is None:
        x_ref, g_ref, sh_ref, sc_ref, w_ref, h_ref, o_ref, h_scr = refs
    else:
        xl_ref, xc_ref, g_ref, sh_ref, sc_ref, w_ref, h_ref, o_ref, h_scr = refs

    @pl.when(pl.program_id(1) == 0)
    def _():
        if n_lat_tiles is None:
            x = x_ref[...]
        else:
            x = jnp.where(pl.program_id(0) < n_lat_tiles, xl_ref[...], xc_ref[...])
        ms = jnp.mean(x * x, axis=-1, keepdims=True)
        xn = x * lax.rsqrt(ms + EPS) * g_ref[...]
        h = (xn * (1.0 + sc_ref[...]) + sh_ref[...]).astype(BF)
        h_scr[...] = h
        h_ref[...] = h

    o_ref[...] = _dot(h_scr[...], w_ref[...])


def _split_row_specs(arrays, tm, width, single_buffer=False):
    mode = dict(pipeline_mode=pl.Buffered(1)) if single_buffer else {}
    if len(arrays) == 1:
        return [pl.BlockSpec((tm, width), lambda i, j: (i, 0), **mode)], None
    n_lat_tiles = arrays[0].shape[0] // tm
    return [pl.BlockSpec((tm, width), lambda i, j: (jnp.minimum(i, n_lat_tiles - 1), 0), **mode),
            pl.BlockSpec((tm, width), lambda i, j: (jnp.maximum(i - n_lat_tiles, 0), 0),
                         pipeline_mode=pl.Buffered(1))], n_lat_tiles


def _proj_call(xs, norm_g, modl, w_in_p, n_batch, n_lat, tm, tn):
    D = xs[0].shape[1]
    R = sum(a.shape[0] for a in xs)
    P = w_in_p.shape[1]
    tpb = n_lat // tm
    if tn == P:
        w_spec = pl.BlockSpec((D, tn), lambda i, j: (0, j), pipeline_mode=pl.Buffered(1))
    else:
        w_spec = pl.BlockSpec((D, tn), lambda i, j: (0, j))
    row_spec = pl.BlockSpec((tm, D), lambda i, j: (i, 0))
    out_specs = [row_spec, pl.BlockSpec((tm, tn), lambda i, j: (i, j))]
    out_shape = [jax.ShapeDtypeStruct((R, D), BF), jax.ShapeDtypeStruct((R, P), F32)]
    x_specs, n_lat_tiles = _split_row_specs(xs, tm, D)
    return pl.pallas_call(
        functools.partial(_proj_kernel, n_lat_tiles=n_lat_tiles),
        grid=(R // tm, P // tn),
        in_specs=x_specs + [
            pl.BlockSpec((1, D), lambda i, j: (0, 0)),
            _mod_spec(0, tpb, n_batch, D),
            _mod_spec(1, tpb, n_batch, D),
            w_spec,
        ],
        out_specs=out_specs,
        out_shape=out_shape,
        scratch_shapes=[pltpu.VMEM((tm, D), BF)],
        compiler_params=_cparams(("parallel", "arbitrary")),
        name="norm_mod_proj",
    )(*xs, norm_g.reshape(1, D), modl, modl, w_in_p)


def _sumsq(x):
    xx = None
    for c in range(x.shape[1] // LANES):
        b = x[:, c * LANES:(c + 1) * LANES]
        xx = b * b if xx is None else xx + b * b
    hi = xx.astype(BF)
    lo = (xx - hi.astype(F32)).astype(BF)
    ones = jnp.ones((LANES, LANES), BF)
    return _dot(hi, ones) + _dot(lo, ones)


def _rms(x, g, n):
    r = lax.rsqrt(_sumsq(x) / n + EPS)
    blocks = [x[:, c * LANES:(c + 1) * LANES] * r * g[:, c * LANES:(c + 1) * LANES]
              for c in range(x.shape[1] // LANES)]
    return blocks[0] if len(blocks) == 1 else jnp.concatenate(blocks, axis=1)


def _rope(x, cos, sin_signed, half):
    lane = lax.broadcasted_iota(jnp.int32, x.shape, 1)
    first = (lane & (2 * half - 1)) < half
    xr = jnp.where(first, pltpu.roll(x, LANES - half, 1), pltpu.roll(x, half, 1))
    return x * cos + xr * sin_signed


def _heads_kernel(p_ref, ca_ref, sa_ref, cm_ref, sm_ref,
                  qa_g, ka_g, qw_g, kw_g, qn_g, kn_g, qm_g, km_g, cqn_g, ckvn_g, wuq_ref, wukv_ref,
                  qa_ref, ka_ref, va_ref, qm_ref, km_ref, vm_ref,
                  qw_ref, kw_ref, vw_ref, qn_ref, kn_ref, vn_ref):
    hd = HEAD_DIM
    sc128 = HEAD_DIM ** -0.5 * LOG2E
    sc192 = MLA_QK ** -0.5 * LOG2E
    ca, sa, cm, sm = ca_ref[...], sa_ref[...], cm_ref[...], sm_ref[...]

    def sl(off, h, w=hd):
        return p_ref[:, off + h * w: off + (h + 1) * w]

    ones = jnp.ones((p_ref.shape[0], hd), BF)

    def put_values(v_out, v_off, n_heads):
        for h in range(n_heads):
            v_out[:, 2 * h * hd:(2 * h + 1) * hd] = sl(v_off, h).astype(BF)
            v_out[:, (2 * h + 1) * hd:(2 * h + 2) * hd] = ones

    for (q_off, k_off, v_off, qg, kg, q_out, k_out, v_out) in (
            (P_AQ, P_AK, P_AV, qa_g, ka_g, qa_ref, ka_ref, va_ref),
            (P_WQ, P_WK, P_WV, qw_g, kw_g, qw_ref, kw_ref, vw_ref)):
        for h in range(4):
            q = _rope(_rms(sl(q_off, h), qg[...], hd), ca, sa, 32)
            q_out[:, h * hd:(h + 1) * hd] = (q * sc128).astype(BF)
        for h in range(2):
            k = _rope(_rms(sl(k_off, h), kg[...], hd), ca, sa, 32)
            k_out[:, h * hd:(h + 1) * hd] = k.astype(BF)
        put_values(v_out, v_off, 2)

    for h in range(4):
        qn_ref[:, h * hd:(h + 1) * hd] = (_rms(sl(P_NQ, h), qn_g[...], hd) * sc128).astype(BF)
        kn_ref[:, h * hd:(h + 1) * hd] = _rms(sl(P_NK, h), kn_g[...], hd).astype(BF)
    put_values(vn_ref, P_NV, 4)

    cq = _rms(p_ref[:, P_MCQ:P_MCQ + 384], cqn_g[...], 384)
    qf = _dot(cq.astype(BF), wuq_ref[...])
    ckv = _rms(p_ref[:, P_MCKV:P_MCKV + 256], ckvn_g[...], 256)
    kvf = _dot(ckv.astype(BF), wukv_ref[...])
    kr = p_ref[:, P_MKR:P_MKR + hd]
    kr_ss = _sumsq(kr)
    qg0, qg1 = qm_g[:, :hd], qm_g[:, hd:]
    kg0, kg1 = km_g[:, :hd], km_g[:, hd:]
    for h in range(4):
        q0 = qf[:, h * MLA_PAD: h * MLA_PAD + hd]
        q1 = qf[:, h * MLA_PAD + hd: (h + 1) * MLA_PAD]
        r = lax.rsqrt(_sumsq(qf[:, h * MLA_PAD:(h + 1) * MLA_PAD]) / MLA_QK + EPS)
        qm_ref[:, h * MLA_PAD: h * MLA_PAD + hd] = (q0 * r * qg0 * sc192).astype(BF)
        qm_ref[:, h * MLA_PAD + hd: (h + 1) * MLA_PAD] = (
            _rope(q1 * r * qg1, cm, sm, 16) * sc192).astype(BF)
        k0 = kvf[:, h * 2 * hd: h * 2 * hd + hd]
        r = lax.rsqrt((_sumsq(k0) + kr_ss) / MLA_QK + EPS)
        km_ref[:, h * MLA_PAD: h * MLA_PAD + hd] = (k0 * r * kg0).astype(BF)
        km_ref[:, h * MLA_PAD + hd: (h + 1) * MLA_PAD] = _rope(kr * r * kg1, cm, sm, 16).astype(BF)
        vm_ref[:, 2 * h * hd:(2 * h + 1) * hd] = kvf[:, h * 2 * hd + hd: (h + 1) * 2 * hd].astype(BF)
        vm_ref[:, (2 * h + 1) * hd:(2 * h + 2) * hd] = ones


def _heads_call(proj, tabs, gains, wuq_p, wukv, n_batch, n_lat, tm):
    R, P = proj.shape
    tpb = n_lat // tm
    tab_spec = pl.BlockSpec((tm, LANES), lambda i: (jnp.where(i < n_batch * tpb, i % tpb, tpb), 0))

    def full(a):
        return pl.BlockSpec(a.shape, lambda i: (0,) * a.ndim)

    widths = (512, 256, 512, 4 * MLA_PAD, 4 * MLA_PAD, 1024, 512, 256, 512, 512, 512, 1024)
    return pl.pallas_call(
        _heads_kernel,
        grid=(R // tm,),
        in_specs=[pl.BlockSpec((tm, P), lambda i: (i, 0))] + [tab_spec] * 4
        + [full(g) for g in gains] + [full(wuq_p), full(wukv)],
        out_specs=[pl.BlockSpec((tm, w), lambda i: (i, 0)) for w in widths],
        out_shape=[jax.ShapeDtypeStruct((R, w), BF) for w in widths],
        compiler_params=_cparams(("parallel",)),
        name="head_prep",
    )(proj, *tabs, *gains, wuq_p, wukv)


def _flash_kernel(q_ref, k_ref, v_ref, kc_ref, vc_ref, o_ref, q_scr, s_scr, m_scr, acc_scr,
                  *, G, d, dv, w, unroll):
    tq = q_ref.shape[0]
    n_blk = k_ref.shape[0] // w
    for g in range(G):
        q_scr[g * tq:(g + 1) * tq, :] = q_ref[:, g * d:(g + 1) * d]
    m_scr[...] = jnp.full_like(m_scr, NEG)
    acc_scr[...] = jnp.zeros_like(acc_scr)

    def scores(k):
        return _dot_nt(q_scr[...], k)

    def absorb(s, v):
        m_prev = m_scr[...]
        m_next = jnp.maximum(m_prev, s.max(axis=1, keepdims=True))
        alpha = jnp.exp2(m_prev - m_next)
        p = jnp.concatenate(
            [jnp.exp2(s[:, c * LANES:(c + 1) * LANES] - m_next) for c in range(s.shape[1] // LANES)],
            axis=1)
        pv = _dot(p.astype(BF), v)
        acc_scr[...] = jnp.concatenate([alpha, alpha], axis=1) * acc_scr[...] + pv
        m_scr[...] = m_next

    n_c = kc_ref.shape[0]
    s_scr[1, :, 0:n_c] = scores(kc_ref[...])
    s_scr[0] = scores(k_ref[0:w, :])
    absorb(s_scr[1, :, 0:n_c], vc_ref[...])

    def group(jj, carry):
        base = jj * (unroll * w)
        for u in range(unroll):
            cur = pl.multiple_of(base + u * w, w)
            nxt = pl.multiple_of(jnp.minimum(base + (u + 1) * w, (n_blk - 1) * w), w)
            s_scr[(u + 1) % 2] = scores(k_ref[pl.ds(nxt, w), :])
            absorb(s_scr[u % 2], v_ref[pl.ds(cur, w), :])
        return carry

    lax.fori_loop(0, n_blk // unroll, group, 0)

    acc = acc_scr[...]
    o = acc[:, :dv] / acc[:, dv:]
    for g in range(G):
        o_ref[:, g * dv:(g + 1) * dv] = o[g * tq:(g + 1) * tq].astype(BF)


def _flash_call(q, k, v, n_batch, n_lat, n_ctx, G, d, tq, w, name):
    dv = LANES
    hkv = k.shape[1] // d
    nq = n_lat // tq
    cb = (n_batch * n_lat) // n_ctx
    n_blk = n_lat // w
    unroll = next(u for u in (8, 4, 2) if n_blk % u == 0)
    assert n_ctx <= w
    return pl.pallas_call(
        functools.partial(_flash_kernel, G=G, d=d, dv=dv, w=w, unroll=unroll),
        grid=(n_batch, hkv, nq),
        in_specs=[
            pl.BlockSpec((tq, G * d), lambda b, h, i: (b * nq + i, h)),
            pl.BlockSpec((n_lat, d), lambda b, h, i: (b, h)),
            pl.BlockSpec((n_lat, 2 * dv), lambda b, h, i: (b, h)),
            pl.BlockSpec((n_ctx, d), lambda b, h, i: (cb + b, h)),
            pl.BlockSpec((n_ctx, 2 * dv), lambda b, h, i: (cb + b, h)),
        ],
        out_specs=pl.BlockSpec((tq, G * dv), lambda b, h, i: (b * nq + i, h)),
        out_shape=jax.ShapeDtypeStruct((n_batch * n_lat, hkv * G * dv), BF),
        scratch_shapes=[pltpu.VMEM((G * tq, d), BF), pltpu.VMEM((2, G * tq, w), F32),
                        pltpu.VMEM((G * tq, LANES), F32), pltpu.VMEM((G * tq, 2 * dv), F32)],
        compiler_params=_cparams(("parallel", "parallel", "parallel")),
        name=name,
    )(q, k, v, k, v)


def _window_kernel(sink_ref, q_ref, kp_ref, km_ref, kn_ref, vp_ref, vm_ref, vn_ref, kc_ref, vc_ref,
                   o_ref, s_scr, sc_scr, *, G):
    hd = HEAD_DIM
    i = pl.program_id(1)
    ni = pl.num_programs(1)
    tq = q_ref.shape[0]
    n_heads = q_ref.shape[1] // hd
    r = lax.broadcasted_iota(jnp.int32, (tq, tq + 2 * WINDOW), 0)
    c = lax.broadcasted_iota(jnp.int32, (tq, tq + 2 * WINDOW), 1)
    rel = c - WINDOW - r
    lo = jnp.where(i > 0, 0, WINDOW)
    hi = jnp.where(i < ni - 1, tq + 2 * WINDOW, tq + WINDOW)
    valid = (jnp.abs(rel) <= WINDOW) & (c >= lo) & (c < hi)

    def kv_cat(p_ref, m_ref, n_ref, kvh, w):
        cols = slice(kvh * w, (kvh + 1) * w)
        return jnp.concatenate([p_ref[:, cols], m_ref[:, cols], n_ref[:, cols]], axis=0)

    for qh in range(n_heads):
        kvh = qh // G
        q = q_ref[:, qh * hd:(qh + 1) * hd]
        s_scr[qh] = jnp.where(valid, _dot_nt(q, kv_cat(kp_ref, km_ref, kn_ref, kvh, hd)), NEG)
        sc_scr[qh] = _dot_nt(q, kc_ref[:, kvh * hd:(kvh + 1) * hd])
    for qh in range(n_heads):
        kvh = qh // G
        s = s_scr[qh]
        sc = sc_scr[qh]
        sk = sink_ref[qh] * LOG2E
        m = jnp.maximum(jnp.max(s, axis=-1, keepdims=True), jnp.max(sc, axis=-1, keepdims=True))
        m = jnp.maximum(m, sk)
        p = jnp.exp2(s - m)
        pc = jnp.exp2(sc - m)
        o = (_dot(p.astype(BF), kv_cat(vp_ref, vm_ref, vn_ref, kvh, 2 * hd))
             + _dot(pc.astype(BF), vc_ref[:, kvh * 2 * hd:(kvh + 1) * 2 * hd]))
        l = o[:, hd:] + jnp.exp2(sk - m)
        o_ref[:, qh * hd:(qh + 1) * hd] = (o[:, :hd] / l).astype(BF)


def _window_call(sink, q, k, v, n_batch, n_lat, n_ctx, G, tq):
    hd = HEAD_DIM
    hkv = k.shape[1] // hd
    nq = n_lat // tq
    bpt = tq // WINDOW
    nblk = k.shape[0] // WINDOW
    cb = (n_batch * n_lat) // n_ctx

    def prev(b, i):
        return (jnp.maximum((b * nq + i) * bpt - 1, 0), 0)

    def nxt(b, i):
        return (jnp.minimum((b * nq + i + 1) * bpt, nblk - 1), 0)

    def main(b, i):
        return (b * nq + i, 0)

    def ctxb(b, i):
        return (cb + b, 0)

    kw, vw, qw = hkv * hd, hkv * 2 * hd, hkv * G * hd
    return pl.pallas_call(
        functools.partial(_window_kernel, G=G),
        grid=(n_batch, nq),
        in_specs=[
            pl.BlockSpec(memory_space=pltpu.SMEM),
            pl.BlockSpec((tq, qw), main),
            pl.BlockSpec((WINDOW, kw), prev), pl.BlockSpec((tq, kw), main), pl.BlockSpec((WINDOW, kw), nxt),
            pl.BlockSpec((WINDOW, vw), prev), pl.BlockSpec((tq, vw), main), pl.BlockSpec((WINDOW, vw), nxt),
            pl.BlockSpec((n_ctx, kw), ctxb), pl.BlockSpec((n_ctx, vw), ctxb),
        ],
        out_specs=pl.BlockSpec((tq, qw), main),
        out_shape=jax.ShapeDtypeStruct((n_batch * n_lat, qw), BF),
        scratch_shapes=[pltpu.VMEM((hkv * G, tq, tq + 2 * WINDOW), F32), pltpu.VMEM((hkv * G, tq, n_ctx), F32)],
        compiler_params=_cparams(("parallel", "parallel")),
        name="window_attn",
    )(sink, q, k, k, k, v, v, v, k, v)


NA_TQ = 8 * GRID_W
NA_KB = 4 * GRID_W
NA_NKB = 4


def _na_kernel(q_ref, k0, k1, k2, k3, v0, v1, v2, v3, kc_ref, vc_ref, bias_ref, o_ref, s_scr, sc_scr):
    hd = HEAD_DIM
    n_heads = q_ref.shape[1] // hd

    def cat(refs, h, w):
        return jnp.concatenate([r[:, h * w:(h + 1) * w] for r in refs], axis=0)

    for h in range(n_heads):
        q = q_ref[:, h * hd:(h + 1) * hd]
        s_scr[h] = _dot_nt(q, cat((k0, k1, k2, k3), h, hd)) + bias_ref[h]
        sc_scr[h] = _dot_nt(q, kc_ref[:, h * hd:(h + 1) * hd])
    for h in range(n_heads):
        s = s_scr[h]
        sc = sc_scr[h]
        m = jnp.maximum(jnp.max(s, axis=-1, keepdims=True), jnp.max(sc, axis=-1, keepdims=True))
        p = jnp.exp2(s - m)
        pc = jnp.exp2(sc - m)
        o = (_dot(p.astype(BF), cat((v0, v1, v2, v3), h, 2 * hd))
             + _dot(pc.astype(BF), vc_ref[:, h * 2 * hd:(h + 1) * 2 * hd]))
        o_ref[:, h * hd:(h + 1) * hd] = (o[:, :hd] / o[:, hd:]).astype(BF)


def _na_bias_tables(rpb, n_rows):
    H = rpb.shape[0]
    J = n_rows // 8
    nb = n_rows // 4
    n_dy, n_dx = 2 * NA_WIN_H - 1, 2 * NA_WIN_W - 1
    cq = np.arange(GRID_W)
    kc = np.arange(GRID_W)
    cs = np.clip(cq - NA_WIN_W // 2, 0, GRID_W - NA_WIN_W)
    dx = kc[None, :] - cq[:, None] + NA_WIN_W - 1
    col_ok = (kc[None, :] >= cs[:, None]) & (kc[None, :] < cs[:, None] + NA_WIN_W)
    dx1h = np.zeros((n_dx, GRID_W * GRID_W), np.float32)
    dx1h[np.clip(dx, 0, n_dx - 1).reshape(-1), np.arange(GRID_W * GRID_W)] = 1.0
    blocks = jnp.einsum("hyx,xb->hyb", rpb.astype(F32) * LOG2E, jnp.asarray(dx1h),
                        precision=lax.Precision.HIGHEST)
    blocks = jnp.where(jnp.asarray(col_ok.reshape(-1))[None, None, :], blocks, NEG)
    blocks = jnp.concatenate([blocks, jnp.full((H, 1, GRID_W * GRID_W), NEG, F32)], axis=1)
    blocks = blocks.reshape(H, n_dy + 1, GRID_W, GRID_W)
    idx = np.zeros((3, 8, 4 * NA_NKB), np.int32)
    for v, jv in enumerate((0, min(1, J - 1), J - 1)):
        r = 8 * jv + np.arange(8)
        kb_un = 2 * jv - 1 + np.arange(NA_NKB)
        kb = np.clip(kb_un, 0, nb - 1)
        krow = (4 * kb[:, None] + np.arange(4)[None, :]).reshape(-1)
        krow_dup = np.repeat(kb != kb_un, 4)
        rs = np.clip(r - NA_WIN_H // 2, 0, n_rows - NA_WIN_H)
        row_ok = (krow[None, :] >= rs[:, None]) & (krow[None, :] < rs[:, None] + NA_WIN_H) & ~krow_dup[None, :]
        dy = krow[None, :] - r[:, None] + NA_WIN_H - 1
        idx[v] = np.where(row_ok, np.clip(dy, 0, n_dy - 1), n_dy)
    t = jnp.take(blocks, jnp.asarray(idx.reshape(-1)), axis=1)
    t = t.reshape(H, 3, 8, 4 * NA_NKB, GRID_W, GRID_W).transpose(0, 1, 2, 4, 3, 5)
    return t.reshape(H, 3, NA_TQ, NA_NKB * NA_KB)


def _na_call(q, k, v, bias, n_batch, n_lat, n_ctx):
    hd = HEAD_DIM
    H = q.shape[1] // hd
    J = n_lat // NA_TQ
    nb = n_lat // NA_KB
    cb = (n_batch * n_lat) // n_ctx

    def kblk(t):
        return lambda b, j: (b * nb + jnp.clip(2 * j - 1 + t, 0, nb - 1), 0)

    def qmap(b, j):
        return (b * J + j, 0)

    def ctxb(b, j):
        return (cb + b, 0)

    def bmap(b, j):
        return (0, jnp.where(j == 0, 0, jnp.where(j == J - 1, 2, 1)), 0, 0)

    n_keys = NA_NKB * NA_KB
    return pl.pallas_call(
        _na_kernel,
        grid=(n_batch, J),
        in_specs=[pl.BlockSpec((NA_TQ, H * hd), qmap)]
        + [pl.BlockSpec((NA_KB, H * hd), kblk(t)) for t in range(NA_NKB)]
        + [pl.BlockSpec((NA_KB, H * 2 * hd), kblk(t)) for t in range(NA_NKB)]
        + [pl.BlockSpec((n_ctx, H * hd), ctxb), pl.BlockSpec((n_ctx, H * 2 * hd), ctxb),
           pl.BlockSpec((H, None, NA_TQ, n_keys), bmap)],
        out_specs=pl.BlockSpec((NA_TQ, H * hd), qmap),
        out_shape=jax.ShapeDtypeStruct((n_batch * n_lat, H * hd), BF),
        scratch_shapes=[pltpu.VMEM((H, NA_TQ, n_keys), F32), pltpu.VMEM((H, NA_TQ, n_ctx), F32)],
        compiler_params=_cparams(("parallel", "parallel")),
        name="neighbourhood_attn",
    )(q, k, k, k, k, v, v, v, v, k, v, bias)


def _ctx_attn_kernel(sink_ref, q_ref, k_ref, v_ref, o_ref, *, use_sink):
    s = _dot_nt(q_ref[...], k_ref[...])
    m = jnp.max(s, axis=-1, keepdims=True)
    if use_sink:
        sk = sink_ref[pl.program_id(1)] * LOG2E
        m = jnp.maximum(m, sk)
    p = jnp.exp2(s - m)
    l = jnp.sum(p, axis=-1, keepdims=True)
    if use_sink:
        l = l + jnp.exp2(sk - m)
    o_ref[...] = (_dot(p.astype(BF), v_ref[...]) / l).astype(BF)


def _ctx_attn_call(sink, q, k, v, n_batch, n_lat, n_ctx, G, d, dv, use_sink, name):
    v_stride = 2
    H = q.shape[1] // d
    cb = (n_batch * n_lat) // n_ctx
    return pl.pallas_call(
        functools.partial(_ctx_attn_kernel, use_sink=use_sink),
        grid=(n_batch, H),
        in_specs=[
            pl.BlockSpec(memory_space=pltpu.SMEM),
            pl.BlockSpec((n_ctx, d), lambda b, h: (cb + b, h)),
            pl.BlockSpec((n_ctx, d), lambda b, h: (cb + b, h // G)),
            pl.BlockSpec((n_ctx, dv), lambda b, h: (cb + b, (h // G) * v_stride)),
        ],
        out_specs=pl.BlockSpec((n_ctx, dv), lambda b, h: (b, h)),
        out_shape=jax.ShapeDtypeStruct((n_batch * n_ctx, H * dv), BF),
        compiler_params=_cparams(("parallel", "parallel")),
        name=name,
    )(sink, q, k, v)


def _merge_kernel(*refs, n_lat_tiles):
    n_row_in = 5 if n_lat_tiles is None else 10
    row_refs = refs[:1 + n_row_in]
    (wg_ref, bg_ref, wb_ref, wo_ref, g1_ref, nf_ref, sh2_ref, sc2_ref, wr_ref, br_ref,
     xo_ref, h2_ref, ei_ref, wt_ref, y_scr) = refs[1 + n_row_in:]
    if n_lat_tiles is None:
        x_ref, h_ref = row_refs[0], row_refs[1]
        mixer_rows = [r.__getitem__ for r in row_refs[2:]]
        x_rows = x_ref.__getitem__
    else:
        h_ref = row_refs[2]
        is_lat = pl.program_id(0) < n_lat_tiles

        def pick(l_ref, c_ref):
            return lambda idx: jnp.where(is_lat, l_ref[idx], c_ref[idx])

        x_rows = pick(row_refs[0], row_refs[1])
        mixer_rows = [pick(row_refs[3 + 2 * i], row_refs[4 + 2 * i]) for i in range(4)]
    j = pl.program_id(1)

    @pl.when(j == 0)
    def _():
        y_scr[...] = jnp.zeros_like(y_scr)

    h = h_ref[...]
    acc = None
    for i, rows_of in enumerate(mixer_rows):
        gate = _sigmoid(_dot(h, wg_ref[i]) + bg_ref[i])
        y = gate * _dot(rows_of(...), wb_ref[i])
        acc = y if acc is None else acc + y
    y_scr[...] += _dot(acc.astype(BF), wo_ref[...])

    @pl.when(j == pl.num_programs(1) - 1)
    def _():
        x = x_rows(...) + g1_ref[...] * y_scr[...]
        xo_ref[...] = x
        ms = jnp.mean(x * x, axis=-1, keepdims=True)
        h2 = x * lax.rsqrt(ms + EPS) * nf_ref[...]
        h2 = h2 * (1.0 + sc2_ref[...]) + sh2_ref[...]
        h2_ref[...] = h2
        hi = h2.astype(BF)
        lo = (h2 - hi.astype(F32)).astype(BF)
        both = _dot(hi, wr_ref[...])
        logits = both[:, :LANES] + (_dot(lo, wr_ref[:, :LANES]) + both[:, LANES:])
        lt = logits.T[:N_EXPERTS, :]
        scores = _sigmoid(lt)
        biased = scores + br_ref[...]
        row = lax.broadcasted_iota(jnp.int32, lt.shape, 0)
        row_f = row.astype(F32)
        ninf = -jnp.inf

        def top2(vals):
            t1 = jnp.max(vals, axis=0, keepdims=True)
            i1 = jnp.min(jnp.where(vals == t1, row_f, float(N_EXPERTS)), axis=0, keepdims=True)
            vals2 = jnp.where(row_f == i1, ninf, vals)
            t2 = jnp.max(vals2, axis=0, keepdims=True)
            i2 = jnp.min(jnp.where(vals2 == t2, row_f, float(N_EXPERTS)), axis=0, keepdims=True)
            return t1, i1, t2, i2

        per = N_EXPERTS // N_GROUPS
        best, gi = None, None
        for g in range(N_GROUPS):
            ing = (row >= g * per) & (row < (g + 1) * per)
            t1, _, t2, _ = top2(jnp.where(ing, biased, ninf))
            gs = t1 + t2
            if best is None:
                best, gi = gs, jnp.zeros_like(gs)
            else:
                better = gs > best
                best = jnp.where(better, gs, best)
                gi = jnp.where(better, float(g), gi)
        row_grp = (row >> 2).astype(F32)
        _, i1, _, i2 = top2(jnp.where(row_grp == gi, biased, NEG))
        w1 = jnp.sum(jnp.where(row_f == i1, scores, 0.0), axis=0, keepdims=True)
        w2 = jnp.sum(jnp.where(row_f == i2, scores, 0.0), axis=0, keepdims=True)
        den = w1 + w2
        out_row = lax.broadcasted_iota(jnp.int32, ei_ref.shape, 0)
        ei_ref[...] = jnp.where(out_row == 0, i1, jnp.where(out_row == 1, i2, 0.0)).astype(jnp.int32)
        wt_ref[...] = jnp.where(out_row == 0, w1 / den, jnp.where(out_row == 1, w2 / den, 0.0))


def _merge_call(xs, h, outs, wg, bg, wb, wo, modl, norm_f, wr_p, br_p, rows, n_batch, n_lat, tm, tn):
    D = xs[0].shape[1]
    bw = wb.shape[1]
    tpb = n_lat // tm

    def rowmap(i, j):
        return (i, 0)

    x_specs, n_lat_tiles = _split_row_specs(xs, tm, D)
    o_specs, o_args = [], []
    for o in outs:
        assert len(o) == len(xs)
        o_specs += _split_row_specs(o, tm, bw)[0]
        o_args += list(o)
    return pl.pallas_call(
        functools.partial(_merge_kernel, n_lat_tiles=n_lat_tiles),
        grid=(rows // tm, D // tn),
        in_specs=x_specs + [pl.BlockSpec((tm, D), rowmap)] + o_specs + [
            pl.BlockSpec((4, D, tn), lambda i, j: (0, 0, j)),
            pl.BlockSpec((4, 1, tn), lambda i, j: (0, 0, j)),
            pl.BlockSpec((4, bw, tn), lambda i, j: (0, 0, j)),
            pl.BlockSpec((tn, D), lambda i, j: (j, 0)),
            _mod_spec(2, tpb, n_batch, D),
            pl.BlockSpec((1, D), lambda i, j: (0, 0)),
            _mod_spec(3, tpb, n_batch, D),
            _mod_spec(4, tpb, n_batch, D),
            pl.BlockSpec((D, 2 * LANES), lambda i, j: (0, 0)),
            pl.BlockSpec((N_EXPERTS, 1), lambda i, j: (0, 0)),
        ],
        out_specs=[
            pl.BlockSpec((tm, D), rowmap),
            pl.BlockSpec((tm, D), rowmap),
            pl.BlockSpec((8, tm), lambda i, j: (0, i)),
            pl.BlockSpec((8, tm), lambda i, j: (0, i)),
        ],
        out_shape=[
            jax.ShapeDtypeStruct((rows, D), F32),
            jax.ShapeDtypeStruct((rows, D), F32),
            jax.ShapeDtypeStruct((8, rows), jnp.int32),
            jax.ShapeDtypeStruct((8, rows), F32),
        ],
        scratch_shapes=[pltpu.VMEM((tm, D), F32)],
        compiler_params=_cparams(("parallel", "arbitrary"), VMEM_LIMIT_MERGE),
        name="merge_residual_router",
    )(*xs, h, *o_args, wg, bg, wb, wo, modl, norm_f.reshape(1, D), modl, modl, wr_p, br_p)


def _row_dma_issue(n_rows, make_copy):
    def issue(r, carry):
        for k in range(2):
            make_copy(r, k).start()
        return carry

    lax.fori_loop(0, n_rows, issue, 0, unroll=8)


def _row_dma_drain(n_rows, make_copy):
    def drain(r, carry):
        for k in range(2):
            make_copy(r, k).wait()
        return carry

    lax.fori_loop(0, n_rows, drain, 0, unroll=8)


def _row_dma_loop(n_rows, make_copy):
    _row_dma_issue(n_rows, make_copy)
    _row_dma_drain(n_rows, make_copy)


def _scatter_kernel(fill_ref, pos_ref, h_ref, xs_ref, zbuf, sem, zsem):
    tg = zbuf.shape[0]

    @pl.when(pl.program_id(0) == 0)
    def _():
        zbuf[...] = jnp.zeros_like(zbuf)

        def fill(t, carry):
            @pl.when(fill_ref[t] != 0)
            def _():
                cp = pltpu.make_async_copy(zbuf, xs_ref.at[pl.ds(pl.multiple_of(t * tg, tg), tg), :], zsem)
                cp.start()
                cp.wait()
            return carry

        lax.fori_loop(0, fill_ref.shape[0], fill, 0)

    def make_copy(r, k):
        return pltpu.make_async_copy(h_ref.at[pl.ds(r, 1), :],
                                     xs_ref.at[pl.ds(pos_ref[2 * r + k], 1), :], sem)

    _row_dma_loop(h_ref.shape[0], make_copy)


def _scatter_call(h2, pos, fill, n_slots, tm, tg):
    rows, D = h2.shape
    grid_spec = pltpu.PrefetchScalarGridSpec(
        num_scalar_prefetch=1,
        grid=(rows // tm,),
        in_specs=[
            pl.BlockSpec((2 * tm,), lambda i, f: (i,), memory_space=pltpu.SMEM),
            pl.BlockSpec((tm, D), lambda i, f: (i, 0)),
        ],
        out_specs=pl.BlockSpec(memory_space=pl.ANY),
        scratch_shapes=[pltpu.VMEM((tg, D), F32), pltpu.SemaphoreType.DMA(()), pltpu.SemaphoreType.DMA(())],
    )
    return pl.pallas_call(
        _scatter_kernel,
        grid_spec=grid_spec,
        out_shape=jax.ShapeDtypeStruct((n_slots, D), F32),
        compiler_params=_cparams(("arbitrary",)),
        name="moe_scatter_rows",
    )(fill, pos, h2)


def _experts_kernel(te_ref, nu_ref, x_ref, w1_ref, w3_ref, w2_ref, y_ref, w1_scr, w3_scr, w2_scr):
    i = pl.program_id(0)

    @pl.when((i == 0) | (te_ref[i] != te_ref[jnp.maximum(i - 1, 0)]))
    def _():
        w1_scr[...] = w1_ref[...].astype(BF)
        w3_scr[...] = w3_ref[...].astype(BF)
        w2_scr[...] = w2_ref[...].astype(BF)

    @pl.when(i < nu_ref[0])
    def _():
        x = x_ref[...].astype(BF)
        a = _dot(x, w1_scr[...])
        b = _dot(x, w3_scr[...])
        hid = (a * _sigmoid(a)) * b
        y_ref[...] = _dot(hid.astype(BF), w2_scr[...])

    @pl.when(i >= nu_ref[0])
    def _():
        y_ref[...] = jnp.zeros_like(y_ref)


def _experts_call(xs, tile_expert, n_used, w1, w3, w2, layer, tg):
    S, D = xs.shape
    de = w1.shape[3]
    grid_spec = pltpu.PrefetchScalarGridSpec(
        num_scalar_prefetch=2,
        grid=(S // tg,),
        in_specs=[
            pl.BlockSpec((tg, D), lambda i, te, nu: (i, 0)),
            pl.BlockSpec((None, None, D, de), lambda i, te, nu: (layer, te[i], 0, 0)),
            pl.BlockSpec((None, None, D, de), lambda i, te, nu: (layer, te[i], 0, 0)),
            pl.BlockSpec((None, None, de, D), lambda i, te, nu: (layer, te[i], 0, 0)),
        ],
        out_specs=pl.BlockSpec((tg, D), lambda i, te, nu: (i, 0)),
        scratch_shapes=[pltpu.VMEM((D, de), BF), pltpu.VMEM((D, de), BF), pltpu.VMEM((de, D), BF)],
    )
    return pl.pallas_call(
        _experts_kernel,
        grid_spec=grid_spec,
        out_shape=jax.ShapeDtypeStruct((S, D), F32),
        compiler_params=_cparams(("arbitrary",)),
        name="moe_experts",
    )(tile_expert, n_used, xs, w1, w3, w2)


def _combine_kernel(pos_ref, pos_next_ref, x_ref, wt_ref, g2_ref, ys_ref, o_ref, buf, sem):
    i = pl.program_id(0)
    n = pl.num_programs(0)
    tm = x_ref.shape[0]
    slot = i % 2

    def gather(p_ref, s):
        def make_copy(r, k):
            return pltpu.make_async_copy(ys_ref.at[pl.ds(p_ref[2 * r + k], 1), :],
                                         buf.at[s, k, pl.ds(r, 1), :], sem.at[s])
        return make_copy

    @pl.when(i == 0)
    def _():
        _row_dma_issue(tm, gather(pos_ref, 0))

    @pl.when(i + 1 < n)
    def _():
        _row_dma_issue(tm, gather(pos_next_ref, 1 - slot))

    _row_dma_drain(tm, gather(pos_ref, slot))
    wt = jnp.concatenate([wt_ref[...], jnp.zeros((LANES - 8, tm), F32)], axis=0).T
    moe = wt[:, 0:1] * buf[slot, 0] + wt[:, 1:2] * buf[slot, 1]
    o_ref[...] = x_ref[...] + g2_ref[...] * moe


def _combine_call(X, ys, pos, wts, modl, n_batch, n_lat, tm):
    rows, D = X.shape
    tpb = n_lat // tm
    n_steps = rows // tm
    return pl.pallas_call(
        _combine_kernel,
        grid=(n_steps,),
        in_specs=[
            pl.BlockSpec((2 * tm,), lambda i: (i,), memory_space=pltpu.SMEM),
            pl.BlockSpec((2 * tm,), lambda i: (jnp.minimum(i + 1, n_steps - 1),), memory_space=pltpu.SMEM),
            pl.BlockSpec((tm, D), lambda i: (i, 0)),
            pl.BlockSpec((8, tm), lambda i: (0, i)),
            _mod_spec(5, tpb, n_batch, D),
            pl.BlockSpec(memory_space=pl.ANY),
        ],
        out_specs=pl.BlockSpec((tm, D), lambda i: (i, 0)),
        out_shape=jax.ShapeDtypeStruct((rows, D), F32),
        scratch_shapes=[pltpu.VMEM((2, 2, tm, D), F32), pltpu.SemaphoreType.DMA((2,))],
        compiler_params=_cparams(("arbitrary",)),
        name="moe_combine_residual",
    )(pos, pos, X, wts, modl, ys)


def _route_positions(eidx, tg):
    rows = eidx.shape[0]
    e_flat = eidx.reshape(-1)
    onehot = (e_flat[:, None] == jnp.arange(N_EXPERTS, dtype=jnp.int32)[None, :]).astype(jnp.int32)
    csum = jnp.cumsum(onehot, axis=0)
    counts = csum[-1]
    rank = jnp.sum((csum - onehot) * onehot, axis=1)
    padded = ((counts + tg - 1) // tg) * tg
    ends = jnp.cumsum(padded)
    offsets = ends - padded
    pos = jnp.sum(onehot * offsets[None, :], axis=1) + rank
    n_tiles = (2 * rows + N_EXPERTS * tg) // tg
    tile_start = jnp.arange(n_tiles, dtype=jnp.int32) * tg
    tile_expert = jnp.sum((tile_start[:, None] >= ends[None, :]).astype(jnp.int32), axis=1)
    n_used = (ends[-1] // tg).astype(jnp.int32)
    used = jnp.arange(n_tiles) < n_used
    te_1h = (tile_expert[:, None] == jnp.arange(N_EXPERTS, dtype=jnp.int32)[None, :]).astype(jnp.int32)
    real_end = jnp.sum(te_1h * (offsets + counts)[None, :], axis=1)
    fill = jnp.logical_not(used & (tile_start + tg <= real_end)).astype(jnp.int32)
    last_e = jnp.sum((((n_used - 1) * tg) >= ends).astype(jnp.int32))
    tile_expert = jnp.where(used, tile_expert, last_e).astype(jnp.int32)
    return pos.astype(jnp.int32), tile_expert, n_used.reshape(1), fill


def _rope_tables(n_lat, tm):
    n_rows = n_lat // GRID_W

    def tab(d2, axis):
        freqs = np.float32(ROPE_THETA) ** (-np.arange(d2, dtype=np.float32) / np.float32(d2))
        pos = np.arange(n_rows if axis == 0 else GRID_W, dtype=np.float32)
        ang = pos[:, None] * freqs[None, :]
        shape = (n_rows, GRID_W, d2)
        small = [np.cos(ang).astype(np.float32), np.sin(ang).astype(np.float32)]
        full = [jnp.broadcast_to(jnp.asarray(a)[:, None, :] if axis == 0 else jnp.asarray(a)[None, :, :], shape)
                for a in small]
        return tuple(a.reshape(n_lat, d2) for a in full)

    cr, sr = tab(32, 0)
    cc, sc = tab(32, 1)
    cos_a = jnp.concatenate([cr, cr, cc, cc], axis=-1)
    sin_a = jnp.concatenate([-sr, sr, -sc, sc], axis=-1)
    cr, sr = tab(16, 0)
    cc, sc = tab(16, 1)
    one, zero = jnp.ones((n_lat, 64), F32), jnp.zeros((n_lat, 64), F32)
    cos_m = jnp.concatenate([cr, cr, cc, cc, one], axis=-1)
    sin_m = jnp.concatenate([-sr, sr, -sc, sc, zero], axis=-1)
    ident_c, ident_s = jnp.ones((tm, LANES), F32), jnp.zeros((tm, LANES), F32)
    return tuple(jnp.concatenate([a, b], axis=0) for a, b in
                 ((cos_a, ident_c), (sin_a, ident_s), (cos_m, ident_c), (sin_m, ident_s)))


def _pad_mla_heads(w, n_heads):
    lead = w.shape[:-1]
    w = w.reshape(lead + (n_heads, MLA_QK))
    w = jnp.pad(w, [(0, 0)] * len(lead) + [(0, 0), (0, MLA_PAD - MLA_QK)])
    return w.reshape(lead + (n_heads * MLA_PAD,))


def kernel(x, c, ctx, c_ctx, w_mod, b_mod, norm_mix, norm_ffn, w_in, mla_qa_norm, mla_w_uq, mla_kva_norm,
           mla_w_ukv, qn_att, kn_att, qn_mla, kn_mla, qn_win, kn_win, qn_na, kn_na, win_sink, na_rpb,
           w_branch, w_gate, b_gate, w_out, w_router, b_router, moe_w1, moe_w3, moe_w2):
    B, N, D = x.shape
    n_ctx = ctx.shape[1]
    L = w_mod.shape[0]
    R = B * N + B * n_ctx
    tm = 512
    tm_moe = 256
    tg = 512
    tn_merge = min(256, D // 2)
    tq = min(512, N)
    tk = min(512, N // 2)

    x_lat, x_ctx = x.reshape(B * N, D), ctx.reshape(B * n_ctx, D)
    X = None
    cvec = jnp.zeros((8, D), F32).at[:B].set(c).at[B].set(c_ctx)
    mod = _mod_call(cvec, w_mod, b_mod, min(1024, D)).reshape(L, 8, 6, 1, D)
    tabs = _rope_tables(N, tm)

    wr_hi = w_router.astype(BF)
    wr_lo = (w_router - wr_hi.astype(F32)).astype(BF)
    lane_pad = ((0, 0), (0, LANES - N_EXPERTS))
    wr_p = jnp.concatenate([jnp.pad(wr_hi, lane_pad), jnp.pad(wr_lo, lane_pad)], axis=1)
    br_p = b_router.astype(F32).reshape(N_EXPERTS, 1)

    for l in range(L):
        last = l == L - 1
        rows = B * N if last else R
        modl = mod[l]
        w_in_l = w_in[l]
        w_in_p = jnp.concatenate(
            [w_in_l[:, :KR_END], jnp.zeros((D, 64), F32), w_in_l[:, KR_END:]], axis=1).astype(BF)
        wuq_p = _pad_mla_heads(mla_w_uq[l], 4).astype(BF)
        wukv = mla_w_ukv[l].astype(BF)
        gains = [g.reshape(1, -1) for g in (
            qn_att[l], kn_att[l], qn_win[l], kn_win[l], qn_na[l], kn_na[l],
            _pad_mla_heads(qn_mla[l], 1), _pad_mla_heads(kn_mla[l], 1), mla_qa_norm[l], mla_kva_norm[l])]

        split = l == 0
        h, proj = _proj_call((x_lat, x_ctx) if split else (X,), norm_mix[l], modl, w_in_p, B, N, tm, P_TOTAL)
        (qa, ka, va, qm, km, vm, qw, kw, vw, qn, kn, vn) = _heads_call(proj, tabs, gains, wuq_p, wukv, B, N, tm)

        sink = win_sink[l].astype(F32)
        o_att = _flash_call(qa, ka, va, B, N, n_ctx, 2, HEAD_DIM, tq, tk, "dense_gqa")
        o_mla = _flash_call(qm, km, vm, B, N, n_ctx, 1, MLA_PAD, 2 * tq, tk, "latent_attn")
        o_win = _window_call(sink, qw, kw, vw, B, N, n_ctx, 2, tq)
        bias = _na_bias_tables(na_rpb[l], N // GRID_W)
        o_na = _na_call(qn, kn, vn, bias, B, N, n_ctx)
        outs = [(o,) for o in (o_att, o_mla, o_win, o_na)]
        if not last:
            outs_c = [
                _ctx_attn_call(sink, qa, ka, va, B, N, n_ctx, 2, HEAD_DIM, HEAD_DIM, False, "ctx_dense_gqa"),
                _ctx_attn_call(sink, qm, km, vm, B, N, n_ctx, 1, MLA_PAD, HEAD_DIM, False, "ctx_latent_attn"),
                _ctx_attn_call(sink, qw, kw, vw, B, N, n_ctx, 2, HEAD_DIM, HEAD_DIM, True, "ctx_window_attn"),
                _ctx_attn_call(sink, qn, kn, vn, B, N, n_ctx, 1, HEAD_DIM, HEAD_DIM, False, "ctx_neighbourhood"),
            ]
            if split:
                outs = [(a[0], b) for a, b in zip(outs, outs_c)]
            else:
                outs = [(jnp.concatenate([a[0], b], axis=0),) for a, b in zip(outs, outs_c)]
        if split:
            xs_res = (x_lat,) if last else (x_lat, x_ctx)
        else:
            xs_res = (X,)

        X, h2, eidx, wts = _merge_call(
            xs_res, h, outs, w_gate[l].astype(BF), b_gate[l].reshape(4, 1, D), w_branch[l].astype(BF),
            w_out[l].astype(BF), modl, norm_ffn[l], wr_p, br_p, rows, B, N, tm, tn_merge)

        pos, tile_expert, n_used, fill = _route_positions(eidx[:2].T, tg)
        n_slots = 2 * rows + N_EXPERTS * tg
        xs = _scatter_call(h2, pos, fill, n_slots, tm_moe, tg)
        ys = _experts_call(xs, tile_expert, n_used, moe_w1, moe_w3, moe_w2, l, tg)
        X = _combine_call(X, ys, pos, wts, modl, B, N, tm_moe)

    return X.reshape(B, N, D)
```

```python
import functools

import numpy as np
import jax
import jax.numpy as jnp
from jax import lax
from jax.experimental import pallas as pl
from jax.experimental.pallas import tpu as pltpu

BF = jnp.bfloat16
F32 = jnp.float32

GRID_W = 64
HEAD_DIM = 128
ROPE_THETA = 10000.0
EPS = 1e-6
NEG = -1e30
LOG2E = 1.4426950408889634
WINDOW = 128
NA_WIN_H = 8
NA_WIN_W = 16
N_EXPERTS = 16
N_GROUPS = 4
MLA_QK = 192
MLA_PAD = 256
LANES = 128
VMEM_LIMIT = 56 * 1024 * 1024
VMEM_LIMIT_MERGE = 62 * 1024 * 1024

P_AQ, P_AK, P_AV = 0, 512, 768
P_MCQ, P_MCKV, P_MKR = 1024, 1408, 1664
P_WQ, P_WK, P_WV = 1792, 2304, 2560
P_NQ, P_NK, P_NV = 2816, 3328, 3840
P_TOTAL = 4352
KR_END = 1728


def _cparams(sem, vmem_limit=VMEM_LIMIT):
    return pltpu.CompilerParams(dimension_semantics=sem, vmem_limit_bytes=vmem_limit)


def _sigmoid(z):
    return 1.0 / (1.0 + jnp.exp(-z))


def _dot(a, b):
    return jnp.dot(a, b, preferred_element_type=F32)


def _dot_nt(a, b):
    return lax.dot_general(a, b, (((1,), (1,)), ((), ())), preferred_element_type=F32)


def _mod_kernel(c_ref, w_ref, b_ref, o_ref):
    c = c_ref[...]
    s = c * _sigmoid(c)
    o_ref[...] = _dot(s.astype(BF), w_ref[...].astype(BF)) + b_ref[...]


def _mod_call(cvec, w_mod, b_mod, tn):
    L, D, D6 = w_mod.shape
    return pl.pallas_call(
        _mod_kernel,
        grid=(L, D6 // tn),
        in_specs=[
            pl.BlockSpec((8, D), lambda l, j: (0, 0)),
            pl.BlockSpec((None, D, tn), lambda l, j: (l, 0, j)),
            pl.BlockSpec((None, 1, tn), lambda l, j: (l, 0, j)),
        ],
        out_specs=pl.BlockSpec((None, 8, tn), lambda l, j: (l, 0, j)),
        out_shape=jax.ShapeDtypeStruct((L, 8, D6), F32),
        compiler_params=_cparams(("parallel", "parallel")),
        name="adaln_mod",
    )(cvec, w_mod, b_mod.reshape(L, 1, D6))


def _mod_spec(comp, tiles_per_batch, n_batch, D):
    return pl.BlockSpec(
        (None, None, 1, D),
        lambda i, *_: (jnp.minimum(i // tiles_per_batch, n_batch), comp, 0, 0))


def _proj_kernel(*refs, n_lat_tiles):
    if n_lat_tiles is None:
        x_ref, g_ref, sh_ref, sc_ref, w_ref, h_ref, o_ref, h_scr = refs
    else:
        xl_ref, xc_ref, g_ref, sh_ref, sc_ref, w_ref, h_ref, o_ref, h_scr = refs

    @pl.when(pl.program_id(1) == 0)
    def _():
        if n_lat_tiles is None:
            x = x_ref[...]
        else:
            x = jnp.where(pl.program_id(0) < n_lat_tiles, xl_ref[...], xc_ref[...])
        ms = jnp.mean(x * x, axis=-1, keepdims=True)
        xn = x * lax.rsqrt(ms + EPS) * g_ref[...]
        h = (xn * (1.0 + sc_ref[...]) + sh_ref[...]).astype(BF)
        h_scr[...] = h
        h_ref[...] = h

    o_ref[...] = _dot(h_scr[...], w_ref[...])


def _split_row_specs(arrays, tm, width, single_buffer=False):
    mode = dict(pipeline_mode=pl.Buffered(1)) if single_buffer else {}
    if len(arrays) == 1:
        return [pl.BlockSpec((tm, width), lambda i, j: (i, 0), **mode)], None
    n_lat_tiles = arrays[0].shape[0] // tm
    return [pl.BlockSpec((tm, width), lambda i, j: (jnp.minimum(i, n_lat_tiles - 1), 0), **mode),
            pl.BlockSpec((tm, width), lambda i, j: (jnp.maximum(i - n_lat_tiles, 0), 0),
                         pipeline_mode=pl.Buffered(1))], n_lat_tiles


def _proj_call(xs, norm_g, modl, w_in_p, n_batch, n_lat, tm, tn):
    D = xs[0].shape[1]
    R = sum(a.shape[0] for a in xs)
    P = w_in_p.shape[1]
    tpb = n_lat // tm
    if tn == P:
        w_spec = pl.BlockSpec((D, tn), lambda i, j: (0, j), pipeline_mode=pl.Buffered(1))
    else:
        w_spec = pl.BlockSpec((D, tn), lambda i, j: (0, j))
    row_spec = pl.BlockSpec((tm, D), lambda i, j: (i, 0))
    out_specs = [row_spec, pl.BlockSpec((tm, tn), lambda i, j: (i, j))]
    out_shape = [jax.ShapeDtypeStruct((R, D), BF), jax.ShapeDtypeStruct((R, P), F32)]
    x_specs, n_lat_tiles = _split_row_specs(xs, tm, D)
    return pl.pallas_call(
        functools.partial(_proj_kernel, n_lat_tiles=n_lat_tiles),
        grid=(R // tm, P // tn),
        in_specs=x_specs + [
            pl.BlockSpec((1, D), lambda i, j: (0, 0)),
            _mod_spec(0, tpb, n_batch, D),
            _mod_spec(1, tpb, n_batch, D),
            w_spec,
        ],
        out_specs=out_specs,
        out_shape=out_shape,
        scratch_shapes=[pltpu.VMEM((tm, D), BF)],
        compiler_params=_cparams(("parallel", "arbitrary")),
        name="norm_mod_proj",
    )(*xs, norm_g.reshape(1, D), modl, modl, w_in_p)


def _sumsq(x):
    xx = None
    for c in range(x.shape[1] // LANES):
        b = x[:, c * LANES:(c + 1) * LANES]
        xx = b * b if xx is None else xx + b * b
    hi = xx.astype(BF)
    lo = (xx - hi.astype(F32)).astype(BF)
    ones = jnp.ones((LANES, LANES), BF)
    return _dot(hi, ones) + _dot(lo, ones)


def _rms(x, g, n):
    r = lax.rsqrt(_sumsq(x) / n + EPS)
    blocks = [x[:, c * LANES:(c + 1) * LANES] * r * g[:, c * LANES:(c + 1) * LANES]
              for c in range(x.shape[1] // LANES)]
    return blocks[0] if len(blocks) == 1 else jnp.concatenate(blocks, axis=1)


def _rope(x, cos, sin_signed, half):
    lane = lax.broadcasted_iota(jnp.int32, x.shape, 1)
    first = (lane & (2 * half - 1)) < half
    xr = jnp.where(first, pltpu.roll(x, LANES - half, 1), pltpu.roll(x, half, 1))
    return x * cos + xr * sin_signed


def _heads_kernel(p_ref, ca_ref, sa_ref, cm_ref, sm_ref,
                  qa_g, ka_g, qw_g, kw_g, qn_g, kn_g, qm_g, km_g, cqn_g, ckvn_g, wuq_ref, wukv_ref,
                  qa_ref, ka_ref, va_ref, qm_ref, km_ref, vm_ref,
                  qw_ref, kw_ref, vw_ref, qn_ref, kn_ref, vn_ref):
    hd = HEAD_DIM
    sc128 = HEAD_DIM ** -0.5 * LOG2E
    sc192 = MLA_QK ** -0.5 * LOG2E
    ca, sa, cm, sm = ca_ref[...], sa_ref[...], cm_ref[...], sm_ref[...]

    def sl(off, h, w=hd):
        return p_ref[:, off + h * w: off + (h + 1) * w]

    ones = jnp.ones((p_ref.shape[0], hd), BF)

    def put_values(v_out, v_off, n_heads):
        for h in range(n_heads):
            v_out[:, 2 * h * hd:(2 * h + 1) * hd] = sl(v_off, h).astype(BF)
            v_out[:, (2 * h + 1) * hd:(2 * h + 2) * hd] = ones

    for (q_off, k_off, v_off, qg, kg, q_out, k_out, v_out) in (
            (P_AQ, P_AK, P_AV, qa_g, ka_g, qa_ref, ka_ref, va_ref),
            (P_WQ, P_WK, P_WV, qw_g, kw_g, qw_ref, kw_ref, vw_ref)):
        for h in range(4):
            q = _rope(_rms(sl(q_off, h), qg[...], hd), ca, sa, 32)
            q_out[:, h * hd:(h + 1) * hd] = (q * sc128).astype(BF)
        for h in range(2):
            k = _rope(_rms(sl(k_off, h), kg[...], hd), ca, sa, 32)
            k_out[:, h * hd:(h + 1) * hd] = k.astype(BF)
        put_values(v_out, v_off, 2)

    for h in range(4):
        qn_ref[:, h * hd:(h + 1) * hd] = (_rms(sl(P_NQ, h), qn_g[...], hd) * sc128).astype(BF)
        kn_ref[:, h * hd:(h + 1) * hd] = _rms(sl(P_NK, h), kn_g[...], hd).astype(BF)
    put_values(vn_ref, P_NV, 4)

    cq = _rms(p_ref[:, P_MCQ:P_MCQ + 384], cqn_g[...], 384)
    qf = _dot(cq.astype(BF), wuq_ref[...])
    ckv = _rms(p_ref[:, P_MCKV:P_MCKV + 256], ckvn_g[...], 256)
    kvf = _dot(ckv.astype(BF), wukv_ref[...])
    kr = p_ref[:, P_MKR:P_MKR + hd]
    kr_ss = _sumsq(kr)
    qg0, qg1 = qm_g[:, :hd], qm_g[:, hd:]
    kg0, kg1 = km_g[:, :hd], km_g[:, hd:]
    for h in range(4):
        q0 = qf[:, h * MLA_PAD: h * MLA_PAD + hd]
        q1 = qf[:, h * MLA_PAD + hd: (h + 1) * MLA_PAD]
        r = lax.rsqrt(_sumsq(qf[:, h * MLA_PAD:(h + 1) * MLA_PAD]) / MLA_QK + EPS)
        qm_ref[:, h * MLA_PAD: h * MLA_PAD + hd] = (q0 * r * qg0 * sc192).astype(BF)
        qm_ref[:, h * MLA_PAD + hd: (h + 1) * MLA_PAD] = (
            _rope(q1 * r * qg1, cm, sm, 16) * sc192).astype(BF)
        k0 = kvf[:, h * 2 * hd: h * 2 * hd + hd]
        r = lax.rsqrt((_sumsq(k0) + kr_ss) / MLA_QK + EPS)
        km_ref[:, h * MLA_PAD: h * MLA_PAD + hd] = (k0 * r * kg0).astype(BF)
        km_ref[:, h * MLA_PAD + hd: (h + 1) * MLA_PAD] = _rope(kr * r * kg1, cm, sm, 16).astype(BF)
        vm_ref[:, 2 * h * hd:(2 * h + 1) * hd] = kvf[:, h * 2 * hd + hd: (h + 1) * 2 * hd].astype(BF)
        vm_ref[:, (2 * h + 1) * hd:(2 * h + 2) * hd] = ones


def _heads_call(proj, tabs, gains, wuq_p, wukv, n_batch, n_lat, tm):
    R, P = proj.shape
    tpb = n_lat // tm
    tab_spec = pl.BlockSpec((tm, LANES), lambda i: (jnp.where(i < n_batch * tpb, i % tpb, tpb), 0))

    def full(a):
        return pl.BlockSpec(a.shape, lambda i: (0,) * a.ndim)

    widths = (512, 256, 512, 4 * MLA_PAD, 4 * MLA_PAD, 1024, 512, 256, 512, 512, 512, 1024)
    return pl.pallas_call(
        _heads_kernel,
        grid=(R // tm,),
        in_specs=[pl.BlockSpec((tm, P), lambda i: (i, 0))] + [tab_spec] * 4
        + [full(g) for g in gains] + [full(wuq_p), full(wukv)],
        out_specs=[pl.BlockSpec((tm, w), lambda i: (i, 0)) for w in widths],
        out_shape=[jax.ShapeDtypeStruct((R, w), BF) for w in widths],
        compiler_params=_cparams(("parallel",)),
        name="head_prep",
    )(proj, *tabs, *gains, wuq_p, wukv)


def _flash_kernel(q_ref, k_ref, v_ref, kc_ref, vc_ref, o_ref, q_scr, s_scr, m_scr, acc_scr,
                  *, G, d, dv, w, unroll):
    tq = q_ref.shape[0]
    n_blk = k_ref.shape[0] // w
    for g in range(G):
        q_scr[g * tq:(g + 1) * tq, :] = q_ref[:, g * d:(g + 1) * d]
    m_scr[...] = jnp.full_like(m_scr, NEG)
    acc_scr[...] = jnp.zeros_like(acc_scr)

    def scores(k):
        return _dot_nt(q_scr[...], k)

    def absorb(s, v):
        m_prev = m_scr[...]
        m_next = jnp.maximum(m_prev, s.max(axis=1, keepdims=True))
        alpha = jnp.exp2(m_prev - m_next)
        p = jnp.concatenate(
            [jnp.exp2(s[:, c * LANES:(c + 1) * LANES] - m_next) for c in range(s.shape[1] // LANES)],
            axis=1)
        pv = _dot(p.astype(BF), v)
        acc_scr[...] = jnp.concatenate([alpha, alpha], axis=1) * acc_scr[...] + pv
        m_scr[...] = m_next

    n_c = kc_ref.shape[0]
    s_scr[1, :, 0:n_c] = scores(kc_ref[...])
    s_scr[0] = scores(k_ref[0:w, :])
    absorb(s_scr[1, :, 0:n_c], vc_ref[...])

    def group(jj, carry):
        base = jj * (unroll * w)
        for u in range(unroll):
            cur = pl.multiple_of(base + u * w, w)
            nxt = pl.multiple_of(jnp.minimum(base + (u + 1) * w, (n_blk - 1) * w), w)
            s_scr[(u + 1) % 2] = scores(k_ref[pl.ds(nxt, w), :])
            absorb(s_scr[u % 2], v_ref[pl.ds(cur, w), :])
        return carry

    lax.fori_loop(0, n_blk // unroll, group, 0)

    acc = acc_scr[...]
    o = acc[:, :dv] / acc[:, dv:]
    for g in range(G):
        o_ref[:, g * dv:(g + 1) * dv] = o[g * tq:(g + 1) * tq].astype(BF)


def _flash_call(q, k, v, n_batch, n_lat, n_ctx, G, d, tq, w, name):
    dv = LANES
    hkv = k.shape[1] // d
    nq = n_lat // tq
    cb = (n_batch * n_lat) // n_ctx
    n_blk = n_lat // w
    unroll = next(u for u in (8, 4, 2) if n_blk % u == 0)
    assert n_ctx <= w
    return pl.pallas_call(
        functools.partial(_flash_kernel, G=G, d=d, dv=dv, w=w, unroll=unroll),
        grid=(n_batch, hkv, nq),
        in_specs=[
            pl.BlockSpec((tq, G * d), lambda b, h, i: (b * nq + i, h)),
            pl.BlockSpec((n_lat, d), lambda b, h, i: (b, h)),
            pl.BlockSpec((n_lat, 2 * dv), lambda b, h, i: (b, h)),
            pl.BlockSpec((n_ctx, d), lambda b, h, i: (cb + b, h)),
            pl.BlockSpec((n_ctx, 2 * dv), lambda b, h, i: (cb + b, h)),
        ],
        out_specs=pl.BlockSpec((tq, G * dv), lambda b, h, i: (b * nq + i, h)),
        out_shape=jax.ShapeDtypeStruct((n_batch * n_lat, hkv * G * dv), BF),
        scratch_shapes=[pltpu.VMEM((G * tq, d), BF), pltpu.VMEM((2, G * tq, w), F32),
                        pltpu.VMEM((G * tq, LANES), F32), pltpu.VMEM((G * tq, 2 * dv), F32)],
        compiler_params=_cparams(("parallel", "parallel", "parallel")),
        name=name,
    )(q, k, v, k, v)


def _window_kernel(sink_ref, q_ref, kp_ref, km_ref, kn_ref, vp_ref, vm_ref, vn_ref, kc_ref, vc_ref,
                   o_ref, s_scr, sc_scr, *, G):
    hd = HEAD_DIM
    i = pl.program_id(1)
    ni = pl.num_programs(1)
    tq = q_ref.shape[0]
    n_heads = q_ref.shape[1] // hd
    r = lax.broadcasted_iota(jnp.int32, (tq, tq + 2 * WINDOW), 0)
    c = lax.broadcasted_iota(jnp.int32, (tq, tq + 2 * WINDOW), 1)
    rel = c - WINDOW - r
    lo = jnp.where(i > 0, 0, WINDOW)
    hi = jnp.where(i < ni - 1, tq + 2 * WINDOW, tq + WINDOW)
    valid = (jnp.abs(rel) <= WINDOW) & (c >= lo) & (c < hi)

    def kv_cat(p_ref, m_ref, n_ref, kvh, w):
        cols = slice(kvh * w, (kvh + 1) * w)
        return jnp.concatenate([p_ref[:, cols], m_ref[:, cols], n_ref[:, cols]], axis=0)

    for qh in range(n_heads):
        kvh = qh // G
        q = q_ref[:, qh * hd:(qh + 1) * hd]
        s_scr[qh] = jnp.where(valid, _dot_nt(q, kv_cat(kp_ref, km_ref, kn_ref, kvh, hd)), NEG)
        sc_scr[qh] = _dot_nt(q, kc_ref[:, kvh * hd:(kvh + 1) * hd])
    for qh in range(n_heads):
        kvh = qh // G
        s = s_scr[qh]
        sc = sc_scr[qh]
        sk = sink_ref[qh] * LOG2E
        m = jnp.maximum(jnp.max(s, axis=-1, keepdims=True), jnp.max(sc, axis=-1, keepdims=True))
        m = jnp.maximum(m, sk)
        p = jnp.exp2(s - m)
        pc = jnp.exp2(sc - m)
        o = (_dot(p.astype(BF), kv_cat(vp_ref, vm_ref, vn_ref, kvh, 2 * hd))
             + _dot(pc.astype(BF), vc_ref[:, kvh * 2 * hd:(kvh + 1) * 2 * hd]))
        l = o[:, hd:] + jnp.exp2(sk - m)
        o_ref[:, qh * hd:(qh + 1) * hd] = (o[:, :hd] / l).astype(BF)


def _window_call(sink, q, k, v, n_batch, n_lat, n_ctx, G, tq):
    hd = HEAD_DIM
    hkv = k.shape[1] // hd
    nq = n_lat // tq
    bpt = tq // WINDOW
    nblk = k.shape[0] // WINDOW
    cb = (n_batch * n_lat) // n_ctx

    def prev(b, i):
        return (jnp.maximum((b * nq + i) * bpt - 1, 0), 0)

    def nxt(b, i):
        return (jnp.minimum((b * nq + i + 1) * bpt, nblk - 1), 0)

    def main(b, i):
        return (b * nq + i, 0)

    def ctxb(b, i):
        return (cb + b, 0)

    kw, vw, qw = hkv * hd, hkv * 2 * hd, hkv * G * hd
    return pl.pallas_call(
        functools.partial(_window_kernel, G=G),
        grid=(n_batch, nq),
        in_specs=[
            pl.BlockSpec(memory_space=pltpu.SMEM),
            pl.BlockSpec((tq, qw), main),
            pl.BlockSpec((WINDOW, kw), prev), pl.BlockSpec((tq, kw), main), pl.BlockSpec((WINDOW, kw), nxt),
            pl.BlockSpec((WINDOW, vw), prev), pl.BlockSpec((tq, vw), main), pl.BlockSpec((WINDOW, vw), nxt),
            pl.BlockSpec((n_ctx, kw), ctxb), pl.BlockSpec((n_ctx, vw), ctxb),
        ],
        out_specs=pl.BlockSpec((tq, qw), main),
        out_shape=jax.ShapeDtypeStruct((n_batch * n_lat, qw), BF),
        scratch_shapes=[pltpu.VMEM((hkv * G, tq, tq + 2 * WINDOW), F32), pltpu.VMEM((hkv * G, tq, n_ctx), F32)],
        compiler_params=_cparams(("parallel", "parallel")),
        name="window_attn",
    )(sink, q, k, k, k, v, v, v, k, v)


NA_TQ = 8 * GRID_W
NA_KB = 4 * GRID_W
NA_NKB = 4


def _na_kernel(q_ref, k0, k1, k2, k3, v0, v1, v2, v3, kc_ref, vc_ref, bias_ref, o_ref, s_scr, sc_scr):
    hd = HEAD_DIM
    n_heads = q_ref.shape[1] // hd

    def cat(refs, h, w):
        return jnp.concatenate([r[:, h * w:(h + 1) * w] for r in refs], axis=0)

    for h in range(n_heads):
        q = q_ref[:, h * hd:(h + 1) * hd]
        s_scr[h] = _dot_nt(q, cat((k0, k1, k2, k3), h, hd)) + bias_ref[h]
        sc_scr[h] = _dot_nt(q, kc_ref[:, h * hd:(h + 1) * hd])
    for h in range(n_heads):
        s = s_scr[h]
        sc = sc_scr[h]
        m = jnp.maximum(jnp.max(s, axis=-1, keepdims=True), jnp.max(sc, axis=-1, keepdims=True))
        p = jnp.exp2(s - m)
        pc = jnp.exp2(sc - m)
        o = (_dot(p.astype(BF), cat((v0, v1, v2, v3), h, 2 * hd))
             + _dot(pc.astype(BF), vc_ref[:, h * 2 * hd:(h + 1) * 2 * hd]))
        o_ref[:, h * hd:(h + 1) * hd] = (o[:, :hd] / o[:, hd:]).astype(BF)


def _na_bias_tables(rpb, n_rows):
    H = rpb.shape[0]
    J = n_rows // 8
    nb = n_rows // 4
    n_dy, n_dx = 2 * NA_WIN_H - 1, 2 * NA_WIN_W - 1
    cq = np.arange(GRID_W)
    kc = np.arange(GRID_W)
    cs = np.clip(cq - NA_WIN_W // 2, 0, GRID_W - NA_WIN_W)
    dx = kc[None, :] - cq[:, None] + NA_WIN_W - 1
    col_ok = (kc[None, :] >= cs[:, None]) & (kc[None, :] < cs[:, None] + NA_WIN_W)
    dx1h = np.zeros((n_dx, GRID_W * GRID_W), np.float32)
    dx1h[np.clip(dx, 0, n_dx - 1).reshape(-1), np.arange(GRID_W * GRID_W)] = 1.0
    blocks = jnp.einsum("hyx,xb->hyb", rpb.astype(F32) * LOG2E, jnp.asarray(dx1h),
                        precision=lax.Precision.HIGHEST)
    blocks = jnp.where(jnp.asarray(col_ok.reshape(-1))[None, None, :], blocks, NEG)
    blocks = jnp.concatenate([blocks, jnp.full((H, 1, GRID_W * GRID_W), NEG, F32)], axis=1)
    blocks = blocks.reshape(H, n_dy + 1, GRID_W, GRID_W)
    idx = np.zeros((3, 8, 4 * NA_NKB), np.int32)
    for v, jv in enumerate((0, min(1, J - 1), J - 1)):
        r = 8 * jv + np.arange(8)
        kb_un = 2 * jv - 1 + np.arange(NA_NKB)
        kb = np.clip(kb_un, 0, nb - 1)
        krow = (4 * kb[:, None] + np.arange(4)[None, :]).reshape(-1)
        krow_dup = np.repeat(kb != kb_un, 4)
        rs = np.clip(r - NA_WIN_H // 2, 0, n_rows - NA_WIN_H)
        row_ok = (krow[None, :] >= rs[:, None]) & (krow[None, :] < rs[:, None] + NA_WIN_H) & ~krow_dup[None, :]
        dy = krow[None, :] - r[:, None] + NA_WIN_H - 1
        idx[v] = np.where(row_ok, np.clip(dy, 0, n_dy - 1), n_dy)
    t = jnp.take(blocks, jnp.asarray(idx.reshape(-1)), axis=1)
    t = t.reshape(H, 3, 8, 4 * NA_NKB, GRID_W, GRID_W).transpose(0, 1, 2, 4, 3, 5)
    return t.reshape(H, 3, NA_TQ, NA_NKB * NA_KB)


def _na_call(q, k, v, bias, n_batch, n_lat, n_ctx):
    hd = HEAD_DIM
    H = q.shape[1] // hd
    J = n_lat // NA_TQ
    nb = n_lat // NA_KB
    cb = (n_batch * n_lat) // n_ctx

    def kblk(t):
        return lambda b, j: (b * nb + jnp.clip(2 * j - 1 + t, 0, nb - 1), 0)

    def qmap(b, j):
        return (b * J + j, 0)

    def ctxb(b, j):
        return (cb + b, 0)

    def bmap(b, j):
        return (0, jnp.where(j == 0, 0, jnp.where(j == J - 1, 2, 1)), 0, 0)

    n_keys = NA_NKB * NA_KB
    return pl.pallas_call(
        _na_kernel,
        grid=(n_batch, J),
        in_specs=[pl.BlockSpec((NA_TQ, H * hd), qmap)]
        + [pl.BlockSpec((NA_KB, H * hd), kblk(t)) for t in range(NA_NKB)]
        + [pl.BlockSpec((NA_KB, H * 2 * hd), kblk(t)) for t in range(NA_NKB)]
        + [pl.BlockSpec((n_ctx, H * hd), ctxb), pl.BlockSpec((n_ctx, H * 2 * hd), ctxb),
           pl.BlockSpec((H, None, NA_TQ, n_keys), bmap)],
        out_specs=pl.BlockSpec((NA_TQ, H * hd), qmap),
        out_shape=jax.ShapeDtypeStruct((n_batch * n_lat, H * hd), BF),
        scratch_shapes=[pltpu.VMEM((H, NA_TQ, n_keys), F32), pltpu.VMEM((H, NA_TQ, n_ctx), F32)],
        compiler_params=_cparams(("parallel", "parallel")),
        name="neighbourhood_attn",
    )(q, k, k, k, k, v, v, v, v, k, v, bias)


def _ctx_attn_kernel(sink_ref, q_ref, k_ref, v_ref, o_ref, *, use_sink):
    s = _dot_nt(q_ref[...], k_ref[...])
    m = jnp.max(s, axis=-1, keepdims=True)
    if use_sink:
        sk = sink_ref[pl.program_id(1)] * LOG2E
        m = jnp.maximum(m, sk)
    p = jnp.exp2(s - m)
    l = jnp.sum(p, axis=-1, keepdims=True)
    if use_sink:
        l = l + jnp.exp2(sk - m)
    o_ref[...] = (_dot(p.astype(BF), v_ref[...]) / l).astype(BF)


def _ctx_attn_call(sink, q, k, v, n_batch, n_lat, n_ctx, G, d, dv, use_sink, name):
    v_stride = 2
    H = q.shape[1] // d
    cb = (n_batch * n_lat) // n_ctx
    return pl.pallas_call(
        functools.partial(_ctx_attn_kernel, use_sink=use_sink),
        grid=(n_batch, H),
        in_specs=[
            pl.BlockSpec(memory_space=pltpu.SMEM),
            pl.BlockSpec((n_ctx, d), lambda b, h: (cb + b, h)),
            pl.BlockSpec((n_ctx, d), lambda b, h: (cb + b, h // G)),
            pl.BlockSpec((n_ctx, dv), lambda b, h: (cb + b, (h // G) * v_stride)),
        ],
        out_specs=pl.BlockSpec((n_ctx, dv), lambda b, h: (b, h)),
        out_shape=jax.ShapeDtypeStruct((n_batch * n_ctx, H * dv), BF),
        compiler_params=_cparams(("parallel", "parallel")),
        name=name,
    )(sink, q, k, v)


def _merge_kernel(*refs, n_lat_tiles):
    n_row_in = 5 if n_lat_tiles is None else 10
    row_refs = refs[:1 + n_row_in]
    (wg_ref, bg_ref, wb_ref, wo_ref, g1_ref, nf_ref, sh2_ref, sc2_ref, wr_ref, br_ref,
     xo_ref, h2_ref, ei_ref, wt_ref, y_scr) = refs[1 + n_row_in:]
    if n_lat_tiles is None:
        x_ref, h_ref = row_refs[0], row_refs[1]
        mixer_rows = [r.__getitem__ for r in row_refs[2:]]
        x_rows = x_ref.__getitem__
    else:
        h_ref = row_refs[2]
        is_lat = pl.program_id(0) < n_lat_tiles

        def pick(l_ref, c_ref):
            return lambda idx: jnp.where(is_lat, l_ref[idx], c_ref[idx])

        x_rows = pick(row_refs[0], row_refs[1])
        mixer_rows = [pick(row_refs[3 + 2 * i], row_refs[4 + 2 * i]) for i in range(4)]
    j = pl.program_id(1)

    @pl.when(j == 0)
    def _():
        y_scr[...] = jnp.zeros_like(y_scr)

    h = h_ref[...]
    acc = None
    for i, rows_of in enumerate(mixer_rows):
        gate = _sigmoid(_dot(h, wg_ref[i]) + bg_ref[i])
        y = gate * _dot(rows_of(...), wb_ref[i])
        acc = y if acc is None else acc + y
    y_scr[...] += _dot(acc.astype(BF), wo_ref[...])

    @pl.when(j == pl.num_programs(1) - 1)
    def _():
        x = x_rows(...) + g1_ref[...] * y_scr[...]
        xo_ref[...] = x
        ms = jnp.mean(x * x, axis=-1, keepdims=True)
        h2 = x * lax.rsqrt(ms + EPS) * nf_ref[...]
        h2 = h2 * (1.0 + sc2_ref[...]) + sh2_ref[...]
        h2_ref[...] = h2
        hi = h2.astype(BF)
        lo = (h2 - hi.astype(F32)).astype(BF)
        both = _dot(hi, wr_ref[...])
        logits = both[:, :LANES] + (_dot(lo, wr_ref[:, :LANES]) + both[:, LANES:])
        lt = logits.T[:N_EXPERTS, :]
        scores = _sigmoid(lt)
        biased = scores + br_ref[...]
        row = lax.broadcasted_iota(jnp.int32, lt.shape, 0)
        row_f = row.astype(F32)
        ninf = -jnp.inf

        def top2(vals):
            t1 = jnp.max(vals, axis=0, keepdims=True)
            i1 = jnp.min(jnp.where(vals == t1, row_f, float(N_EXPERTS)), axis=0, keepdims=True)
            vals2 = jnp.where(row_f == i1, ninf, vals)
            t2 = jnp.max(vals2, axis=0, keepdims=True)
            i2 = jnp.min(jnp.where(vals2 == t2, row_f, float(N_EXPERTS)), axis=0, keepdims=True)
            return t1, i1, t2, i2

        per = N_EXPERTS // N_GROUPS
        best, gi = None, None
        for g in range(N_GROUPS):
            ing = (row >= g * per) & (row < (g + 1) * per)
            t1, _, t2, _ = top2(jnp.where(ing, biased, ninf))
            gs = t1 + t2
            if best is None:
                best, gi = gs, jnp.zeros_like(gs)
            else:
                better = gs > best
                best = jnp.where(better, gs, best)
                gi = jnp.where(better, float(g), gi)
        row_grp = (row >> 2).astype(F32)
        _, i1, _, i2 = top2(jnp.where(row_grp == gi, biased, NEG))
        w1 = jnp.sum(jnp.where(row_f == i1, scores, 0.0), axis=0, keepdims=True)
        w2 = jnp.sum(jnp.where(row_f == i2, scores, 0.0), axis=0, keepdims=True)
        den = w1 + w2
        out_row = lax.broadcasted_iota(jnp.int32, ei_ref.shape, 0)
        ei_ref[...] = jnp.where(out_row == 0, i1, jnp.where(out_row == 1, i2, 0.0)).astype(jnp.int32)
        wt_ref[...] = jnp.where(out_row == 0, w1 / den, jnp.where(out_row == 1, w2 / den, 0.0))


def _merge_call(xs, h, outs, wg, bg, wb, wo, modl, norm_f, wr_p, br_p, rows, n_batch, n_lat, tm, tn):
    D = xs[0].shape[1]
    bw = wb.shape[1]
    tpb = n_lat // tm

    def rowmap(i, j):
        return (i, 0)

    x_specs, n_lat_tiles = _split_row_specs(xs, tm, D)
    o_specs, o_args = [], []
    for o in outs:
        assert len(o) == len(xs)
        o_specs += _split_row_specs(o, tm, bw)[0]
        o_args += list(o)
    return pl.pallas_call(
        functools.partial(_merge_kernel, n_lat_tiles=n_lat_tiles),
        grid=(rows // tm, D // tn),
        in_specs=x_specs + [pl.BlockSpec((tm, D), rowmap)] + o_specs + [
            pl.BlockSpec((4, D, tn), lambda i, j: (0, 0, j)),
            pl.BlockSpec((4, 1, tn), lambda i, j: (0, 0, j)),
            pl.BlockSpec((4, bw, tn), lambda i, j: (0, 0, j)),
            pl.BlockSpec((tn, D), lambda i, j: (j, 0)),
            _mod_spec(2, tpb, n_batch, D),
            pl.BlockSpec((1, D), lambda i, j: (0, 0)),
            _mod_spec(3, tpb, n_batch, D),
            _mod_spec(4, tpb, n_batch, D),
            pl.BlockSpec((D, 2 * LANES), lambda i, j: (0, 0)),
            pl.BlockSpec((N_EXPERTS, 1), lambda i, j: (0, 0)),
        ],
        out_specs=[
            pl.BlockSpec((tm, D), rowmap),
            pl.BlockSpec((tm, D), rowmap),
            pl.BlockSpec((8, tm), lambda i, j: (0, i)),
            pl.BlockSpec((8, tm), lambda i, j: (0, i)),
        ],
        out_shape=[
            jax.ShapeDtypeStruct((rows, D), F32),
            jax.ShapeDtypeStruct((rows, D), F32),
            jax.ShapeDtypeStruct((8, rows), jnp.int32),
            jax.ShapeDtypeStruct((8, rows), F32),
        ],
        scratch_shapes=[pltpu.VMEM((tm, D), F32)],
        compiler_params=_cparams(("parallel", "arbitrary"), VMEM_LIMIT_MERGE),
        name="merge_residual_router",
    )(*xs, h, *o_args, wg, bg, wb, wo, modl, norm_f.reshape(1, D), modl, modl, wr_p, br_p)


def _row_dma_issue(n_rows, make_copy):
    def issue(r, carry):
        for k in range(2):
            make_copy(r, k).start()
        return carry

    lax.fori_loop(0, n_rows, issue, 0, unroll=8)


def _row_dma_drain(n_rows, make_copy):
    def drain(r, carry):
        for k in range(2):
            make_copy(r, k).wait()
        return carry

    lax.fori_loop(0, n_rows, drain, 0, unroll=8)


def _row_dma_loop(n_rows, make_copy):
    _row_dma_issue(n_rows, make_copy)
    _row_dma_drain(n_rows, make_copy)


def _scatter_kernel(fill_ref, pos_ref, h_ref, xs_ref, zbuf, sem, zsem):
    tg = zbuf.shape[0]

    @pl.when(pl.program_id(0) == 0)
    def _():
        zbuf[...] = jnp.zeros_like(zbuf)

        def fill_copy(t):
            return pltpu.make_async_copy(zbuf, xs_ref.at[pl.ds(pl.multiple_of(t * tg, tg), tg), :], zsem)

        def start_fill(t, carry):
            @pl.when(fill_ref[t] != 0)
            def _():
                fill_copy(t).start()
            return carry

        def wait_fill(t, carry):
            @pl.when(fill_ref[t] != 0)
            def _():
                fill_copy(t).wait()
            return carry

        lax.fori_loop(0, fill_ref.shape[0], start_fill, 0)
        lax.fori_loop(0, fill_ref.shape[0], wait_fill, 0)

    def make_copy(r, k):
        return pltpu.make_async_copy(h_ref.at[pl.ds(r, 1), :],
                                     xs_ref.at[pl.ds(pos_ref[2 * r + k], 1), :], sem)

    _row_dma_loop(h_ref.shape[0], make_copy)


def _scatter_call(h2, pos, fill, n_slots, tm, tg):
    rows, D = h2.shape
    grid_spec = pltpu.PrefetchScalarGridSpec(
        num_scalar_prefetch=1,
        grid=(rows // tm,),
        in_specs=[
            pl.BlockSpec((2 * tm,), lambda i, f: (i,), memory_space=pltpu.SMEM),
            pl.BlockSpec((tm, D), lambda i, f: (i, 0)),
        ],
        out_specs=pl.BlockSpec(memory_space=pl.ANY),
        scratch_shapes=[pltpu.VMEM((tg, D), F32), pltpu.SemaphoreType.DMA(()), pltpu.SemaphoreType.DMA(())],
    )
    return pl.pallas_call(
        _scatter_kernel,
        grid_spec=grid_spec,
        out_shape=jax.ShapeDtypeStruct((n_slots, D), F32),
        compiler_params=_cparams(("arbitrary",)),
        name="moe_scatter_rows",
    )(fill, pos, h2)


def _experts_kernel(te_ref, nu_ref, x_ref, w1_ref, w3_ref, w2_ref, y_ref, w1_scr, w3_scr, w2_scr):
    i = pl.program_id(0)

    @pl.when((i == 0) | (te_ref[i] != te_ref[jnp.maximum(i - 1, 0)]))
    def _():
        w1_scr[...] = w1_ref[...].astype(BF)
        w3_scr[...] = w3_ref[...].astype(BF)
        w2_scr[...] = w2_ref[...].astype(BF)

    @pl.when(i < nu_ref[0])
    def _():
        x = x_ref[...].astype(BF)
        a = _dot(x, w1_scr[...])
        b = _dot(x, w3_scr[...])
        hid = (a * _sigmoid(a)) * b
        y_ref[...] = _dot(hid.astype(BF), w2_scr[...])

    @pl.when(i >= nu_ref[0])
    def _():
        y_ref[...] = jnp.zeros_like(y_ref)


def _experts_call(xs, tile_expert, n_used, w1, w3, w2, layer, tg):
    S, D = xs.shape
    de = w1.shape[3]
    grid_spec = pltpu.PrefetchScalarGridSpec(
        num_scalar_prefetch=2,
        grid=(S // tg,),
        in_specs=[
            pl.BlockSpec((tg, D), lambda i, te, nu: (i, 0)),
            pl.BlockSpec((None, None, D, de), lambda i, te, nu: (layer, te[i], 0, 0)),
            pl.BlockSpec((None, None, D, de), lambda i, te, nu: (layer, te[i], 0, 0)),
            pl.BlockSpec((None, None, de, D), lambda i, te, nu: (layer, te[i], 0, 0)),
        ],
        out_specs=pl.BlockSpec((tg, D), lambda i, te, nu: (i, 0)),
        scratch_shapes=[pltpu.VMEM((D, de), BF), pltpu.VMEM((D, de), BF), pltpu.VMEM((de, D), BF)],
    )
    return pl.pallas_call(
        _experts_kernel,
        grid_spec=grid_spec,
        out_shape=jax.ShapeDtypeStruct((S, D), F32),
        compiler_params=_cparams(("arbitrary",)),
        name="moe_experts",
    )(tile_expert, n_used, xs, w1, w3, w2)


def _combine_kernel(pos_ref, pos_next_ref, x_ref, wt_ref, g2_ref, ys_ref, o_ref, buf, sem):
    i = pl.program_id(0)
    n = pl.num_programs(0)
    tm = x_ref.shape[0]
    slot = i % 2

    def gather(p_ref, s):
        def make_copy(r, k):
            return pltpu.make_async_copy(ys_ref.at[pl.ds(p_ref[2 * r + k], 1), :],
                                         buf.at[s, k, pl.ds(r, 1), :], sem.at[s])
        return make_copy

    @pl.when(i == 0)
    def _():
        _row_dma_issue(tm, gather(pos_ref, 0))

    @pl.when(i + 1 < n)
    def _():
        _row_dma_issue(tm, gather(pos_next_ref, 1 - slot))

    _row_dma_drain(tm, gather(pos_ref, slot))
    wt = jnp.concatenate([wt_ref[...], jnp.zeros((LANES - 8, tm), F32)], axis=0).T
    moe = wt[:, 0:1] * buf[slot, 0] + wt[:, 1:2] * buf[slot, 1]
    o_ref[...] = x_ref[...] + g2_ref[...] * moe


def _combine_call(X, ys, pos, wts, modl, n_batch, n_lat, tm):
    rows, D = X.shape
    tpb = n_lat // tm
    n_steps = rows // tm
    return pl.pallas_call(
        _combine_kernel,
        grid=(n_steps,),
        in_specs=[
            pl.BlockSpec((2 * tm,), lambda i: (i,), memory_space=pltpu.SMEM),
            pl.BlockSpec((2 * tm,), lambda i: (jnp.minimum(i + 1, n_steps - 1),), memory_space=pltpu.SMEM),
            pl.BlockSpec((tm, D), lambda i: (i, 0)),
            pl.BlockSpec((8, tm), lambda i: (0, i)),
            _mod_spec(5, tpb, n_batch, D),
            pl.BlockSpec(memory_space=pl.ANY),
        ],
        out_specs=pl.BlockSpec((tm, D), lambda i: (i, 0)),
        out_shape=jax.ShapeDtypeStruct((rows, D), F32),
        scratch_shapes=[pltpu.VMEM((2, 2, tm, D), F32), pltpu.SemaphoreType.DMA((2,))],
        compiler_params=_cparams(("arbitrary",)),
        name="moe_combine_residual",
    )(pos, pos, X, wts, modl, ys)


def _route_positions(eidx, tg):
    rows = eidx.shape[0]
    e_flat = eidx.reshape(-1)
    onehot = (e_flat[:, None] == jnp.arange(N_EXPERTS, dtype=jnp.int32)[None, :]).astype(jnp.int32)
    csum = jnp.cumsum(onehot, axis=0)
    counts = csum[-1]
    rank = jnp.sum((csum - onehot) * onehot, axis=1)
    padded = ((counts + tg - 1) // tg) * tg
    ends = jnp.cumsum(padded)
    offsets = ends - padded
    pos = jnp.sum(onehot * offsets[None, :], axis=1) + rank
    n_tiles = (2 * rows + N_EXPERTS * tg) // tg
    tile_start = jnp.arange(n_tiles, dtype=jnp.int32) * tg
    tile_expert = jnp.sum((tile_start[:, None] >= ends[None, :]).astype(jnp.int32), axis=1)
    n_used = (ends[-1] // tg).astype(jnp.int32)
    used = jnp.arange(n_tiles) < n_used
    te_1h = (tile_expert[:, None] == jnp.arange(N_EXPERTS, dtype=jnp.int32)[None, :]).astype(jnp.int32)
    real_end = jnp.sum(te_1h * (offsets + counts)[None, :], axis=1)
    fill = jnp.logical_not(used & (tile_start + tg <= real_end)).astype(jnp.int32)
    last_e = jnp.sum((((n_used - 1) * tg) >= ends).astype(jnp.int32))
    tile_expert = jnp.where(used, tile_expert, last_e).astype(jnp.int32)
    return pos.astype(jnp.int32), tile_expert, n_used.reshape(1), fill


def _rope_tables(n_lat, tm):
    n_rows = n_lat // GRID_W

    def tab(d2, axis):
        freqs = np.float32(ROPE_THETA) ** (-np.arange(d2, dtype=np.float32) / np.float32(d2))
        pos = np.arange(n_rows if axis == 0 else GRID_W, dtype=np.float32)
        ang = pos[:, None] * freqs[None, :]
        shape = (n_rows, GRID_W, d2)
        small = [np.cos(ang).astype(np.float32), np.sin(ang).astype(np.float32)]
        full = [jnp.broadcast_to(jnp.asarray(a)[:, None, :] if axis == 0 else jnp.asarray(a)[None, :, :], shape)
                for a in small]
        return tuple(a.reshape(n_lat, d2) for a in full)

    cr, sr = tab(32, 0)
    cc, sc = tab(32, 1)
    cos_a = jnp.concatenate([cr, cr, cc, cc], axis=-1)
    sin_a = jnp.concatenate([-sr, sr, -sc, sc], axis=-1)
    cr, sr = tab(16, 0)
    cc, sc = tab(16, 1)
    one, zero = jnp.ones((n_lat, 64), F32), jnp.zeros((n_lat, 64), F32)
    cos_m = jnp.concatenate([cr, cr, cc, cc, one], axis=-1)
    sin_m = jnp.concatenate([-sr, sr, -sc, sc, zero], axis=-1)
    ident_c, ident_s = jnp.ones((tm, LANES), F32), jnp.zeros((tm, LANES), F32)
    return tuple(jnp.concatenate([a, b], axis=0) for a, b in
                 ((cos_a, ident_c), (sin_a, ident_s), (cos_m, ident_c), (sin_m, ident_s)))


def _pad_mla_heads(w, n_heads):
    lead = w.shape[:-1]
    w = w.reshape(lead + (n_heads, MLA_QK))
    w = jnp.pad(w, [(0, 0)] * len(lead) + [(0, 0), (0, MLA_PAD - MLA_QK)])
    return w.reshape(lead + (n_heads * MLA_PAD,))


def kernel(x, c, ctx, c_ctx, w_mod, b_mod, norm_mix, norm_ffn, w_in, mla_qa_norm, mla_w_uq, mla_kva_norm,
           mla_w_ukv, qn_att, kn_att, qn_mla, kn_mla, qn_win, kn_win, qn_na, kn_na, win_sink, na_rpb,
           w_branch, w_gate, b_gate, w_out, w_router, b_router, moe_w1, moe_w3, moe_w2):
    B, N, D = x.shape
    n_ctx = ctx.shape[1]
    L = w_mod.shape[0]
    R = B * N + B * n_ctx
    tm = 512
    tm_moe = 256
    tg = 512
    tn_merge = min(256, D // 2)
    tq = min(512, N)
    tk = min(512, N // 2)

    x_lat, x_ctx = x.reshape(B * N, D), ctx.reshape(B * n_ctx, D)
    X = None
    cvec = jnp.zeros((8, D), F32).at[:B].set(c).at[B].set(c_ctx)
    mod = _mod_call(cvec, w_mod, b_mod, min(1024, D)).reshape(L, 8, 6, 1, D)
    tabs = _rope_tables(N, tm)

    wr_hi = w_router.astype(BF)
    wr_lo = (w_router - wr_hi.astype(F32)).astype(BF)
    lane_pad = ((0, 0), (0, LANES - N_EXPERTS))
    wr_p = jnp.concatenate([jnp.pad(wr_hi, lane_pad), jnp.pad(wr_lo, lane_pad)], axis=1)
    br_p = b_router.astype(F32).reshape(N_EXPERTS, 1)

    for l in range(L):
        last = l == L - 1
        rows = B * N if last else R
        modl = mod[l]
        w_in_l = w_in[l].astype(BF)
        w_in_p = jnp.concatenate(
            [w_in_l[:, :KR_END], jnp.zeros((D, 64), BF), w_in_l[:, KR_END:]], axis=1)
        wuq_p = _pad_mla_heads(mla_w_uq[l], 4).astype(BF)
        wukv = mla_w_ukv[l].astype(BF)
        gains = [g.reshape(1, -1) for g in (
            qn_att[l], kn_att[l], qn_win[l], kn_win[l], qn_na[l], kn_na[l],
            _pad_mla_heads(qn_mla[l], 1), _pad_mla_heads(kn_mla[l], 1), mla_qa_norm[l], mla_kva_norm[l])]

        split = l == 0
        h, proj = _proj_call((x_lat, x_ctx) if split else (X,), norm_mix[l], modl, w_in_p, B, N, tm, P_TOTAL)
        (qa, ka, va, qm, km, vm, qw, kw, vw, qn, kn, vn) = _heads_call(proj, tabs, gains, wuq_p, wukv, B, N, tm)

        sink = win_sink[l].astype(F32)
        o_att = _flash_call(qa, ka, va, B, N, n_ctx, 2, HEAD_DIM, tq, tk, "dense_gqa")
        o_mla = _flash_call(qm, km, vm, B, N, n_ctx, 1, MLA_PAD, 2 * tq, tk, "latent_attn")
        o_win = _window_call(sink, qw, kw, vw, B, N, n_ctx, 2, tq)
        bias = _na_bias_tables(na_rpb[l], N // GRID_W)
        o_na = _na_call(qn, kn, vn, bias, B, N, n_ctx)
        outs = [(o,) for o in (o_att, o_mla, o_win, o_na)]
        if not last:
            outs_c = [
                _ctx_attn_call(sink, qa, ka, va, B, N, n_ctx, 2, HEAD_DIM, HEAD_DIM, False, "ctx_dense_gqa"),
                _ctx_attn_call(sink, qm, km, vm, B, N, n_ctx, 1, MLA_PAD, HEAD_DIM, False, "ctx_latent_attn"),
                _ctx_attn_call(sink, qw, kw, vw, B, N, n_ctx, 2, HEAD_DIM, HEAD_DIM, True, "ctx_window_attn"),
                _ctx_attn_call(sink, qn, kn, vn, B, N, n_ctx, 1, HEAD_DIM, HEAD_DIM, False, "ctx_neighbourhood"),
            ]
            if split:
                outs = [(a[0], b) for a, b in zip(outs, outs_c)]
            else:
                outs = [(jnp.concatenate([a[0], b], axis=0),) for a, b in zip(outs, outs_c)]
        if split:
            xs_res = (x_lat,) if last else (x_lat, x_ctx)
        else:
            xs_res = (X,)

        X, h2, eidx, wts = _merge_call(
            xs_res, h, outs, w_gate[l].astype(BF), b_gate[l].reshape(4, 1, D), w_branch[l].astype(BF),
            w_out[l].astype(BF), modl, norm_ffn[l], wr_p, br_p, rows, B, N, tm, tn_merge)

        pos, tile_expert, n_used, fill = _route_positions(eidx[:2].T, tg)
        n_slots = 2 * rows + N_EXPERTS * tg
        xs = _scatter_call(h2, pos, fill, n_slots, tm_moe, tg)
        ys = _experts_call(xs, tile_expert, n_used, moe_w1, moe_w3, moe_w2, l, tg)
        X = _combine_call(X, ys, pos, wts, modl, B, N, tm_moe)

    return X.reshape(B, N, D)
```

```python
import functools

import numpy as np
import jax
import jax.numpy as jnp
from jax import lax
from jax.experimental import pallas as pl
from jax.experimental.pallas import tpu as pltpu

BF = jnp.bfloat16
F32 = jnp.float32

GRID_W = 64
HEAD_DIM = 128
ROPE_THETA = 10000.0
EPS = 1e-6
NEG = -1e30
LOG2E = 1.4426950408889634
WINDOW = 128
NA_WIN_H = 8
NA_WIN_W = 16
N_EXPERTS = 16
N_GROUPS = 4
MLA_QK = 192
MLA_PAD = 256
LANES = 128
VMEM_LIMIT = 56 * 1024 * 1024
VMEM_LIMIT_MERGE = 62 * 1024 * 1024

P_AQ, P_AK, P_AV = 0, 512, 768
P_MCQ, P_MCKV, P_MKR = 1024, 1408, 1664
P_WQ, P_WK, P_WV = 1792, 2304, 2560
P_NQ, P_NK, P_NV = 2816, 3328, 3840
P_TOTAL = 4352
KR_END = 1728


def _cparams(sem, vmem_limit=VMEM_LIMIT):
    return pltpu.CompilerParams(dimension_semantics=sem, vmem_limit_bytes=vmem_limit)


def _sigmoid(z):
    return 1.0 / (1.0 + jnp.exp(-z))


def _dot(a, b):
    return jnp.dot(a, b, preferred_element_type=F32)


def _dot_nt(a, b):
    return lax.dot_general(a, b, (((1,), (1,)), ((), ())), preferred_element_type=F32)


def _mod_kernel(c_ref, w_ref, b_ref, o_ref):
    c = c_ref[...]
    s = c * _sigmoid(c)
    o_ref[...] = _dot(s.astype(BF), w_ref[...].astype(BF)) + b_ref[...]


def _mod_call(cvec, w_mod, b_mod, tn):
    L, D, D6 = w_mod.shape
    return pl.pallas_call(
        _mod_kernel,
        grid=(L, D6 // tn),
        in_specs=[
            pl.BlockSpec((8, D), lambda l, j: (0, 0)),
            pl.BlockSpec((None, D, tn), lambda l, j: (l, 0, j)),
            pl.BlockSpec((None, 1, tn), lambda l, j: (l, 0, j)),
        ],
        out_specs=pl.BlockSpec((None, 8, tn), lambda l, j: (l, 0, j)),
        out_shape=jax.ShapeDtypeStruct((L, 8, D6), F32),
        compiler_params=_cparams(("parallel", "parallel")),
        name="adaln_mod",
    )(cvec, w_mod, b_mod.reshape(L, 1, D6))


def _mod_spec(comp, tiles_per_batch, n_batch, D):
    return pl.BlockSpec(
        (None, None, 1, D),
        lambda i, *_: (jnp.minimum(i // tiles_per_batch, n_batch), comp, 0, 0))


def _proj_kernel(*refs, n_lat_tiles):
    if n_lat_tiles is None:
        x_ref, g_ref, sh_ref, sc_ref, w_ref, h_ref, o_ref, h_scr = refs
    else:
        xl_ref, xc_ref, g_ref, sh_ref, sc_ref, w_ref, h_ref, o_ref, h_scr = refs

    @pl.when(pl.program_id(1) == 0)
    def _():
        if n_lat_tiles is None:
            x = x_ref[...]
        else:
            x = jnp.where(pl.program_id(0) < n_lat_tiles, xl_ref[...], xc_ref[...])
        ms = jnp.mean(x * x, axis=-1, keepdims=True)
        xn = x * lax.rsqrt(ms + EPS) * g_ref[...]
        h = (xn * (1.0 + sc_ref[...]) + sh_ref[...]).astype(BF)
        h_scr[...] = h
        h_ref[...] = h

    o_ref[...] = _dot(h_scr[...], w_ref[...])


def _split_row_specs(arrays, tm, width, single_buffer=False):
    mode = dict(pipeline_mode=pl.Buffered(1)) if single_buffer else {}
    if len(arrays) == 1:
        return [pl.BlockSpec((tm, width), lambda i, j: (i, 0), **mode)], None
    n_lat_tiles = arrays[0].shape[0] // tm
    return [pl.BlockSpec((tm, width), lambda i, j: (jnp.minimum(i, n_lat_tiles - 1), 0), **mode),
            pl.BlockSpec((tm, width), lambda i, j: (jnp.maximum(i - n_lat_tiles, 0), 0),
                         pipeline_mode=pl.Buffered(1))], n_lat_tiles


def _proj_call(xs, norm_g, modl, w_in_p, n_batch, n_lat, tm, tn):
    D = xs[0].shape[1]
    R = sum(a.shape[0] for a in xs)
    P = w_in_p.shape[1]
    tpb = n_lat // tm
    if tn == P:
        w_spec = pl.BlockSpec((D, tn), lambda i, j: (0, j), pipeline_mode=pl.Buffered(1))
    else:
        w_spec = pl.BlockSpec((D, tn), lambda i, j: (0, j))
    row_spec = pl.BlockSpec((tm, D), lambda i, j: (i, 0))
    out_specs = [row_spec, pl.BlockSpec((tm, tn), lambda i, j: (i, j))]
    out_shape = [jax.ShapeDtypeStruct((R, D), BF), jax.ShapeDtypeStruct((R, P), F32)]
    x_specs, n_lat_tiles = _split_row_specs(xs, tm, D)
    return pl.pallas_call(
        functools.partial(_proj_kernel, n_lat_tiles=n_lat_tiles),
        grid=(R // tm, P // tn),
        in_specs=x_specs + [
            pl.BlockSpec((1, D), lambda i, j: (0, 0)),
            _mod_spec(0, tpb, n_batch, D),
            _mod_spec(1, tpb, n_batch, D),
            w_spec,
        ],
        out_specs=out_specs,
        out_shape=out_shape,
        scratch_shapes=[pltpu.VMEM((tm, D), BF)],
        compiler_params=_cparams(("parallel", "arbitrary")),
        name="norm_mod_proj",
    )(*xs, norm_g.reshape(1, D), modl, modl, w_in_p)


def _sumsq(x):
    xx = None
    for c in range(x.shape[1] // LANES):
        b = x[:, c * LANES:(c + 1) * LANES]
        xx = b * b if xx is None else xx + b * b
    hi = xx.astype(BF)
    lo = (xx - hi.astype(F32)).astype(BF)
    ones = jnp.ones((LANES, LANES), BF)
    return _dot(hi, ones) + _dot(lo, ones)


def _rms(x, g, n):
    r = lax.rsqrt(_sumsq(x) / n + EPS)
    blocks = [x[:, c * LANES:(c + 1) * LANES] * r * g[:, c * LANES:(c + 1) * LANES]
              for c in range(x.shape[1] // LANES)]
    return blocks[0] if len(blocks) == 1 else jnp.concatenate(blocks, axis=1)


def _rope(x, cos, sin_signed, half):
    lane = lax.broadcasted_iota(jnp.int32, x.shape, 1)
    first = (lane & (2 * half - 1)) < half
    xr = jnp.where(first, pltpu.roll(x, LANES - half, 1), pltpu.roll(x, half, 1))
    return x * cos + xr * sin_signed


def _heads_kernel(p_ref, ca_ref, sa_ref, cm_ref, sm_ref,
                  qa_g, ka_g, qw_g, kw_g, qn_g, kn_g, qm_g, km_g, cqn_g, ckvn_g, wuq_ref, wukv_ref,
                  qa_ref, ka_ref, va_ref, qm_ref, km_ref, vm_ref,
                  qw_ref, kw_ref, vw_ref, qn_ref, kn_ref, vn_ref):
    hd = HEAD_DIM
    sc128 = HEAD_DIM ** -0.5 * LOG2E
    sc192 = MLA_QK ** -0.5 * LOG2E
    ca, sa, cm, sm = ca_ref[...], sa_ref[...], cm_ref[...], sm_ref[...]

    def sl(off, h, w=hd):
        return p_ref[:, off + h * w: off + (h + 1) * w]

    ones = jnp.ones((p_ref.shape[0], hd), BF)

    def put_values(v_out, v_off, n_heads):
        for h in range(n_heads):
            v_out[:, 2 * h * hd:(2 * h + 1) * hd] = sl(v_off, h).astype(BF)
            v_out[:, (2 * h + 1) * hd:(2 * h + 2) * hd] = ones

    for (q_off, k_off, v_off, qg, kg, q_out, k_out, v_out) in (
            (P_AQ, P_AK, P_AV, qa_g, ka_g, qa_ref, ka_ref, va_ref),
            (P_WQ, P_WK, P_WV, qw_g, kw_g, qw_ref, kw_ref, vw_ref)):
        for h in range(4):
            q = _rope(_rms(sl(q_off, h), qg[...], hd), ca, sa, 32)
            q_out[:, h * hd:(h + 1) * hd] = (q * sc128).astype(BF)
        for h in range(2):
            k = _rope(_rms(sl(k_off, h), kg[...], hd), ca, sa, 32)
            k_out[:, h * hd:(h + 1) * hd] = k.astype(BF)
        put_values(v_out, v_off, 2)

    for h in range(4):
        qn_ref[:, h * hd:(h + 1) * hd] = (_rms(sl(P_NQ, h), qn_g[...], hd) * sc128).astype(BF)
        kn_ref[:, h * hd:(h + 1) * hd] = _rms(sl(P_NK, h), kn_g[...], hd).astype(BF)
    put_values(vn_ref, P_NV, 4)

    cq = _rms(p_ref[:, P_MCQ:P_MCQ + 384], cqn_g[...], 384)
    qf = _dot(cq.astype(BF), wuq_ref[...])
    ckv = _rms(p_ref[:, P_MCKV:P_MCKV + 256], ckvn_g[...], 256)
    kvf = _dot(ckv.astype(BF), wukv_ref[...])
    kr = p_ref[:, P_MKR:P_MKR + hd]
    kr_ss = _sumsq(kr)
    qg0, qg1 = qm_g[:, :hd], qm_g[:, hd:]
    kg0, kg1 = km_g[:, :hd], km_g[:, hd:]
    for h in range(4):
        q0 = qf[:, h * MLA_PAD: h * MLA_PAD + hd]
        q1 = qf[:, h * MLA_PAD + hd: (h + 1) * MLA_PAD]
        r = lax.rsqrt(_sumsq(qf[:, h * MLA_PAD:(h + 1) * MLA_PAD]) / MLA_QK + EPS)
        qm_ref[:, h * MLA_PAD: h * MLA_PAD + hd] = (q0 * r * qg0 * sc192).astype(BF)
        qm_ref[:, h * MLA_PAD + hd: (h + 1) * MLA_PAD] = (
            _rope(q1 * r * qg1, cm, sm, 16) * sc192).astype(BF)
        k0 = kvf[:, h * 2 * hd: h * 2 * hd + hd]
        r = lax.rsqrt((_sumsq(k0) + kr_ss) / MLA_QK + EPS)
        km_ref[:, h * MLA_PAD: h * MLA_PAD + hd] = (k0 * r * kg0).astype(BF)
        km_ref[:, h * MLA_PAD + hd: (h + 1) * MLA_PAD] = _rope(kr * r * kg1, cm, sm, 16).astype(BF)
        vm_ref[:, 2 * h * hd:(2 * h + 1) * hd] = kvf[:, h * 2 * hd + hd: (h + 1) * 2 * hd].astype(BF)
        vm_ref[:, (2 * h + 1) * hd:(2 * h + 2) * hd] = ones


def _heads_call(proj, tabs, gains, wuq_p, wukv, n_batch, n_lat, tm):
    R, P = proj.shape
    tpb = n_lat // tm
    tab_spec = pl.BlockSpec((tm, LANES), lambda i: (jnp.where(i < n_batch * tpb, i % tpb, tpb), 0))

    def full(a):
        return pl.BlockSpec(a.shape, lambda i: (0,) * a.ndim)

    widths = (512, 256, 512, 4 * MLA_PAD, 4 * MLA_PAD, 1024, 512, 256, 512, 512, 512, 1024)
    return pl.pallas_call(
        _heads_kernel,
        grid=(R // tm,),
        in_specs=[pl.BlockSpec((tm, P), lambda i: (i, 0))] + [tab_spec] * 4
        + [full(g) for g in gains] + [full(wuq_p), full(wukv)],
        out_specs=[pl.BlockSpec((tm, w), lambda i: (i, 0)) for w in widths],
        out_shape=[jax.ShapeDtypeStruct((R, w), BF) for w in widths],
        compiler_params=_cparams(("parallel",)),
        name="head_prep",
    )(proj, *tabs, *gains, wuq_p, wukv)


def _flash_kernel(q_ref, k_ref, v_ref, kc_ref, vc_ref, o_ref, q_scr, s_scr, m_scr, acc_scr,
                  *, G, d, dv, w, unroll):
    tq = q_ref.shape[0]
    n_blk = k_ref.shape[0] // w
    for g in range(G):
        q_scr[g * tq:(g + 1) * tq, :] = q_ref[:, g * d:(g + 1) * d]
    m_scr[...] = jnp.full_like(m_scr, NEG)
    acc_scr[...] = jnp.zeros_like(acc_scr)

    def scores(k):
        return _dot_nt(q_scr[...], k)

    def absorb(s, v):
        m_prev = m_scr[...]
        m_next = jnp.maximum(m_prev, s.max(axis=1, keepdims=True))
        alpha = jnp.exp2(m_prev - m_next)
        p = jnp.concatenate(
            [jnp.exp2(s[:, c * LANES:(c + 1) * LANES] - m_next) for c in range(s.shape[1] // LANES)],
            axis=1)
        pv = _dot(p.astype(BF), v)
        acc_scr[...] = jnp.concatenate([alpha, alpha], axis=1) * acc_scr[...] + pv
        m_scr[...] = m_next

    n_c = kc_ref.shape[0]
    s_scr[1, :, 0:n_c] = scores(kc_ref[...])
    s_scr[0] = scores(k_ref[0:w, :])
    absorb(s_scr[1, :, 0:n_c], vc_ref[...])

    def group(jj, carry):
        base = jj * (unroll * w)
        for u in range(unroll):
            cur = pl.multiple_of(base + u * w, w)
            nxt = pl.multiple_of(jnp.minimum(base + (u + 1) * w, (n_blk - 1) * w), w)
            s_scr[(u + 1) % 2] = scores(k_ref[pl.ds(nxt, w), :])
            absorb(s_scr[u % 2], v_ref[pl.ds(cur, w), :])
        return carry

    lax.fori_loop(0, n_blk // unroll, group, 0)

    acc = acc_scr[...]
    o = acc[:, :dv] / acc[:, dv:]
    for g in range(G):
        o_ref[:, g * dv:(g + 1) * dv] = o[g * tq:(g + 1) * tq].astype(BF)


def _flash_call(q, k, v, n_batch, n_lat, n_ctx, G, d, tq, w, name):
    dv = LANES
    hkv = k.shape[1] // d
    nq = n_lat // tq
    cb = (n_batch * n_lat) // n_ctx
    n_blk = n_lat // w
    unroll = next(u for u in (8, 4, 2) if n_blk % u == 0)
    assert n_ctx <= w
    return pl.pallas_call(
        functools.partial(_flash_kernel, G=G, d=d, dv=dv, w=w, unroll=unroll),
        grid=(n_batch, hkv, nq),
        in_specs=[
            pl.BlockSpec((tq, G * d), lambda b, h, i: (b * nq + i, h)),
            pl.BlockSpec((n_lat, d), lambda b, h, i: (b, h)),
            pl.BlockSpec((n_lat, 2 * dv), lambda b, h, i: (b, h)),
            pl.BlockSpec((n_ctx, d), lambda b, h, i: (cb + b, h)),
            pl.BlockSpec((n_ctx, 2 * dv), lambda b, h, i: (cb + b, h)),
        ],
        out_specs=pl.BlockSpec((tq, G * dv), lambda b, h, i: (b * nq + i, h)),
        out_shape=jax.ShapeDtypeStruct((n_batch * n_lat, hkv * G * dv), BF),
        scratch_shapes=[pltpu.VMEM((G * tq, d), BF), pltpu.VMEM((2, G * tq, w), F32),
                        pltpu.VMEM((G * tq, LANES), F32), pltpu.VMEM((G * tq, 2 * dv), F32)],
        compiler_params=_cparams(("parallel", "parallel", "parallel")),
        name=name,
    )(q, k, v, k, v)


def _window_kernel(sink_ref, q_ref, kp_ref, km_ref, kn_ref, vp_ref, vm_ref, vn_ref, kc_ref, vc_ref,
                   o_ref, s_scr, sc_scr, *, G):
    hd = HEAD_DIM
    i = pl.program_id(1)
    ni = pl.num_programs(1)
    tq = q_ref.shape[0]
    n_heads = q_ref.shape[1] // hd
    r = lax.broadcasted_iota(jnp.int32, (tq, tq + 2 * WINDOW), 0)
    c = lax.broadcasted_iota(jnp.int32, (tq, tq + 2 * WINDOW), 1)
    rel = c - WINDOW - r
    lo = jnp.where(i > 0, 0, WINDOW)
    hi = jnp.where(i < ni - 1, tq + 2 * WINDOW, tq + WINDOW)
    valid = (jnp.abs(rel) <= WINDOW) & (c >= lo) & (c < hi)

    def kv_cat(p_ref, m_ref, n_ref, kvh, w):
        cols = slice(kvh * w, (kvh + 1) * w)
        return jnp.concatenate([p_ref[:, cols], m_ref[:, cols], n_ref[:, cols]], axis=0)

    for qh in range(n_heads):
        kvh = qh // G
        q = q_ref[:, qh * hd:(qh + 1) * hd]
        s_scr[qh] = jnp.where(valid, _dot_nt(q, kv_cat(kp_ref, km_ref, kn_ref, kvh, hd)), NEG)
        sc_scr[qh] = _dot_nt(q, kc_ref[:, kvh * hd:(kvh + 1) * hd])
    for qh in range(n_heads):
        kvh = qh // G
        s = s_scr[qh]
        sc = sc_scr[qh]
        sk = sink_ref[qh] * LOG2E
        m = jnp.maximum(jnp.max(s, axis=-1, keepdims=True), jnp.max(sc, axis=-1, keepdims=True))
        m = jnp.maximum(m, sk)
        p = jnp.exp2(s - m)
        pc = jnp.exp2(sc - m)
        o = (_dot(p.astype(BF), kv_cat(vp_ref, vm_ref, vn_ref, kvh, 2 * hd))
             + _dot(pc.astype(BF), vc_ref[:, kvh * 2 * hd:(kvh + 1) * 2 * hd]))
        l = o[:, hd:] + jnp.exp2(sk - m)
        o_ref[:, qh * hd:(qh + 1) * hd] = (o[:, :hd] / l).astype(BF)


def _window_call(sink, q, k, v, n_batch, n_lat, n_ctx, G, tq):
    hd = HEAD_DIM
    hkv = k.shape[1] // hd
    nq = n_lat // tq
    bpt = tq // WINDOW
    nblk = k.shape[0] // WINDOW
    cb = (n_batch * n_lat) // n_ctx

    def prev(b, i):
        return (jnp.maximum((b * nq + i) * bpt - 1, 0), 0)

    def nxt(b, i):
        return (jnp.minimum((b * nq + i + 1) * bpt, nblk - 1), 0)

    def main(b, i):
        return (b * nq + i, 0)

    def ctxb(b, i):
        return (cb + b, 0)

    kw, vw, qw = hkv * hd, hkv * 2 * hd, hkv * G * hd
    return pl.pallas_call(
        functools.partial(_window_kernel, G=G),
        grid=(n_batch, nq),
        in_specs=[
            pl.BlockSpec(memory_space=pltpu.SMEM),
            pl.BlockSpec((tq, qw), main),
            pl.BlockSpec((WINDOW, kw), prev), pl.BlockSpec((tq, kw), main), pl.BlockSpec((WINDOW, kw), nxt),
            pl.BlockSpec((WINDOW, vw), prev), pl.BlockSpec((tq, vw), main), pl.BlockSpec((WINDOW, vw), nxt),
            pl.BlockSpec((n_ctx, kw), ctxb), pl.BlockSpec((n_ctx, vw), ctxb),
        ],
        out_specs=pl.BlockSpec((tq, qw), main),
        out_shape=jax.ShapeDtypeStruct((n_batch * n_lat, qw), BF),
        scratch_shapes=[pltpu.VMEM((hkv * G, tq, tq + 2 * WINDOW), F32), pltpu.VMEM((hkv * G, tq, n_ctx), F32)],
        compiler_params=_cparams(("parallel", "parallel")),
        name="window_attn",
    )(sink, q, k, k, k, v, v, v, k, v)


NA_TQ = 8 * GRID_W
NA_KB = 4 * GRID_W
NA_NKB = 4


def _na_kernel(q_ref, k0, k1, k2, k3, v0, v1, v2, v3, kc_ref, vc_ref, bias_ref, o_ref, s_scr, sc_scr):
    hd = HEAD_DIM
    n_heads = q_ref.shape[1] // hd

    def cat(refs, h, w):
        return jnp.concatenate([r[:, h * w:(h + 1) * w] for r in refs], axis=0)

    for h in range(n_heads):
        q = q_ref[:, h * hd:(h + 1) * hd]
        s_scr[h] = _dot_nt(q, cat((k0, k1, k2, k3), h, hd)) + bias_ref[h]
        sc_scr[h] = _dot_nt(q, kc_ref[:, h * hd:(h + 1) * hd])
    for h in range(n_heads):
        s = s_scr[h]
        sc = sc_scr[h]
        m = jnp.maximum(jnp.max(s, axis=-1, keepdims=True), jnp.max(sc, axis=-1, keepdims=True))
        p = jnp.exp2(s - m)
        pc = jnp.exp2(sc - m)
        o = (_dot(p.astype(BF), cat((v0, v1, v2, v3), h, 2 * hd))
             + _dot(pc.astype(BF), vc_ref[:, h * 2 * hd:(h + 1) * 2 * hd]))
        o_ref[:, h * hd:(h + 1) * hd] = (o[:, :hd] / o[:, hd:]).astype(BF)


def _na_bias_tables(rpb, n_rows):
    H = rpb.shape[0]
    J = n_rows // 8
    nb = n_rows // 4
    n_dy, n_dx = 2 * NA_WIN_H - 1, 2 * NA_WIN_W - 1
    cq = np.arange(GRID_W)
    kc = np.arange(GRID_W)
    cs = np.clip(cq - NA_WIN_W // 2, 0, GRID_W - NA_WIN_W)
    dx = kc[None, :] - cq[:, None] + NA_WIN_W - 1
    col_ok = (kc[None, :] >= cs[:, None]) & (kc[None, :] < cs[:, None] + NA_WIN_W)
    dx1h = np.zeros((n_dx, GRID_W * GRID_W), np.float32)
    dx1h[np.clip(dx, 0, n_dx - 1).reshape(-1), np.arange(GRID_W * GRID_W)] = 1.0
    blocks = jnp.einsum("hyx,xb->hyb", rpb.astype(F32) * LOG2E, jnp.asarray(dx1h),
                        precision=lax.Precision.HIGHEST)
    blocks = jnp.where(jnp.asarray(col_ok.reshape(-1))[None, None, :], blocks, NEG)
    blocks = jnp.concatenate([blocks, jnp.full((H, 1, GRID_W * GRID_W), NEG, F32)], axis=1)
    blocks = blocks.reshape(H, n_dy + 1, GRID_W, GRID_W)
    idx = np.zeros((3, 8, 4 * NA_NKB), np.int32)
    for v, jv in enumerate((0, min(1, J - 1), J - 1)):
        r = 8 * jv + np.arange(8)
        kb_un = 2 * jv - 1 + np.arange(NA_NKB)
        kb = np.clip(kb_un, 0, nb - 1)
        krow = (4 * kb[:, None] + np.arange(4)[None, :]).reshape(-1)
        krow_dup = np.repeat(kb != kb_un, 4)
        rs = np.clip(r - NA_WIN_H // 2, 0, n_rows - NA_WIN_H)
        row_ok = (krow[None, :] >= rs[:, None]) & (krow[None, :] < rs[:, None] + NA_WIN_H) & ~krow_dup[None, :]
        dy = krow[None, :] - r[:, None] + NA_WIN_H - 1
        idx[v] = np.where(row_ok, np.clip(dy, 0, n_dy - 1), n_dy)
    t = jnp.take(blocks, jnp.asarray(idx.reshape(-1)), axis=1)
    t = t.reshape(H, 3, 8, 4 * NA_NKB, GRID_W, GRID_W).transpose(0, 1, 2, 4, 3, 5)
    return t.reshape(H, 3, NA_TQ, NA_NKB * NA_KB)


def _na_call(q, k, v, bias, n_batch, n_lat, n_ctx):
    hd = HEAD_DIM
    H = q.shape[1] // hd
    J = n_lat // NA_TQ
    nb = n_lat // NA_KB
    cb = (n_batch * n_lat) // n_ctx

    def kblk(t):
        return lambda b, j: (b * nb + jnp.clip(2 * j - 1 + t, 0, nb - 1), 0)

    def qmap(b, j):
        return (b * J + j, 0)

    def ctxb(b, j):
        return (cb + b, 0)

    def bmap(b, j):
        return (0, jnp.where(j == 0, 0, jnp.where(j == J - 1, 2, 1)), 0, 0)

    n_keys = NA_NKB * NA_KB
    return pl.pallas_call(
        _na_kernel,
        grid=(n_batch, J),
        in_specs=[pl.BlockSpec((NA_TQ, H * hd), qmap)]
        + [pl.BlockSpec((NA_KB, H * hd), kblk(t)) for t in range(NA_NKB)]
        + [pl.BlockSpec((NA_KB, H * 2 * hd), kblk(t)) for t in range(NA_NKB)]
        + [pl.BlockSpec((n_ctx, H * hd), ctxb), pl.BlockSpec((n_ctx, H * 2 * hd), ctxb),
           pl.BlockSpec((H, None, NA_TQ, n_keys), bmap)],
        out_specs=pl.BlockSpec((NA_TQ, H * hd), qmap),
        out_shape=jax.ShapeDtypeStruct((n_batch * n_lat, H * hd), BF),
        scratch_shapes=[pltpu.VMEM((H, NA_TQ, n_keys), F32), pltpu.VMEM((H, NA_TQ, n_ctx), F32)],
        compiler_params=_cparams(("parallel", "parallel")),
        name="neighbourhood_attn",
    )(q, k, k, k, k, v, v, v, v, k, v, bias)


def _ctx_attn_kernel(sink_ref, q_ref, k_ref, v_ref, o_ref, *, use_sink):
    s = _dot_nt(q_ref[...], k_ref[...])
    m = jnp.max(s, axis=-1, keepdims=True)
    if use_sink:
        sk = sink_ref[pl.program_id(1)] * LOG2E
        m = jnp.maximum(m, sk)
    p = jnp.exp2(s - m)
    l = jnp.sum(p, axis=-1, keepdims=True)
    if use_sink:
        l = l + jnp.exp2(sk - m)
    o_ref[...] = (_dot(p.astype(BF), v_ref[...]) / l).astype(BF)


def _ctx_attn_call(sink, q, k, v, n_batch, n_lat, n_ctx, G, d, dv, use_sink, name):
    v_stride = 2
    H = q.shape[1] // d
    cb = (n_batch * n_lat) // n_ctx
    return pl.pallas_call(
        functools.partial(_ctx_attn_kernel, use_sink=use_sink),
        grid=(n_batch, H),
        in_specs=[
            pl.BlockSpec(memory_space=pltpu.SMEM),
            pl.BlockSpec((n_ctx, d), lambda b, h: (cb + b, h)),
            pl.BlockSpec((n_ctx, d), lambda b, h: (cb + b, h // G)),
            pl.BlockSpec((n_ctx, dv), lambda b, h: (cb + b, (h // G) * v_stride)),
        ],
        out_specs=pl.BlockSpec((n_ctx, dv), lambda b, h: (b, h)),
        out_shape=jax.ShapeDtypeStruct((n_batch * n_ctx, H * dv), BF),
        compiler_params=_cparams(("parallel", "parallel")),
        name=name,
    )(sink, q, k, v)


def _merge_kernel(*refs, n_lat_tiles):
    n_row_in = 5 if n_lat_tiles is None else 10
    row_refs = refs[:1 + n_row_in]
    (wg_ref, bg_ref, wb_ref, wo_ref, g1_ref, nf_ref, sh2_ref, sc2_ref, wr_ref, br_ref,
     xo_ref, h2_ref, ei_ref, wt_ref, y_scr) = refs[1 + n_row_in:]
    if n_lat_tiles is None:
        x_ref, h_ref = row_refs[0], row_refs[1]
        mixer_rows = [r.__getitem__ for r in row_refs[2:]]
        x_rows = x_ref.__getitem__
    else:
        h_ref = row_refs[2]
        is_lat = pl.program_id(0) < n_lat_tiles

        def pick(l_ref, c_ref):
            return lambda idx: jnp.where(is_lat, l_ref[idx], c_ref[idx])

        x_rows = pick(row_refs[0], row_refs[1])
        mixer_rows = [pick(row_refs[3 + 2 * i], row_refs[4 + 2 * i]) for i in range(4)]
    j = pl.program_id(1)

    @pl.when(j == 0)
    def _():
        y_scr[...] = jnp.zeros_like(y_scr)

    h = h_ref[...]
    acc = None
    for i, rows_of in enumerate(mixer_rows):
        gate = _sigmoid(_dot(h, wg_ref[i]) + bg_ref[i])
        y = gate * _dot(rows_of(...), wb_ref[i])
        acc = y if acc is None else acc + y
    y_scr[...] += _dot(acc.astype(BF), wo_ref[...])

    @pl.when(j == pl.num_programs(1) - 1)
    def _():
        x = x_rows(...) + g1_ref[...] * y_scr[...]
        xo_ref[...] = x
        ms = jnp.mean(x * x, axis=-1, keepdims=True)
        h2 = x * lax.rsqrt(ms + EPS) * nf_ref[...]
        h2 = h2 * (1.0 + sc2_ref[...]) + sh2_ref[...]
        h2_ref[...] = h2
        hi = h2.astype(BF)
        lo = (h2 - hi.astype(F32)).astype(BF)
        both = _dot(hi, wr_ref[...])
        logits = both[:, :LANES] + (_dot(lo, wr_ref[:, :LANES]) + both[:, LANES:])
        lt = logits.T[:N_EXPERTS, :]
        scores = _sigmoid(lt)
        biased = scores + br_ref[...]
        row = lax.broadcasted_iota(jnp.int32, lt.shape, 0)
        row_f = row.astype(F32)
        ninf = -jnp.inf

        def top2(vals):
            t1 = jnp.max(vals, axis=0, keepdims=True)
            i1 = jnp.min(jnp.where(vals == t1, row_f, float(N_EXPERTS)), axis=0, keepdims=True)
            vals2 = jnp.where(row_f == i1, ninf, vals)
            t2 = jnp.max(vals2, axis=0, keepdims=True)
            i2 = jnp.min(jnp.where(vals2 == t2, row_f, float(N_EXPERTS)), axis=0, keepdims=True)
            return t1, i1, t2, i2

        per = N_EXPERTS // N_GROUPS
        best, gi = None, None
        for g in range(N_GROUPS):
            ing = (row >= g * per) & (row < (g + 1) * per)
            t1, _, t2, _ = top2(jnp.where(ing, biased, ninf))
            gs = t1 + t2
            if best is None:
                best, gi = gs, jnp.zeros_like(gs)
            else:
                better = gs > best
                best = jnp.where(better, gs, best)
                gi = jnp.where(better, float(g), gi)
        row_grp = (row >> 2).astype(F32)
        _, i1, _, i2 = top2(jnp.where(row_grp == gi, biased, NEG))
        w1 = jnp.sum(jnp.where(row_f == i1, scores, 0.0), axis=0, keepdims=True)
        w2 = jnp.sum(jnp.where(row_f == i2, scores, 0.0), axis=0, keepdims=True)
        den = w1 + w2
        out_row = lax.broadcasted_iota(jnp.int32, ei_ref.shape, 0)
        ei_ref[...] = jnp.where(out_row == 0, i1, jnp.where(out_row == 1, i2, 0.0)).astype(jnp.int32)
        wt_ref[...] = jnp.where(out_row == 0, w1 / den, jnp.where(out_row == 1, w2 / den, 0.0))


def _merge_call(xs, h, outs, wg, bg, wb, wo, modl, norm_f, wr_p, br_p, rows, n_batch, n_lat, tm, tn):
    D = xs[0].shape[1]
    bw = wb.shape[1]
    tpb = n_lat // tm

    def rowmap(i, j):
        return (i, 0)

    x_specs, n_lat_tiles = _split_row_specs(xs, tm, D)
    o_specs, o_args = [], []
    for o in outs:
        assert len(o) == len(xs)
        o_specs += _split_row_specs(o, tm, bw)[0]
        o_args += list(o)
    return pl.pallas_call(
        functools.partial(_merge_kernel, n_lat_tiles=n_lat_tiles),
        grid=(rows // tm, D // tn),
        in_specs=x_specs + [pl.BlockSpec((tm, D), rowmap)] + o_specs + [
            pl.BlockSpec((4, D, tn), lambda i, j: (0, 0, j)),
            pl.BlockSpec((4, 1, tn), lambda i, j: (0, 0, j)),
            pl.BlockSpec((4, bw, tn), lambda i, j: (0, 0, j)),
            pl.BlockSpec((tn, D), lambda i, j: (j, 0)),
            _mod_spec(2, tpb, n_batch, D),
            pl.BlockSpec((1, D), lambda i, j: (0, 0)),
            _mod_spec(3, tpb, n_batch, D),
            _mod_spec(4, tpb, n_batch, D),
            pl.BlockSpec((D, 2 * LANES), lambda i, j: (0, 0)),
            pl.BlockSpec((N_EXPERTS, 1), lambda i, j: (0, 0)),
        ],
        out_specs=[
            pl.BlockSpec((tm, D), rowmap),
            pl.BlockSpec((tm, D), rowmap),
            pl.BlockSpec((8, tm), lambda i, j: (0, i)),
            pl.BlockSpec((8, tm), lambda i, j: (0, i)),
        ],
        out_shape=[
            jax.ShapeDtypeStruct((rows, D), F32),
            jax.ShapeDtypeStruct((rows, D), F32),
            jax.ShapeDtypeStruct((8, rows), jnp.int32),
            jax.ShapeDtypeStruct((8, rows), F32),
        ],
        scratch_shapes=[pltpu.VMEM((tm, D), F32)],
        compiler_params=_cparams(("parallel", "arbitrary"), VMEM_LIMIT_MERGE),
        name="merge_residual_router",
    )(*xs, h, *o_args, wg, bg, wb, wo, modl, norm_f.reshape(1, D), modl, modl, wr_p, br_p)


def _row_dma_issue(n_rows, make_copy):
    def issue(r, carry):
        for k in range(2):
            make_copy(r, k).start(priority=k)
        return carry

    lax.fori_loop(0, n_rows, issue, 0, unroll=8)


def _row_dma_drain(n_rows, make_copy):
    def drain(r, carry):
        for k in range(2):
            make_copy(r, k).wait()
        return carry

    lax.fori_loop(0, n_rows, drain, 0, unroll=8)


def _row_dma_loop(n_rows, make_copy):
    _row_dma_issue(n_rows, make_copy)
    _row_dma_drain(n_rows, make_copy)


def _scatter_kernel(fill_ref, pos_ref, h_ref, xs_ref, zbuf, sem, zsem):
    tg = zbuf.shape[0]

    @pl.when(pl.program_id(0) == 0)
    def _():
        zbuf[...] = jnp.zeros_like(zbuf)

        def fill_copy(t):
            return pltpu.make_async_copy(zbuf, xs_ref.at[pl.ds(pl.multiple_of(t * tg, tg), tg), :], zsem)

        def start_fill(t, carry):
            @pl.when(fill_ref[t] != 0)
            def _():
                fill_copy(t).start()
            return carry

        def wait_fill(t, carry):
            @pl.when(fill_ref[t] != 0)
            def _():
                fill_copy(t).wait()
            return carry

        lax.fori_loop(0, fill_ref.shape[0], start_fill, 0)
        lax.fori_loop(0, fill_ref.shape[0], wait_fill, 0)

    def make_copy(r, k):
        return pltpu.make_async_copy(h_ref.at[pl.ds(r, 1), :],
                                     xs_ref.at[pl.ds(pos_ref[2 * r + k], 1), :], sem)

    _row_dma_loop(h_ref.shape[0], make_copy)


def _scatter_call(h2, pos, fill, n_slots, tm, tg):
    rows, D = h2.shape
    grid_spec = pltpu.PrefetchScalarGridSpec(
        num_scalar_prefetch=1,
        grid=(rows // tm,),
        in_specs=[
            pl.BlockSpec((2 * tm,), lambda i, f: (i,), memory_space=pltpu.SMEM),
            pl.BlockSpec((tm, D), lambda i, f: (i, 0)),
        ],
        out_specs=pl.BlockSpec(memory_space=pl.ANY),
        scratch_shapes=[pltpu.VMEM((tg, D), F32), pltpu.SemaphoreType.DMA(()), pltpu.SemaphoreType.DMA(())],
    )
    return pl.pallas_call(
        _scatter_kernel,
        grid_spec=grid_spec,
        out_shape=jax.ShapeDtypeStruct((n_slots, D), F32),
        compiler_params=_cparams(("arbitrary",)),
        name="moe_scatter_rows",
    )(fill, pos, h2)


def _experts_kernel(te_ref, nu_ref, x_ref, w1_ref, w3_ref, w2_ref, y_ref, w1_scr, w3_scr, w2_scr):
    i = pl.program_id(0)

    @pl.when((i == 0) | (te_ref[i] != te_ref[jnp.maximum(i - 1, 0)]))
    def _():
        w1_scr[...] = w1_ref[...].astype(BF)
        w3_scr[...] = w3_ref[...].astype(BF)
        w2_scr[...] = w2_ref[...].astype(BF)

    @pl.when(i < nu_ref[0])
    def _():
        x = x_ref[...].astype(BF)
        a = _dot(x, w1_scr[...])
        b = _dot(x, w3_scr[...])
        hid = (a * _sigmoid(a)) * b
        y_ref[...] = _dot(hid.astype(BF), w2_scr[...])

    @pl.when(i >= nu_ref[0])
    def _():
        y_ref[...] = jnp.zeros_like(y_ref)


def _experts_call(xs, tile_expert, n_used, w1, w3, w2, layer, tg):
    S, D = xs.shape
    de = w1.shape[3]
    grid_spec = pltpu.PrefetchScalarGridSpec(
        num_scalar_prefetch=2,
        grid=(S // tg,),
        in_specs=[
            pl.BlockSpec((tg, D), lambda i, te, nu: (i, 0)),
            pl.BlockSpec((None, None, D, de), lambda i, te, nu: (layer, te[i], 0, 0)),
            pl.BlockSpec((None, None, D, de), lambda i, te, nu: (layer, te[i], 0, 0)),
            pl.BlockSpec((None, None, de, D), lambda i, te, nu: (layer, te[i], 0, 0)),
        ],
        out_specs=pl.BlockSpec((tg, D), lambda i, te, nu: (i, 0)),
        scratch_shapes=[pltpu.VMEM((D, de), BF), pltpu.VMEM((D, de), BF), pltpu.VMEM((de, D), BF)],
    )
    return pl.pallas_call(
        _experts_kernel,
        grid_spec=grid_spec,
        out_shape=jax.ShapeDtypeStruct((S, D), F32),
        compiler_params=_cparams(("arbitrary",)),
        name="moe_experts",
    )(tile_expert, n_used, xs, w1, w3, w2)


def _combine_kernel(pos_ref, pos_next_ref, x_ref, wt_ref, g2_ref, ys_ref, o_ref, buf, sem):
    i = pl.program_id(0)
    n = pl.num_programs(0)
    tm = x_ref.shape[0]
    slot = i % 2

    def gather(p_ref, s):
        def make_copy(r, k):
            return pltpu.make_async_copy(ys_ref.at[pl.ds(p_ref[2 * r + k], 1), :],
                                         buf.at[s, k, pl.ds(r, 1), :], sem.at[s])
        return make_copy

    @pl.when(i == 0)
    def _():
        _row_dma_issue(tm, gather(pos_ref, 0))

    @pl.when(i + 1 < n)
    def _():
        _row_dma_issue(tm, gather(pos_next_ref, 1 - slot))

    _row_dma_drain(tm, gather(pos_ref, slot))
    wt = jnp.concatenate([wt_ref[...], jnp.zeros((LANES - 8, tm), F32)], axis=0).T
    moe = wt[:, 0:1] * buf[slot, 0] + wt[:, 1:2] * buf[slot, 1]
    o_ref[...] = x_ref[...] + g2_ref[...] * moe


def _combine_call(X, ys, pos, wts, modl, n_batch, n_lat, tm):
    rows, D = X.shape
    tpb = n_lat // tm
    n_steps = rows // tm
    return pl.pallas_call(
        _combine_kernel,
        grid=(n_steps,),
        in_specs=[
            pl.BlockSpec((2 * tm,), lambda i: (i,), memory_space=pltpu.SMEM),
            pl.BlockSpec((2 * tm,), lambda i: (jnp.minimum(i + 1, n_steps - 1),), memory_space=pltpu.SMEM),
            pl.BlockSpec((tm, D), lambda i: (i, 0)),
            pl.BlockSpec((8, tm), lambda i: (0, i)),
            _mod_spec(5, tpb, n_batch, D),
            pl.BlockSpec(memory_space=pl.ANY),
        ],
        out_specs=pl.BlockSpec((tm, D), lambda i: (i, 0)),
        out_shape=jax.ShapeDtypeStruct((rows, D), F32),
        scratch_shapes=[pltpu.VMEM((2, 2, tm, D), F32), pltpu.SemaphoreType.DMA((2,))],
        compiler_params=_cparams(("arbitrary",)),
        name="moe_combine_residual",
    )(pos, pos, X, wts, modl, ys)


def _route_positions(eidx, tg):
    rows = eidx.shape[0]
    e_flat = eidx.reshape(-1)
    onehot = (e_flat[:, None] == jnp.arange(N_EXPERTS, dtype=jnp.int32)[None, :]).astype(jnp.int32)
    csum = jnp.cumsum(onehot, axis=0)
    counts = csum[-1]
    rank = jnp.sum((csum - onehot) * onehot, axis=1)
    padded = ((counts + tg - 1) // tg) * tg
    ends = jnp.cumsum(padded)
    offsets = ends - padded
    pos = jnp.sum(onehot * offsets[None, :], axis=1) + rank
    n_tiles = (2 * rows + N_EXPERTS * tg) // tg
    tile_start = jnp.arange(n_tiles, dtype=jnp.int32) * tg
    tile_expert = jnp.sum((tile_start[:, None] >= ends[None, :]).astype(jnp.int32), axis=1)
    n_used = (ends[-1] // tg).astype(jnp.int32)
    used = jnp.arange(n_tiles) < n_used
    te_1h = (tile_expert[:, None] == jnp.arange(N_EXPERTS, dtype=jnp.int32)[None, :]).astype(jnp.int32)
    real_end = jnp.sum(te_1h * (offsets + counts)[None, :], axis=1)
    fill = jnp.logical_not(used & (tile_start + tg <= real_end)).astype(jnp.int32)
    last_e = jnp.sum((((n_used - 1) * tg) >= ends).astype(jnp.int32))
    tile_expert = jnp.where(used, tile_expert, last_e).astype(jnp.int32)
    return pos.astype(jnp.int32), tile_expert, n_used.reshape(1), fill


def _rope_tables(n_lat, tm):
    n_rows = n_lat // GRID_W

    def tab(d2, axis):
        freqs = np.float32(ROPE_THETA) ** (-np.arange(d2, dtype=np.float32) / np.float32(d2))
        pos = np.arange(n_rows if axis == 0 else GRID_W, dtype=np.float32)
        ang = pos[:, None] * freqs[None, :]
        shape = (n_rows, GRID_W, d2)
        small = [np.cos(ang).astype(np.float32), np.sin(ang).astype(np.float32)]
        full = [jnp.broadcast_to(jnp.asarray(a)[:, None, :] if axis == 0 else jnp.asarray(a)[None, :, :], shape)
                for a in small]
        return tuple(a.reshape(n_lat, d2) for a in full)

    cr, sr = tab(32, 0)
    cc, sc = tab(32, 1)
    cos_a = jnp.concatenate([cr, cr, cc, cc], axis=-1)
    sin_a = jnp.concatenate([-sr, sr, -sc, sc], axis=-1)
    cr, sr = tab(16, 0)
    cc, sc = tab(16, 1)
    one, zero = jnp.ones((n_lat, 64), F32), jnp.zeros((n_lat, 64), F32)
    cos_m = jnp.concatenate([cr, cr, cc, cc, one], axis=-1)
    sin_m = jnp.concatenate([-sr, sr, -sc, sc, zero], axis=-1)
    ident_c, ident_s = jnp.ones((tm, LANES), F32), jnp.zeros((tm, LANES), F32)
    return tuple(jnp.concatenate([a, b], axis=0) for a, b in
                 ((cos_a, ident_c), (sin_a, ident_s), (cos_m, ident_c), (sin_m, ident_s)))


def _pad_mla_heads(w, n_heads):
    lead = w.shape[:-1]
    w = w.reshape(lead + (n_heads, MLA_QK))
    w = jnp.pad(w, [(0, 0)] * len(lead) + [(0, 0), (0, MLA_PAD - MLA_QK)])
    return w.reshape(lead + (n_heads * MLA_PAD,))


def kernel(x, c, ctx, c_ctx, w_mod, b_mod, norm_mix, norm_ffn, w_in, mla_qa_norm, mla_w_uq, mla_kva_norm,
           mla_w_ukv, qn_att, kn_att, qn_mla, kn_mla, qn_win, kn_win, qn_na, kn_na, win_sink, na_rpb,
           w_branch, w_gate, b_gate, w_out, w_router, b_router, moe_w1, moe_w3, moe_w2):
    B, N, D = x.shape
    n_ctx = ctx.shape[1]
    L = w_mod.shape[0]
    R = B * N + B * n_ctx
    tm = 512
    tm_moe = 256
    tg = 512
    tn_merge = min(256, D // 2)
    tq = min(512, N)
    tk = min(512, N // 2)

    x_lat, x_ctx = x.reshape(B * N, D), ctx.reshape(B * n_ctx, D)
    X = None
    cvec = jnp.zeros((8, D), F32).at[:B].set(c).at[B].set(c_ctx)
    mod = _mod_call(cvec, w_mod, b_mod, min(1024, D)).reshape(L, 8, 6, 1, D)
    tabs = _rope_tables(N, tm)

    wr_hi = w_router.astype(BF)
    wr_lo = (w_router - wr_hi.astype(F32)).astype(BF)
    lane_pad = ((0, 0), (0, LANES - N_EXPERTS))
    wr_p = jnp.concatenate([jnp.pad(wr_hi, lane_pad), jnp.pad(wr_lo, lane_pad)], axis=1)
    br_p = b_router.astype(F32).reshape(N_EXPERTS, 1)

    for l in range(L):
        last = l == L - 1
        rows = B * N if last else R
        modl = mod[l]
        w_in_l = w_in[l].astype(BF)
        w_in_p = jnp.concatenate(
            [w_in_l[:, :KR_END], jnp.zeros((D, 64), BF), w_in_l[:, KR_END:]], axis=1)
        wuq_p = _pad_mla_heads(mla_w_uq[l], 4).astype(BF)
        wukv = mla_w_ukv[l].astype(BF)
        gains = [g.reshape(1, -1) for g in (
            qn_att[l], kn_att[l], qn_win[l], kn_win[l], qn_na[l], kn_na[l],
            _pad_mla_heads(qn_mla[l], 1), _pad_mla_heads(kn_mla[l], 1), mla_qa_norm[l], mla_kva_norm[l])]

        split = l == 0
        h, proj = _proj_call((x_lat, x_ctx) if split else (X,), norm_mix[l], modl, w_in_p, B, N, tm, P_TOTAL)
        (qa, ka, va, qm, km, vm, qw, kw, vw, qn, kn, vn) = _heads_call(proj, tabs, gains, wuq_p, wukv, B, N, tm)

        sink = win_sink[l].astype(F32)
        o_att = _flash_call(qa, ka, va, B, N, n_ctx, 2, HEAD_DIM, tq, tk, "dense_gqa")
        o_mla = _flash_call(qm, km, vm, B, N, n_ctx, 1, MLA_PAD, 2 * tq, tk, "latent_attn")
        o_win = _window_call(sink, qw, kw, vw, B, N, n_ctx, 2, tq)
        bias = _na_bias_tables(na_rpb[l], N // GRID_W)
        o_na = _na_call(qn, kn, vn, bias, B, N, n_ctx)
        outs = [(o,) for o in (o_att, o_mla, o_win, o_na)]
        if not last:
            outs_c = [
                _ctx_attn_call(sink, qa, ka, va, B, N, n_ctx, 2, HEAD_DIM, HEAD_DIM, False, "ctx_dense_gqa"),
                _ctx_attn_call(sink, qm, km, vm, B, N, n_ctx, 1, MLA_PAD, HEAD_DIM, False, "ctx_latent_attn"),
                _ctx_attn_call(sink, qw, kw, vw, B, N, n_ctx, 2, HEAD_DIM, HEAD_DIM, True, "ctx_window_attn"),
                _ctx_attn_call(sink, qn, kn, vn, B, N, n_ctx, 1, HEAD_DIM, HEAD_DIM, False, "ctx_neighbourhood"),
            ]
            if split:
                outs = [(a[0], b) for a, b in zip(outs, outs_c)]
            else:
                outs = [(jnp.concatenate([a[0], b], axis=0),) for a, b in zip(outs, outs_c)]
        if split:
            xs_res = (x_lat,) if last else (x_lat, x_ctx)
        else:
            xs_res = (X,)

        X, h2, eidx, wts = _merge_call(
            xs_res, h, outs, w_gate[l].astype(BF), b_gate[l].reshape(4, 1, D), w_branch[l].astype(BF),
            w_out[l].astype(BF), modl, norm_ffn[l], wr_p, br_p, rows, B, N, tm, tn_merge)

        pos, tile_expert, n_used, fill = _route_positions(eidx[:2].T, tg)
        n_slots = 2 * rows + N_EXPERTS * tg
        xs = _scatter_call(h2, pos, fill, n_slots, tm_moe, tg)
        ys = _experts_call(xs, tile_expert, n_used, moe_w1, moe_w3, moe_w2, l, tg)
        X = _combine_call(X, ys, pos, wts, modl, B, N, tm_moe)

    return X.reshape(B, N, D)
```
